```python
import math
import jax, jax.numpy as jnp
from jax import lax
import numpy as np

D_MODEL = 2048
BATCH = 4
SEQ = 4096
DEPTH = 1

HEAD_DIM = 128
SB_HEADS = 8
NSA_HEADS = 8
NSA_KV_GROUPS = 2
NSA_HPG = NSA_HEADS // NSA_KV_GROUPS
CMP_BLOCK = 32
CMP_STRIDE = 16
SLC_BLOCK = 64
N_SELECT = 16
WINDOW = 512
Q_BLOCK = 128
SLC_Q_BLOCK = 64
D_FF = 5632
NORM_EPS = 1e-6
NEG_INF = -1e30
FORCED_SCORE = 1e9

IN_SIZES = ([SB_HEADS * HEAD_DIM] * 3 + [NSA_HEADS * HEAD_DIM]
            + [NSA_KV_GROUPS * HEAD_DIM] * 6 + [3 * NSA_HEADS, 2 * D_MODEL])
IN_COLS = sum(IN_SIZES)

kernel_name = "hybrid_stickbreaking_nsa_macaron_block"


def rmsnorm(x, g):
    xf = x.astype(jnp.float32)
    y = xf * lax.rsqrt(jnp.mean(xf * xf, axis=-1, keepdims=True) + NORM_EPS)
    return (y * g.astype(jnp.float32)).astype(x.dtype)


def swiglu_ffn(h, w_in, w_out):
    gate, up = jnp.split(h @ w_in, 2, axis=-1)
    return (jax.nn.silu(gate) * up) @ w_out


def alibi_slopes(n):
    return jnp.asarray(2.0 ** (-8.0 * np.arange(1, n + 1) / n), jnp.float32)


def stick_breaking_attention(q, k, v):
    B, S, H, dh = q.shape
    f32 = jnp.float32
    scale = dh ** -0.5
    qf = jnp.transpose(q, (0, 2, 1, 3)).astype(f32)
    kf = jnp.transpose(k, (0, 2, 1, 3)).astype(f32)
    vf = jnp.transpose(v, (0, 2, 1, 3)).astype(f32)
    key_pos = jnp.arange(S)
    n_blocks = S // Q_BLOCK

    def block(i):
        start = i * Q_BLOCK
        qb = lax.dynamic_slice_in_dim(qf, start, Q_BLOCK, axis=2)
        z = jnp.einsum('bhqd,bhkd->bhqk', qb, kf) * scale
        q_pos = start + jnp.arange(Q_BLOCK)
        past = key_pos[None, :] < q_pos[:, None]
        log_beta = jax.nn.log_sigmoid(z)
        log_1m = jnp.where(past, jax.nn.log_sigmoid(-z), 0.0)
        suffix = lax.cumsum(log_1m, axis=3, reverse=True)
        weight = jnp.where(past, jnp.exp(log_beta + suffix - log_1m), 0.0)
        return jnp.einsum('bhqk,bhkd->bhqd', weight, vf)

    out = lax.map(block, jnp.arange(n_blocks))
    out = jnp.transpose(out, (1, 0, 3, 2, 4)).reshape(B, S, H * dh)
    return out


def compress_blocks(kv, pos, w1, w2):
    B, S, G, dh = kv.shape
    n_cmp = (S - CMP_BLOCK) // CMP_STRIDE + 1
    idx = np.arange(n_cmp)[:, None] * CMP_STRIDE + np.arange(CMP_BLOCK)[None, :]
    blocks = kv[:, idx] + pos[None, None, :, None, :]
    blocks = jnp.transpose(blocks, (0, 1, 3, 2, 4)).reshape(B, n_cmp, G, CMP_BLOCK * dh)
    return jax.nn.gelu(blocks @ w1) @ w2


def nsa_attention(q, k_cmp, v_cmp, k_slc, v_slc, k_win, v_win, gate_logits,
                  cmp_pos_k, cmp_k_w1, cmp_k_w2, cmp_pos_v, cmp_v_w1, cmp_v_w2):
    B, S, G, R, dh = q.shape
    f32 = jnp.float32
    scale = dh ** -0.5
    slopes = alibi_slopes(G * R).reshape(G, R)
    qf = q.astype(f32)
    t = np.arange(S)

    kc = compress_blocks(k_cmp.astype(f32), cmp_pos_k, cmp_k_w1, cmp_k_w2)
    vc = compress_blocks(v_cmp.astype(f32), cmp_pos_v, cmp_v_w1, cmp_v_w2)
    n_cmp = kc.shape[1]
    cmp_end = np.arange(n_cmp) * CMP_STRIDE + CMP_BLOCK - 1
    dist_c_np = t[:, None] - cmp_end[None, :]
    dist_c = jnp.asarray(dist_c_np, f32)
    valid_c = jnp.asarray(dist_c_np >= 0)
    s_c = (jnp.einsum('bqgrd,bkgd->bgrqk', qf, kc) * scale
           - slopes[None, :, :, None, None] * dist_c)
    p_cmp = jax.nn.softmax(jnp.where(valid_c, s_c, NEG_INF), axis=-1) * valid_c
    o_cmp = jnp.einsum('bgrqk,bkgd->bqgrd', p_cmp, vc)

    ratio = SLC_BLOCK // CMP_STRIDE
    span = CMP_BLOCK // CMP_STRIDE
    n_slc = S // SLC_BLOCK
    p_grp = jnp.sum(p_cmp, axis=2)
    need = ratio * (n_slc - 1) + (ratio - 1) + (span - 1) + 1
    p_pad = jnp.pad(p_grp, ((0, 0), (0, 0), (0, 0), (0, max(need - n_cmp, 0))))
    terms = [p_pad[..., m + n: m + n + ratio * (n_slc - 1) + 1: ratio]
             for m in range(ratio) for n in range(span)]
    slc_score = jnp.sum(jnp.stack(terms, axis=0), axis=0)
    blk = np.arange(n_slc)
    cur = t // SLC_BLOCK
    forced = (blk[None, :] == 0) | (blk[None, :] == cur[:, None]) | (blk[None, :] == cur[:, None] - 1)
    valid_blk = blk[None, :] * SLC_BLOCK <= t[:, None]
    slc_score = jnp.where(jnp.asarray(forced), FORCED_SCORE, slc_score)
    slc_score = jnp.where(jnp.asarray(valid_blk), slc_score, NEG_INF)
    n_top = min(N_SELECT, n_slc)
    _, sel_idx = lax.top_k(slc_score, n_top)

    kb = jnp.transpose(k_slc.astype(f32), (0, 2, 1, 3)).reshape(B, G, n_slc, SLC_BLOCK, dh)
    vb = jnp.transpose(v_slc.astype(f32), (0, 2, 1, 3)).reshape(B, G, n_slc, SLC_BLOCK, dh)
    bi = jnp.arange(B)[:, None, None, None]
    gi = jnp.arange(G)[None, :, None, None]
    in_blk = jnp.arange(SLC_BLOCK)

    def slc_block(i):
        start = i * SLC_Q_BLOCK
        qb = lax.dynamic_slice_in_dim(qf, start, SLC_Q_BLOCK, axis=1)
        ib = lax.dynamic_slice_in_dim(sel_idx, start, SLC_Q_BLOCK, axis=2)
        kg = kb[bi, gi, ib].reshape(B, G, SLC_Q_BLOCK, n_top * SLC_BLOCK, dh)
        vg = vb[bi, gi, ib].reshape(B, G, SLC_Q_BLOCK, n_top * SLC_BLOCK, dh)
        pos = (ib[..., None] * SLC_BLOCK + in_blk).reshape(B, G, SLC_Q_BLOCK, n_top * SLC_BLOCK)
        tq = start + jnp.arange(SLC_Q_BLOCK)
        dist = (tq[None, None, :, None] - pos)[:, :, None]
        s = (jnp.einsum('bqgrd,bgqkd->bgrqk', qb, kg) * scale
             - slopes[None, :, :, None, None] * dist.astype(f32))
        p = jax.nn.softmax(jnp.where(dist >= 0, s, NEG_INF), axis=-1)
        return jnp.einsum('bgrqk,bgqkd->bqgrd', p, vg)

    o_slc = lax.map(slc_block, jnp.arange(S // SLC_Q_BLOCK))
    o_slc = jnp.moveaxis(o_slc, 0, 1).reshape(B, S, G, R, dh)

    n_wb = S // Q_BLOCK
    kw_pad = jnp.pad(k_win.astype(f32), ((0, 0), (WINDOW, 0), (0, 0), (0, 0)))
    vw_pad = jnp.pad(v_win.astype(f32), ((0, 0), (WINDOW, 0), (0, 0), (0, 0)))
    widx = np.arange(n_wb)[:, None] * Q_BLOCK + np.arange(WINDOW + Q_BLOCK)[None, :]
    kwb = kw_pad[:, widx]
    vwb = vw_pad[:, widx]
    q_pos = np.arange(n_wb)[:, None] * Q_BLOCK + np.arange(Q_BLOCK)[None, :]
    k_pos = widx - WINDOW
    dist_w_np = q_pos[:, :, None] - k_pos[:, None, :]
    valid_w = jnp.asarray((dist_w_np >= 0) & (dist_w_np < WINDOW) & (k_pos[:, None, :] >= 0))
    dist_w = jnp.asarray(dist_w_np, f32)
    qwb = qf.reshape(B, n_wb, Q_BLOCK, G, R, dh)
    s_w = (jnp.einsum('bnqgrd,bnkgd->bgrnqk', qwb, kwb) * scale
           - slopes[None, :, :, None, None, None] * dist_w)
    p_w = jax.nn.softmax(jnp.where(valid_w, s_w, NEG_INF), axis=-1)
    o_win = jnp.einsum('bgrnqk,bnkgd->bnqgrd', p_w, vwb).reshape(B, S, G, R, dh)

    g = jax.nn.sigmoid(gate_logits.astype(f32)).reshape(B, S, 3, G, R, 1)
    out = g[:, :, 0] * o_cmp + g[:, :, 1] * o_slc + g[:, :, 2] * o_win
    return out.reshape(B, S, G * R * dh)


def hybrid_mixer(h, w_in, cmp_pos_k, cmp_k_w1, cmp_k_w2, cmp_pos_v, cmp_v_w1, cmp_v_w2,
                 w_branch_sb, w_branch_nsa, w_out):
    B, S, _ = h.shape
    proj = h @ w_in
    offsets = np.cumsum(IN_SIZES)[:-1].tolist()
    (q_sb, k_sb, v_sb, q_nsa, k_cmp, v_cmp, k_slc, v_slc, k_win, v_win,
     nsa_gates, merge_logits) = jnp.split(proj, offsets, axis=-1)
    heads = lambda a, n: a.reshape(B, S, n, HEAD_DIM)
    sb_out = stick_breaking_attention(heads(q_sb, SB_HEADS), heads(k_sb, SB_HEADS),
                                      heads(v_sb, SB_HEADS)).astype(h.dtype)
    G = NSA_KV_GROUPS
    nsa_out = nsa_attention(q_nsa.reshape(B, S, G, NSA_HPG, HEAD_DIM),
                            heads(k_cmp, G), heads(v_cmp, G), heads(k_slc, G), heads(v_slc, G),
                            heads(k_win, G), heads(v_win, G), nsa_gates,
                            cmp_pos_k, cmp_k_w1, cmp_k_w2, cmp_pos_v, cmp_v_w1, cmp_v_w2).astype(h.dtype)
    y_sb = sb_out @ w_branch_sb
    y_nsa = nsa_out @ w_branch_nsa
    gates = jax.nn.sigmoid(merge_logits.astype(jnp.float32)).reshape(B, S, 2, D_MODEL).astype(h.dtype)
    merged = gates[:, :, 0] * y_sb + gates[:, :, 1] * y_nsa
    return merged @ w_out


def setup_inputs(seed: int = 0) -> dict:
    key = jax.random.key(seed)
    ks = jax.random.split(key, 22)
    f32 = jnp.float32
    L = DEPTH
    dense = lambda k, shape, fan_in: jax.random.normal(k, shape, f32) * fan_in ** -0.5
    gain = lambda k: 1.0 + 0.02 * jax.random.normal(k, (L, D_MODEL), f32)
    return {
        "x": jax.random.normal(ks[0], (BATCH, SEQ, D_MODEL), f32),
        "ffn1_pre_g": gain(ks[1]),
        "ffn1_w_in": dense(ks[2], (L, D_MODEL, 2 * D_FF), D_MODEL),
        "ffn1_w_out": dense(ks[3], (L, D_FF, D_MODEL), D_FF),
        "ffn1_post_g": gain(ks[4]),
        "mix_pre_g": gain(ks[5]),
        "w_in": dense(ks[6], (L, D_MODEL, IN_COLS), D_MODEL),
        "cmp_pos_k": 0.02 * jax.random.normal(ks[7], (L, CMP_BLOCK, HEAD_DIM), f32),
        "cmp_k_w1": dense(ks[8], (L, CMP_BLOCK * HEAD_DIM, HEAD_DIM), CMP_BLOCK * HEAD_DIM),
        "cmp_k_w2": dense(ks[9], (L, HEAD_DIM, HEAD_DIM), HEAD_DIM),
        "cmp_pos_v": 0.02 * jax.random.normal(ks[10], (L, CMP_BLOCK, HEAD_DIM), f32),
        "cmp_v_w1": dense(ks[11], (L, CMP_BLOCK * HEAD_DIM, HEAD_DIM), CMP_BLOCK * HEAD_DIM),
        "cmp_v_w2": dense(ks[12], (L, HEAD_DIM, HEAD_DIM), HEAD_DIM),
        "w_branch_sb": dense(ks[13], (L, SB_HEADS * HEAD_DIM, D_MODEL), SB_HEADS * HEAD_DIM),
        "w_branch_nsa": dense(ks[14], (L, NSA_HEADS * HEAD_DIM, D_MODEL), NSA_HEADS * HEAD_DIM),
        "w_out": dense(ks[15], (L, D_MODEL, D_MODEL), D_MODEL),
        "mix_post_g": gain(ks[16]),
        "ffn2_pre_g": gain(ks[17]),
        "ffn2_w_in": dense(ks[18], (L, D_MODEL, 2 * D_FF), D_MODEL),
        "ffn2_w_out": dense(ks[19], (L, D_FF, D_MODEL), D_FF),
        "ffn2_post_g": gain(ks[20]),
    }


def reference(x, ffn1_pre_g, ffn1_w_in, ffn1_w_out, ffn1_post_g, mix_pre_g, w_in,
              cmp_pos_k, cmp_k_w1, cmp_k_w2, cmp_pos_v, cmp_v_w1, cmp_v_w2,
              w_branch_sb, w_branch_nsa, w_out, mix_post_g,
              ffn2_pre_g, ffn2_w_in, ffn2_w_out, ffn2_post_g):
    for l in range(DEPTH):
        h = rmsnorm(x, ffn1_pre_g[l])
        x = x + 0.5 * rmsnorm(swiglu_ffn(h, ffn1_w_in[l], ffn1_w_out[l]), ffn1_post_g[l])
        h = rmsnorm(x, mix_pre_g[l])
        y = hybrid_mixer(h, w_in[l], cmp_pos_k[l], cmp_k_w1[l], cmp_k_w2[l],
                         cmp_pos_v[l], cmp_v_w1[l], cmp_v_w2[l],
                         w_branch_sb[l], w_branch_nsa[l], w_out[l])
        x = x + rmsnorm(y, mix_post_g[l])
        h = rmsnorm(x, ffn2_pre_g[l])
        x = x + 0.5 * rmsnorm(swiglu_ffn(h, ffn2_w_in[l], ffn2_w_out[l]), ffn2_post_g[l])
    return x
```

```python
import functools

import numpy as np
import jax
import jax.numpy as jnp
from jax import lax
from jax.experimental import pallas as pl
from jax.experimental.pallas import tpu as pltpu

HEAD_DIM = 128
SB_HEADS = 8
NSA_HEADS = 8
NSA_GROUPS = 2
NSA_HPG = NSA_HEADS // NSA_GROUPS
CMP_BLOCK = 32
CMP_STRIDE = 16
SLC_BLOCK = 64
N_SELECT = 16
WINDOW = 512
NORM_EPS = 1e-6
NEG_INF = -1e30
FORCED_SCORE = 1e9

QKV_COLS = (3 * SB_HEADS + NSA_HEADS + 6 * NSA_GROUPS) * HEAD_DIM
COL_Q_SB = 0
COL_K_SB = SB_HEADS
COL_V_SB = 2 * SB_HEADS
COL_Q_NSA = 3 * SB_HEADS
COL_K_CMP = COL_Q_NSA + NSA_HEADS
COL_K_SLC = COL_K_CMP + 2 * NSA_GROUPS
COL_V_SLC = COL_K_SLC + NSA_GROUPS
COL_K_WIN = COL_V_SLC + NSA_GROUPS
COL_V_WIN = COL_K_WIN + NSA_GROUPS
N_GATE_LOGITS = 3 * NSA_HEADS
GATE_PAD = 128

LANES = 128
V7X_VMEM_LIMIT = 56 * 1024 * 1024

F32 = jnp.float32
BF16 = jnp.bfloat16
NT_DIMS = (((1,), (1,)), ((), ()))


def _params(semantics):
    return pltpu.CompilerParams(dimension_semantics=semantics,
                                vmem_limit_bytes=V7X_VMEM_LIMIT)


def _rms(x, g):
    ms = jnp.mean(x * x, axis=-1, keepdims=True)
    return x * lax.rsqrt(ms + NORM_EPS) * g


def _ffn_body(x_ref, gpre_ref, wg_ref, wu_ref, wo_ref, gpost_ref, o_ref, h_ref, acc_ref):
    j = pl.program_id(1)

    @pl.when(j == 0)
    def _():
        h_ref[...] = _rms(x_ref[...], gpre_ref[...]).astype(BF16)
        acc_ref[...] = jnp.zeros_like(acc_ref)

    h = h_ref[...]
    gate = jnp.dot(h, wg_ref[...], preferred_element_type=F32)
    up = jnp.dot(h, wu_ref[...], preferred_element_type=F32)
    act = (gate * jax.nn.sigmoid(gate)) * up
    acc_ref[...] += jnp.dot(act.astype(BF16), wo_ref[...], preferred_element_type=F32)

    @pl.when(j == pl.num_programs(1) - 1)
    def _():
        o_ref[...] = x_ref[...] + 0.5 * _rms(acc_ref[...], gpost_ref[...])


def _ffn(x, g_pre, w_in, w_out, g_post, *, tm, tf):
    m, d = x.shape
    f = w_out.shape[0]
    nf = f // tf
    assert m % tm == 0 and f % tf == 0 and w_in.shape == (d, 2 * f)
    return pl.pallas_call(
        _ffn_body,
        name="ffn",
        grid=(m // tm, nf),
        in_specs=[
            pl.BlockSpec((tm, d), lambda i, j: (i, 0)),
            pl.BlockSpec((1, d), lambda i, j: (0, 0)),
            pl.BlockSpec((d, tf), lambda i, j: (0, j)),
            pl.BlockSpec((d, tf), lambda i, j: (0, j + nf)),
            pl.BlockSpec((tf, d), lambda i, j: (j, 0)),
            pl.BlockSpec((1, d), lambda i, j: (0, 0)),
        ],
        out_specs=pl.BlockSpec((tm, d), lambda i, j: (i, 0)),
        out_shape=jax.ShapeDtypeStruct((m, d), F32),
        scratch_shapes=[pltpu.VMEM((tm, d), BF16), pltpu.VMEM((tm, d), F32)],
        compiler_params=_params(("parallel", "arbitrary")),
    )(x, g_pre, w_in, w_in, w_out, g_post)


def _norm_matmul_body(x_ref, g_ref, w_ref, o_ref, h_ref):
    @pl.when(pl.program_id(1) == 0)
    def _():
        h_ref[...] = _rms(x_ref[...], g_ref[...]).astype(BF16)

    o_ref[...] = jnp.dot(h_ref[...], w_ref[...], preferred_element_type=F32).astype(o_ref.dtype)


def _norm_matmul(x, g, w, out_dtype, *, tm, tn, name):
    m, d = x.shape
    n = w.shape[1]
    assert m % tm == 0 and n % tn == 0
    return pl.pallas_call(
        _norm_matmul_body,
        name=name,
        grid=(m // tm, n // tn),
        in_specs=[
            pl.BlockSpec((tm, d), lambda i, j: (i, 0)),
            pl.BlockSpec((1, d), lambda i, j: (0, 0)),
            pl.BlockSpec((d, tn), lambda i, j: (0, j)),
        ],
        out_specs=pl.BlockSpec((tm, tn), lambda i, j: (i, j)),
        out_shape=jax.ShapeDtypeStruct((m, n), out_dtype),
        scratch_shapes=[pltpu.VMEM((tm, d), BF16)],
        compiler_params=_params(("parallel", "arbitrary")),
    )(x, g, w)


def _sb_body(q_ref, k_ref, v_ref, tri_ref, o_ref, *, tq):
    i = pl.program_id(2)
    q = q_ref[0]
    tri = tri_ref[...]
    scale = HEAD_DIM ** -0.5
    row = lax.broadcasted_iota(jnp.int32, (tq, tq), 0)
    col = lax.broadcasted_iota(jnp.int32, (tq, tq), 1)
    past = col < row

    def tile(j, c, acc, diag):
        k0 = pl.multiple_of(j * tq, tq)
        k = k_ref[0, pl.ds(k0, tq), :]
        v = v_ref[0, pl.ds(k0, tq), :]
        z = lax.dot_general(q, k, NT_DIMS, preferred_element_type=F32) * scale
        softplus = jnp.maximum(z, 0.0) + jnp.log1p(jnp.exp(-jnp.abs(z)))
        log_beta = z - softplus
        log_1m = -softplus
        if diag:
            log_1m = jnp.where(past, log_1m, 0.0)
        hi = log_1m.astype(BF16)
        lo = (log_1m - hi.astype(F32)).astype(BF16)
        suffix = (jnp.dot(hi, tri, preferred_element_type=F32)
                  + jnp.dot(lo, tri, preferred_element_type=F32))
        w = jnp.exp(log_beta + c + suffix)
        if diag:
            w = jnp.where(past, w, 0.0)
        acc = acc + jnp.dot(w.astype(BF16), v, preferred_element_type=F32)
        c = c + jnp.sum(log_1m, axis=-1, keepdims=True)
        return c, acc

    c, acc = tile(i, jnp.zeros((tq, 1), F32), jnp.zeros((tq, HEAD_DIM), F32), True)
    c, acc = lax.fori_loop(0, i, lambda t, s: tile(i - 1 - t, s[0], s[1], False), (c, acc))
    o_ref[0] = acc.astype(o_ref.dtype)


def _sb_attention(proj, *, tq):
    b, s, _ = proj.shape
    assert s % tq == 0
    tri = jnp.asarray(np.tril(np.ones((tq, tq), np.float32), -1), BF16)
    return pl.pallas_call(
        functools.partial(_sb_body, tq=tq),
        name="sb_attn",
        grid=(b, SB_HEADS, s // tq),
        in_specs=[
            pl.BlockSpec((1, tq, HEAD_DIM), lambda bb, h, i: (bb, i, COL_Q_SB + h)),
            pl.BlockSpec((1, s, HEAD_DIM), lambda bb, h, i: (bb, 0, COL_K_SB + h)),
            pl.BlockSpec((1, s, HEAD_DIM), lambda bb, h, i: (bb, 0, COL_V_SB + h)),
            pl.BlockSpec((tq, tq), lambda bb, h, i: (0, 0)),
        ],
        out_specs=pl.BlockSpec((1, tq, HEAD_DIM), lambda bb, h, i: (bb, i, h)),
        out_shape=jax.ShapeDtypeStruct((b, s, SB_HEADS * HEAD_DIM), BF16),
        compiler_params=_params(("parallel", "parallel", "arbitrary")),
    )(proj, proj, proj, tri)


def _compress_body(x_ref, pos_ref, w1_ref, w2_ref, o_ref):
    x = x_ref[0, 0].astype(F32)
    pos = pos_ref[0]
    xa = (x + pos[0:1]).astype(BF16)
    xb = (x + pos[1:2]).astype(BF16)
    first = jnp.dot(xa, w1_ref[0, 0], preferred_element_type=F32)
    second = jnp.dot(xb, w1_ref[0, 1], preferred_element_type=F32)
    n_chunk = x.shape[0]
    pre = first + pltpu.roll(second, n_chunk - 1, 0)
    y = jax.nn.gelu(pre).astype(BF16)
    o_ref[0, 0] = jnp.dot(y, w2_ref[0], preferred_element_type=F32).astype(o_ref.dtype)


def _compress(x, pos, w1, w2):
    _, bg, n_chunk, width = x.shape
    return pl.pallas_call(
        _compress_body,
        name="compress",
        grid=(2, bg),
        in_specs=[
            pl.BlockSpec((1, 1, n_chunk, width), lambda a, n: (a, n, 0, 0)),
            pl.BlockSpec((1, 2, width), lambda a, n: (a, 0, 0)),
            pl.BlockSpec((1, 2, width, HEAD_DIM), lambda a, n: (a, 0, 0, 0)),
            pl.BlockSpec((1, HEAD_DIM, HEAD_DIM), lambda a, n: (a, 0, 0)),
        ],
        out_specs=pl.BlockSpec((1, 1, n_chunk, HEAD_DIM), lambda a, n: (a, n, 0, 0)),
        out_shape=jax.ShapeDtypeStruct((2, bg, n_chunk, HEAD_DIM), BF16),
        compiler_params=_params(("parallel", "parallel")),
    )(x, pos, w1, w2)


def _nsa_body(q_ref, kc_ref, vc_ref, ks_ref, vs_ref, kw_ref, vw_ref, gate_ref,
              pool_ref, expand_ref, o_ref, bias_ref, *, tq, n_slc):
    g = pl.program_id(1)
    i = pl.program_id(2)
    q0 = i * tq
    scale = HEAD_DIM ** -0.5
    n_cmp_pad = kc_ref.shape[2]
    slopes = [jnp.where(g == 0, 2.0 ** -(r + 1), 2.0 ** -(r + 1 + NSA_HPG)) for r in range(NSA_HPG)]
    q_heads = [q_ref[0, :, r * HEAD_DIM:(r + 1) * HEAD_DIM] for r in range(NSA_HPG)]
    t_col = q0 + lax.broadcasted_iota(jnp.int32, (tq, 1), 0)

    kc = kc_ref[0, 0]
    vc = vc_ref[0, 0]
    c_row = lax.broadcasted_iota(jnp.int32, (1, n_cmp_pad), 1)
    cmp_end = c_row * CMP_STRIDE + (CMP_BLOCK - 1)
    valid_c = jnp.logical_and(t_col >= cmp_end, c_row < n_cmp_pad - 1)
    col_bias_c = (cmp_end - q0).astype(F32)
    p_grp = jnp.zeros((tq, n_cmp_pad), F32)
    o_cmp = []
    for r in range(NSA_HPG):
        s = lax.dot_general(q_heads[r], kc, NT_DIMS, preferred_element_type=F32) * scale
        s = jnp.where(valid_c, s + slopes[r] * col_bias_c, NEG_INF)
        m = jnp.max(s, axis=-1, keepdims=True)
        p = jnp.where(valid_c, jnp.exp(s - m), 0.0)
        l = jnp.sum(p, axis=-1, keepdims=True)
        p = p / jnp.where(l > 0.0, l, 1.0)
        o_cmp.append(jnp.dot(p.astype(BF16), vc, preferred_element_type=F32))
        p_grp = p_grp + p

    pool = pool_ref[...]
    p1 = p_grp.astype(BF16)
    r1 = p_grp - p1.astype(F32)
    p2 = r1.astype(BF16)
    p3 = (r1 - p2.astype(F32)).astype(BF16)
    slc = (jnp.dot(p1, pool, preferred_element_type=F32)
           + jnp.dot(p2, pool, preferred_element_type=F32)
           + jnp.dot(p3, pool, preferred_element_type=F32))
    blk = lax.broadcasted_iota(jnp.int32, (1, LANES), 1)
    cur = t_col // SLC_BLOCK
    forced = jnp.logical_or(blk == 0, jnp.logical_or(blk == cur, blk == cur - 1))
    score = jnp.where(forced, FORCED_SCORE, slc)
    score = jnp.where(blk <= cur, score, NEG_INF)
    score_t = score.T[:n_slc]

    idx = lax.broadcasted_iota(jnp.int32, (n_slc, tq), 0)
    rank = jnp.zeros((n_slc, tq), F32)
    for ii in range(n_slc):
        row = score_t[ii:ii + 1, :]
        ge = jnp.where(row >= score_t, 1.0, 0.0)
        gt = jnp.where(row > score_t, 1.0, 0.0)
        rank = rank + jnp.where(idx > ii, ge, gt)
    sel_t = jnp.where(rank < float(N_SELECT), 1.0, 0.0)
    sel_t = jnp.concatenate([sel_t, jnp.zeros((LANES - n_slc, tq), F32)], axis=0)
    sel = sel_t.T.astype(BF16)
    key_sel = jnp.dot(sel, expand_ref[...], preferred_element_type=F32)
    bias_ref[...] = (key_sel - 1.0) * (-NEG_INF)

    row_i = lax.broadcasted_iota(jnp.int32, (tq, tq), 0)
    col_i = lax.broadcasted_iota(jnp.int32, (tq, tq), 1)
    col_f = lax.broadcasted_iota(jnp.int32, (1, tq), 1)

    def attend(k_ref, v_ref, j, mask_bias, state):
        k0 = pl.multiple_of(j * tq, tq)
        k = k_ref[0, pl.ds(k0, tq), :]
        v = v_ref[0, pl.ds(k0, tq), :]
        col_bias = (k0 - q0 + col_f).astype(F32)
        new_state = []
        for r in range(NSA_HPG):
            m, l, acc = state[r]
            s = lax.dot_general(q_heads[r], k, NT_DIMS, preferred_element_type=F32) * scale
            s = s + slopes[r] * col_bias + mask_bias
            m_new = jnp.maximum(m, jnp.max(s, axis=-1, keepdims=True))
            alpha = jnp.exp(m - m_new)
            p = jnp.exp(s - m_new)
            l = alpha * l + jnp.sum(p, axis=-1, keepdims=True)
            acc = alpha * acc + jnp.dot(p.astype(BF16), v, preferred_element_type=F32)
            new_state.append((m_new, l, acc))
        return tuple(new_state)

    def init_state():
        return tuple((jnp.full((tq, 1), NEG_INF, F32), jnp.zeros((tq, 1), F32),
                      jnp.zeros((tq, HEAD_DIM), F32)) for _ in range(NSA_HPG))

    causal_bias = jnp.where(col_i <= row_i, 0.0, NEG_INF)

    diag0 = pl.multiple_of(q0, tq)
    st = attend(ks_ref, vs_ref, i, bias_ref[:, pl.ds(diag0, tq)] + causal_bias, init_state())

    def slc_step(j, state):
        return attend(ks_ref, vs_ref, j, bias_ref[:, pl.ds(pl.multiple_of(j * tq, tq), tq)], state)

    st = lax.fori_loop(0, i, slc_step, st)
    o_slc = [acc / l for (_, l, acc) in st]

    st = attend(kw_ref, vw_ref, i, causal_bias, init_state())
    for d in range(1, (WINDOW - 1) // tq + 2):
        dist = d * tq + row_i - col_i
        in_window = jnp.logical_and(dist < WINDOW, i - d >= 0)
        st = attend(kw_ref, vw_ref, jnp.maximum(i - d, 0), jnp.where(in_window, 0.0, NEG_INF), st)
    o_win = [acc / l for (_, l, acc) in st]

    sg = jax.nn.sigmoid(gate_ref[0])

    def gate(branch, r):
        c0 = branch * NSA_HEADS + r
        c1 = c0 + NSA_HPG
        return jnp.where(g == 0, sg[:, c0:c0 + 1], sg[:, c1:c1 + 1])

    for r in range(NSA_HPG):
        out = gate(0, r) * o_cmp[r] + gate(1, r) * o_slc[r] + gate(2, r) * o_win[r]
        o_ref[0, :, r * HEAD_DIM:(r + 1) * HEAD_DIM] = out.astype(o_ref.dtype)


def _nsa_attention(proj, kvc, gm, *, tq):
    b, s, _ = proj.shape
    n_cmp_pad = s // CMP_STRIDE
    n_slc = s // SLC_BLOCK
    assert s % tq == 0 and tq % SLC_BLOCK == 0 and n_slc <= LANES and n_cmp_pad % LANES == 0
    ratio = SLC_BLOCK // CMP_STRIDE
    span = CMP_BLOCK // CMP_STRIDE
    pool = np.zeros((n_cmp_pad, LANES), np.float32)
    for jj in range(n_slc):
        for mm in range(ratio):
            for nn in range(span):
                c = ratio * jj + mm + nn
                if c < n_cmp_pad:
                    pool[c, jj] += 1.0
    expand = np.zeros((LANES, s), np.float32)
    expand[np.arange(s) // SLC_BLOCK, np.arange(s)] = 1.0
    grp_w = NSA_HPG * HEAD_DIM
    kv_spec = lambda col: pl.BlockSpec((1, s, HEAD_DIM), lambda bb, g, i: (bb, 0, col + g))
    return pl.pallas_call(
        functools.partial(_nsa_body, tq=tq, n_slc=n_slc),
        name="nsa_attn",
        grid=(b, NSA_GROUPS, s // tq),
        in_specs=[
            pl.BlockSpec((1, tq, grp_w), lambda bb, g, i: (bb, i, COL_Q_NSA // NSA_HPG + g)),
            pl.BlockSpec((1, 1, n_cmp_pad, HEAD_DIM), lambda bb, g, i: (0, bb * NSA_GROUPS + g, 0, 0)),
            pl.BlockSpec((1, 1, n_cmp_pad, HEAD_DIM), lambda bb, g, i: (1, bb * NSA_GROUPS + g, 0, 0)),
            kv_spec(COL_K_SLC), kv_spec(COL_V_SLC), kv_spec(COL_K_WIN), kv_spec(COL_V_WIN),
            pl.BlockSpec((1, tq, GATE_PAD), lambda bb, g, i: (bb, i, 0)),
            pl.BlockSpec((n_cmp_pad, LANES), lambda bb, g, i: (0, 0)),
            pl.BlockSpec((LANES, s), lambda bb, g, i: (0, 0)),
        ],
        out_specs=pl.BlockSpec((1, tq, grp_w), lambda bb, g, i: (bb, i, g)),
        out_shape=jax.ShapeDtypeStruct((b, s, NSA_HEADS * HEAD_DIM), BF16),
        scratch_shapes=[pltpu.VMEM((tq, s), F32)],
        compiler_params=_params(("parallel", "parallel", "arbitrary")),
    )(proj, kvc, kvc, proj, proj, proj, proj, gm, jnp.asarray(pool, BF16), jnp.asarray(expand, BF16))


def _merge_body(sb_ref, nsa_ref, m0_ref, m1_ref, x_ref, wsb_ref, wnsa_ref, wout_ref, g_ref,
                o_ref, acc_ref):
    j = pl.program_id(1)

    @pl.when(j == 0)
    def _():
        acc_ref[...] = jnp.zeros_like(acc_ref)

    y_sb = jnp.dot(sb_ref[...], wsb_ref[...], preferred_element_type=F32)
    y_nsa = jnp.dot(nsa_ref[...], wnsa_ref[...], preferred_element_type=F32)
    merged = jax.nn.sigmoid(m0_ref[...]) * y_sb + jax.nn.sigmoid(m1_ref[...]) * y_nsa
    acc_ref[...] += jnp.dot(merged.astype(BF16), wout_ref[...], preferred_element_type=F32)

    @pl.when(j == pl.num_programs(1) - 1)
    def _():
        o_ref[...] = x_ref[...] + _rms(acc_ref[...], g_ref[...])


def _merge_out(sb, nsa, gm, x, w_sb, w_nsa, w_out, g_post, *, tm, tn):
    m, d = x.shape
    assert m % tm == 0 and d % tn == 0 and gm.shape == (m, 2 * d)
    nj = d // tn
    return pl.pallas_call(
        _merge_body,
        name="merge_out",
        grid=(m // tm, nj),
        in_specs=[
            pl.BlockSpec((tm, sb.shape[1]), lambda i, j: (i, 0)),
            pl.BlockSpec((tm, nsa.shape[1]), lambda i, j: (i, 0)),
            pl.BlockSpec((tm, tn), lambda i, j: (i, j)),
            pl.BlockSpec((tm, tn), lambda i, j: (i, j + nj)),
            pl.BlockSpec((tm, d), lambda i, j: (i, 0)),
            pl.BlockSpec((sb.shape[1], tn), lambda i, j: (0, j)),
            pl.BlockSpec((nsa.shape[1], tn), lambda i, j: (0, j)),
            pl.BlockSpec((tn, d), lambda i, j: (j, 0)),
            pl.BlockSpec((1, d), lambda i, j: (0, 0)),
        ],
        out_specs=pl.BlockSpec((tm, d), lambda i, j: (i, 0)),
        out_shape=jax.ShapeDtypeStruct((m, d), F32),
        scratch_shapes=[pltpu.VMEM((tm, d), F32)],
        compiler_params=_params(("parallel", "arbitrary")),
    )(sb, nsa, gm, gm, x, w_sb, w_nsa, w_out, g_post)


def kernel(x, ffn1_pre_g, ffn1_w_in, ffn1_w_out, ffn1_post_g, mix_pre_g, w_in, cmp_pos_k, cmp_k_w1, cmp_k_w2, cmp_pos_v, cmp_v_w1, cmp_v_w2, w_branch_sb, w_branch_nsa, w_out, mix_post_g, ffn2_pre_g, ffn2_w_in, ffn2_w_out, ffn2_post_g):
    b, s, d = x.shape
    m = b * s
    depth = ffn1_pre_g.shape[0]
    h = x.reshape(m, d)
    for l in range(depth):
        h = _ffn(h, ffn1_pre_g[l][None], ffn1_w_in[l].astype(BF16), ffn1_w_out[l].astype(BF16),
                 ffn1_post_g[l][None], tm=512, tf=512)

        w = w_in[l]
        merge_logit_w = w[:, QKV_COLS + N_GATE_LOGITS:]
        w_gates = jnp.pad(w[:, QKV_COLS:QKV_COLS + N_GATE_LOGITS], ((0, 0), (0, GATE_PAD - N_GATE_LOGITS)))
        g_mix = mix_pre_g[l][None]
        proj = _norm_matmul(h, g_mix, w[:, :QKV_COLS].astype(BF16), BF16, tm=512, tn=512, name="in_proj_qkv")
        gates = _norm_matmul(h, g_mix, w_gates.astype(BF16), F32, tm=512, tn=GATE_PAD, name="in_proj_gates")
        merge = _norm_matmul(h, g_mix, merge_logit_w.astype(BF16), F32, tm=512, tn=512, name="in_proj_merge")
        proj = proj.reshape(b, s, QKV_COLS)

        sb = _sb_attention(proj, tq=256)

        n_chunk = s // CMP_STRIDE
        kv = proj[:, :, COL_K_CMP * HEAD_DIM:COL_K_SLC * HEAD_DIM]
        kv = kv.reshape(b, n_chunk, CMP_STRIDE, 2, NSA_GROUPS, HEAD_DIM)
        kv = jnp.transpose(kv, (3, 0, 4, 1, 2, 5)).reshape(2, b * NSA_GROUPS, n_chunk, CMP_STRIDE * HEAD_DIM)
        half = CMP_STRIDE * HEAD_DIM
        pos = jnp.stack([cmp_pos_k[l], cmp_pos_v[l]]).reshape(2, 2, half)
        w1 = jnp.stack([cmp_k_w1[l], cmp_v_w1[l]]).astype(BF16).reshape(2, 2, half, HEAD_DIM)
        w2 = jnp.stack([cmp_k_w2[l], cmp_v_w2[l]]).astype(BF16)
        kvc = _compress(kv, pos, w1, w2)

        nsa = _nsa_attention(proj, kvc, gates.reshape(b, s, GATE_PAD), tq=256)

        h = _merge_out(sb.reshape(m, -1), nsa.reshape(m, -1), merge, h,
                       w_branch_sb[l].astype(BF16), w_branch_nsa[l].astype(BF16), w_out[l].astype(BF16),
                       mix_post_g[l][None], tm=512, tn=512)

        h = _ffn(h, ffn2_pre_g[l][None], ffn2_w_in[l].astype(BF16), ffn2_w_out[l].astype(BF16),
                 ffn2_post_g[l][None], tm=512, tf=512)
    return h.reshape(b, s, d)
```

```python
import functools

import numpy as np
import jax
import jax.numpy as jnp
from jax import lax
from jax.experimental import pallas as pl
from jax.experimental.pallas import tpu as pltpu

HEAD_DIM = 128
SB_HEADS = 8
NSA_HEADS = 8
NSA_GROUPS = 2
NSA_HPG = NSA_HEADS // NSA_GROUPS
CMP_BLOCK = 32
CMP_STRIDE = 16
SLC_BLOCK = 64
N_SELECT = 16
WINDOW = 512
NORM_EPS = 1e-6
NEG_INF = -1e30
SCORE_SCALE = 2.0 ** 64
FORCED_SCORE = 1e30

QKV_COLS = (3 * SB_HEADS + NSA_HEADS + 6 * NSA_GROUPS) * HEAD_DIM
COL_Q_SB = 0
COL_K_SB = SB_HEADS
COL_V_SB = 2 * SB_HEADS
COL_Q_NSA = 3 * SB_HEADS
COL_K_CMP = COL_Q_NSA + NSA_HEADS
COL_K_SLC = COL_K_CMP + 2 * NSA_GROUPS
COL_V_SLC = COL_K_SLC + NSA_GROUPS
COL_K_WIN = COL_V_SLC + NSA_GROUPS
COL_V_WIN = COL_K_WIN + NSA_GROUPS
N_GATE_LOGITS = 3 * NSA_HEADS
GATE_PAD = 128

LANES = 128
V7X_VMEM_LIMIT = 56 * 1024 * 1024

F32 = jnp.float32
BF16 = jnp.bfloat16
NT_DIMS = (((1,), (1,)), ((), ()))


def _params(semantics):
    return pltpu.CompilerParams(dimension_semantics=semantics,
                                vmem_limit_bytes=V7X_VMEM_LIMIT)


def _rms(x, g):
    ms = jnp.mean(x * x, axis=-1, keepdims=True)
    return x * lax.rsqrt(ms + NORM_EPS) * g


def _ffn_body(x_ref, gpre_ref, wg_ref, wu_ref, wo_ref, gpost_ref, o_ref, h_ref, acc_ref):
    j = pl.program_id(1)

    @pl.when(j == 0)
    def _():
        h_ref[...] = _rms(x_ref[...], gpre_ref[...]).astype(BF16)
        acc_ref[...] = jnp.zeros_like(acc_ref)

    h = h_ref[...]
    gate = jnp.dot(h, wg_ref[...], preferred_element_type=F32)
    up = jnp.dot(h, wu_ref[...], preferred_element_type=F32)
    act = (gate * jax.nn.sigmoid(gate)) * up
    acc_ref[...] += jnp.dot(act.astype(BF16), wo_ref[...], preferred_element_type=F32)

    @pl.when(j == pl.num_programs(1) - 1)
    def _():
        o_ref[...] = x_ref[...] + 0.5 * _rms(acc_ref[...], gpost_ref[...])


def _ffn(x, g_pre, w_in, w_out, g_post, *, tm, tf):
    m, d = x.shape
    f = w_out.shape[0]
    nf = f // tf
    assert m % tm == 0 and f % tf == 0 and w_in.shape == (d, 2 * f)
    return pl.pallas_call(
        _ffn_body,
        name="ffn",
        grid=(m // tm, nf),
        in_specs=[
            pl.BlockSpec((tm, d), lambda i, j: (i, 0)),
            pl.BlockSpec((1, d), lambda i, j: (0, 0)),
            pl.BlockSpec((d, tf), lambda i, j: (0, j)),
            pl.BlockSpec((d, tf), lambda i, j: (0, j + nf)),
            pl.BlockSpec((tf, d), lambda i, j: (j, 0)),
            pl.BlockSpec((1, d), lambda i, j: (0, 0)),
        ],
        out_specs=pl.BlockSpec((tm, d), lambda i, j: (i, 0)),
        out_shape=jax.ShapeDtypeStruct((m, d), F32),
        scratch_shapes=[pltpu.VMEM((tm, d), BF16), pltpu.VMEM((tm, d), F32)],
        compiler_params=_params(("parallel", "arbitrary")),
    )(x, g_pre, w_in, w_in, w_out, g_post)


def _norm_matmul_body(x_ref, g_ref, w_ref, o_ref, h_ref):
    @pl.when(pl.program_id(1) == 0)
    def _():
        h_ref[...] = _rms(x_ref[...], g_ref[...]).astype(BF16)

    o_ref[...] = jnp.dot(h_ref[...], w_ref[...], preferred_element_type=F32).astype(o_ref.dtype)


def _norm_matmul(x, g, w, out_dtype, *, tm, tn, name):
    m, d = x.shape
    n = w.shape[1]
    assert m % tm == 0 and n % tn == 0
    return pl.pallas_call(
        _norm_matmul_body,
        name=name,
        grid=(m // tm, n // tn),
        in_specs=[
            pl.BlockSpec((tm, d), lambda i, j: (i, 0)),
            pl.BlockSpec((1, d), lambda i, j: (0, 0)),
            pl.BlockSpec((d, tn), lambda i, j: (0, j)),
        ],
        out_specs=pl.BlockSpec((tm, tn), lambda i, j: (i, j)),
        out_shape=jax.ShapeDtypeStruct((m, n), out_dtype),
        scratch_shapes=[pltpu.VMEM((tm, d), BF16)],
        compiler_params=_params(("parallel", "arbitrary")),
    )(x, g, w)


EXP_UNDERFLOW = 104.0


def _sb_body(q_ref, k_ref, v_ref, tri_ref, o_ref, *, tq, heads):
    i = pl.program_id(2)
    tri = tri_ref[...]
    row = lax.broadcasted_iota(jnp.int32, (tq, tq), 0)
    col = lax.broadcasted_iota(jnp.int32, (tq, tq), 1)
    past = col < row

    def tile(h, j, c, acc, diag):
        lanes = slice(h * HEAD_DIM, (h + 1) * HEAD_DIM)
        k0 = pl.multiple_of(j * tq, tq)
        k = k_ref[0, pl.ds(k0, tq), lanes]
        v = v_ref[0, pl.ds(k0, tq), lanes]
        z = lax.dot_general(q_ref[0, :, lanes], k, NT_DIMS, preferred_element_type=F32)
        softplus = jnp.maximum(z, 0.0) + jnp.log(1.0 + jnp.exp(-jnp.abs(z)))
        log_beta = z - softplus
        log_1m = -softplus
        if diag:
            log_1m = jnp.where(past, log_1m, 0.0)
        hi = log_1m.astype(BF16)
        lo = (log_1m - hi.astype(F32)).astype(BF16)
        suffix = (jnp.dot(hi, tri, preferred_element_type=F32)
                  + jnp.dot(lo, tri, preferred_element_type=F32))
        w = jnp.exp(log_beta + c + suffix)
        if diag:
            w = jnp.where(past, w, 0.0)
        acc = acc + jnp.dot(w.astype(BF16), v, preferred_element_type=F32)
        c = c + jnp.sum(log_1m, axis=-1, keepdims=True)
        return c, acc

    def any_alive(state):
        c_max = functools.reduce(jnp.maximum, [jnp.max(c) for c, _ in state])
        return (c_max >= -EXP_UNDERFLOW).astype(jnp.int32)

    state = tuple(tile(h, i, jnp.zeros((tq, 1), F32), jnp.zeros((tq, HEAD_DIM), F32), True)
                  for h in range(heads))

    def keep_going(carry):
        j, alive, _ = carry
        return jnp.logical_and(j >= 0, alive > 0)

    def step(carry):
        j, _, st = carry
        st = tuple(tile(h, j, st[h][0], st[h][1], False) for h in range(heads))
        return j - 1, any_alive(st), st

    _, _, state = lax.while_loop(keep_going, step, (i - 1, any_alive(state), state))
    for h in range(heads):
        o_ref[0, :, h * HEAD_DIM:(h + 1) * HEAD_DIM] = state[h][1].astype(o_ref.dtype)


def _sb_attention(proj, *, tq, heads):
    b, s, _ = proj.shape
    assert s % tq == 0 and SB_HEADS % heads == 0
    tri = jnp.asarray(np.tril(np.ones((tq, tq), np.float32), -1), BF16)
    width = heads * HEAD_DIM
    return pl.pallas_call(
        functools.partial(_sb_body, tq=tq, heads=heads),
        name="sb_attn",
        grid=(b, SB_HEADS // heads, s // tq),
        in_specs=[
            pl.BlockSpec((1, tq, width), lambda bb, h, i: (bb, i, COL_Q_SB // heads + h)),
            pl.BlockSpec((1, s, width), lambda bb, h, i: (bb, 0, COL_K_SB // heads + h)),
            pl.BlockSpec((1, s, width), lambda bb, h, i: (bb, 0, COL_V_SB // heads + h)),
            pl.BlockSpec((tq, tq), lambda bb, h, i: (0, 0)),
        ],
        out_specs=pl.BlockSpec((1, tq, width), lambda bb, h, i: (bb, i, h)),
        out_shape=jax.ShapeDtypeStruct((b, s, SB_HEADS * HEAD_DIM), BF16),
        compiler_params=_params(("parallel", "parallel", "arbitrary")),
    )(proj, proj, proj, tri)


def _compress_body(x_ref, pos_ref, w1_ref, w2_ref, o_ref):
    x = x_ref[0, 0].astype(F32)
    pos = pos_ref[0]
    xa = (x + pos[0:1]).astype(BF16)
    xb = (x + pos[1:2]).astype(BF16)
    first = jnp.dot(xa, w1_ref[0, 0], preferred_element_type=F32)
    second = jnp.dot(xb, w1_ref[0, 1], preferred_element_type=F32)
    n_chunk = x.shape[0]
    pre = first + pltpu.roll(second, n_chunk - 1, 0)
    y = jax.nn.gelu(pre).astype(BF16)
    o_ref[0, 0] = jnp.dot(y, w2_ref[0], preferred_element_type=F32).astype(o_ref.dtype)


def _compress(x, pos, w1, w2):
    _, bg, n_chunk, width = x.shape
    return pl.pallas_call(
        _compress_body,
        name="compress",
        grid=(2, bg),
        in_specs=[
            pl.BlockSpec((1, 1, n_chunk, width), lambda a, n: (a, n, 0, 0)),
            pl.BlockSpec((1, 2, width), lambda a, n: (a, 0, 0)),
            pl.BlockSpec((1, 2, width, HEAD_DIM), lambda a, n: (a, 0, 0, 0)),
            pl.BlockSpec((1, HEAD_DIM, HEAD_DIM), lambda a, n: (a, 0, 0)),
        ],
        out_specs=pl.BlockSpec((1, 1, n_chunk, HEAD_DIM), lambda a, n: (a, n, 0, 0)),
        out_shape=jax.ShapeDtypeStruct((2, bg, n_chunk, HEAD_DIM), BF16),
        compiler_params=_params(("parallel", "parallel")),
    )(x, pos, w1, w2)


def _nsa_body(q_ref, kc_ref, vc_ref, ks_ref, vs_ref, kw_ref, vw_ref, gate_ref,
              pool_ref, expand_ref, o_ref, bias_ref, *, tq, n_slc):
    g = pl.program_id(1)
    i = pl.program_id(2)
    q0 = i * tq
    n_cmp_pad = kc_ref.shape[2]
    slopes = [jnp.where(g == 0, 2.0 ** -(r + 1), 2.0 ** -(r + 1 + NSA_HPG)) for r in range(NSA_HPG)]
    q_heads = [q_ref[0, :, r * HEAD_DIM:(r + 1) * HEAD_DIM] for r in range(NSA_HPG)]
    t_col = q0 + lax.broadcasted_iota(jnp.int32, (tq, 1), 0)

    kc = kc_ref[0, 0]
    vc = vc_ref[0, 0]
    c_row = lax.broadcasted_iota(jnp.int32, (1, n_cmp_pad), 1)
    cmp_end = c_row * CMP_STRIDE + (CMP_BLOCK - 1)
    valid_c = jnp.logical_and(t_col >= cmp_end, c_row < n_cmp_pad - 1)
    col_bias_c = (cmp_end - q0).astype(F32)
    p_grp = jnp.zeros((tq, n_cmp_pad), F32)
    o_cmp = []
    for r in range(NSA_HPG):
        s = lax.dot_general(q_heads[r], kc, NT_DIMS, preferred_element_type=F32)
        s = jnp.where(valid_c, s + slopes[r] * col_bias_c, NEG_INF)
        m = jnp.max(s, axis=-1, keepdims=True)
        p = jnp.where(valid_c, jnp.exp(s - m), 0.0)
        l = jnp.sum(p, axis=-1, keepdims=True)
        p = p * (1.0 / jnp.where(l > 0.0, l, 1.0))
        o_cmp.append(jnp.dot(p.astype(BF16), vc, preferred_element_type=F32))
        p_grp = p_grp + p

    pool = pool_ref[...]
    p_grp = p_grp * SCORE_SCALE
    p1 = p_grp.astype(BF16)
    r1 = p_grp - p1.astype(F32)
    p2 = r1.astype(BF16)
    p3 = (r1 - p2.astype(F32)).astype(BF16)
    slc = (jnp.dot(p1, pool, preferred_element_type=F32)
           + jnp.dot(p2, pool, preferred_element_type=F32)
           + jnp.dot(p3, pool, preferred_element_type=F32))
    blk = lax.broadcasted_iota(jnp.int32, (1, LANES), 1)
    cur = t_col // SLC_BLOCK
    forced = jnp.logical_or(blk == 0, jnp.logical_or(blk == cur, blk == cur - 1))
    score = jnp.where(forced, FORCED_SCORE, slc)
    score = jnp.where(blk <= cur, score, NEG_INF)
    score_t = score.T[:n_slc]

    idx = lax.broadcasted_iota(jnp.int32, (n_slc, tq), 0)
    rank = jnp.zeros((n_slc, tq), F32)
    for ii in range(n_slc):
        row = score_t[ii:ii + 1, :]
        ge = jnp.where(row >= score_t, 1.0, 0.0)
        gt = jnp.where(row > score_t, 1.0, 0.0)
        rank = rank + jnp.where(idx > ii, ge, gt)
    sel_t = jnp.where(rank < float(N_SELECT), 1.0, 0.0)
    sel_t = jnp.concatenate([sel_t, jnp.zeros((LANES - n_slc, tq), F32)], axis=0)
    sel = sel_t.T.astype(BF16)
    key_sel = jnp.dot(sel, expand_ref[...], preferred_element_type=F32)
    bias_ref[...] = (key_sel - 1.0) * (-NEG_INF)

    row_i = lax.broadcasted_iota(jnp.int32, (tq, tq), 0)
    col_i = lax.broadcasted_iota(jnp.int32, (tq, tq), 1)
    col_f = lax.broadcasted_iota(jnp.int32, (1, tq), 1)

    def attend(k_ref, v_ref, j, mask_bias, state):
        k0 = pl.multiple_of(j * tq, tq)
        k = k_ref[0, pl.ds(k0, tq), :]
        v = v_ref[0, pl.ds(k0, tq), :]
        col_bias = (k0 - q0 + col_f).astype(F32)
        new_state = []
        for r in range(NSA_HPG):
            m, l, acc = state[r]
            s = lax.dot_general(q_heads[r], k, NT_DIMS, preferred_element_type=F32)
            s = s + slopes[r] * col_bias + mask_bias
            m_new = jnp.maximum(m, jnp.max(s, axis=-1, keepdims=True))
            alpha = jnp.exp(m - m_new)
            p = jnp.exp(s - m_new)
            l = alpha * l + jnp.sum(p, axis=-1, keepdims=True)
            acc = alpha * acc + jnp.dot(p.astype(BF16), v, preferred_element_type=F32)
            new_state.append((m_new, l, acc))
        return tuple(new_state)

    def init_state():
        return tuple((jnp.full((tq, 1), NEG_INF, F32), jnp.zeros((tq, 1), F32),
                      jnp.zeros((tq, HEAD_DIM), F32)) for _ in range(NSA_HPG))

    causal_bias = jnp.where(col_i <= row_i, 0.0, NEG_INF)

    diag0 = pl.multiple_of(q0, tq)
    st = attend(ks_ref, vs_ref, i, bias_ref[:, pl.ds(diag0, tq)] + causal_bias, init_state())

    def slc_step(j, state):
        return attend(ks_ref, vs_ref, j, bias_ref[:, pl.ds(pl.multiple_of(j * tq, tq), tq)], state)

    st = lax.fori_loop(0, i, slc_step, st)
    o_slc = [acc * (1.0 / l) for (_, l, acc) in st]

    st = attend(kw_ref, vw_ref, i, causal_bias, init_state())
    for d in range(1, (WINDOW - 1) // tq + 2):
        dist = d * tq + row_i - col_i
        in_window = jnp.logical_and(dist < WINDOW, i - d >= 0)
        st = attend(kw_ref, vw_ref, jnp.maximum(i - d, 0), jnp.where(in_window, 0.0, NEG_INF), st)
    o_win = [acc * (1.0 / l) for (_, l, acc) in st]

    sg = jax.nn.sigmoid(gate_ref[0])

    def gate(branch, r):
        c0 = branch * NSA_HEADS + r
        c1 = c0 + NSA_HPG
        return jnp.where(g == 0, sg[:, c0:c0 + 1], sg[:, c1:c1 + 1])

    for r in range(NSA_HPG):
        out = gate(0, r) * o_cmp[r] + gate(1, r) * o_slc[r] + gate(2, r) * o_win[r]
        o_ref[0, :, r * HEAD_DIM:(r + 1) * HEAD_DIM] = out.astype(o_ref.dtype)


def _nsa_attention(proj, kvc, gm, *, tq):
    b, s, _ = proj.shape
    n_cmp_pad = s // CMP_STRIDE
    n_slc = s // SLC_BLOCK
    assert s % tq == 0 and tq % SLC_BLOCK == 0 and n_slc <= LANES and n_cmp_pad % LANES == 0
    ratio = SLC_BLOCK // CMP_STRIDE
    span = CMP_BLOCK // CMP_STRIDE
    pool = np.zeros((n_cmp_pad, LANES), np.float32)
    for jj in range(n_slc):
        for mm in range(ratio):
            for nn in range(span):
                c = ratio * jj + mm + nn
                if c < n_cmp_pad:
                    pool[c, jj] += 1.0
    expand = np.zeros((LANES, s), np.float32)
    expand[np.arange(s) // SLC_BLOCK, np.arange(s)] = 1.0
    grp_w = NSA_HPG * HEAD_DIM
    kv_spec = lambda col: pl.BlockSpec((1, s, HEAD_DIM), lambda bb, g, i: (bb, 0, col + g))
    return pl.pallas_call(
        functools.partial(_nsa_body, tq=tq, n_slc=n_slc),
        name="nsa_attn",
        grid=(b, NSA_GROUPS, s // tq),
        in_specs=[
            pl.BlockSpec((1, tq, grp_w), lambda bb, g, i: (bb, i, COL_Q_NSA // NSA_HPG + g)),
            pl.BlockSpec((1, 1, n_cmp_pad, HEAD_DIM), lambda bb, g, i: (0, bb * NSA_GROUPS + g, 0, 0)),
            pl.BlockSpec((1, 1, n_cmp_pad, HEAD_DIM), lambda bb, g, i: (1, bb * NSA_GROUPS + g, 0, 0)),
            kv_spec(COL_K_SLC), kv_spec(COL_V_SLC), kv_spec(COL_K_WIN), kv_spec(COL_V_WIN),
            pl.BlockSpec((1, tq, GATE_PAD), lambda bb, g, i: (bb, i, 0)),
            pl.BlockSpec((n_cmp_pad, LANES), lambda bb, g, i: (0, 0)),
            pl.BlockSpec((LANES, s), lambda bb, g, i: (0, 0)),
        ],
        out_specs=pl.BlockSpec((1, tq, grp_w), lambda bb, g, i: (bb, i, g)),
        out_shape=jax.ShapeDtypeStruct((b, s, NSA_HEADS * HEAD_DIM), BF16),
        scratch_shapes=[pltpu.VMEM((tq, s), F32)],
        compiler_params=_params(("parallel", "parallel", "arbitrary")),
    )(proj, kvc, kvc, proj, proj, proj, proj, gm, jnp.asarray(pool, BF16), jnp.asarray(expand, BF16))


def _merge_body(sb_ref, nsa_ref, m0_ref, m1_ref, x_ref, wsb_ref, wnsa_ref, wout_ref, g_ref,
                o_ref, acc_ref):
    j = pl.program_id(1)

    @pl.when(j == 0)
    def _():
        acc_ref[...] = jnp.zeros_like(acc_ref)

    y_sb = jnp.dot(sb_ref[...], wsb_ref[...], preferred_element_type=F32)
    y_nsa = jnp.dot(nsa_ref[...], wnsa_ref[...], preferred_element_type=F32)
    merged = jax.nn.sigmoid(m0_ref[...]) * y_sb + jax.nn.sigmoid(m1_ref[...]) * y_nsa
    acc_ref[...] += jnp.dot(merged.astype(BF16), wout_ref[...], preferred_element_type=F32)

    @pl.when(j == pl.num_programs(1) - 1)
    def _():
        o_ref[...] = x_ref[...] + _rms(acc_ref[...], g_ref[...])


def _merge_out(sb, nsa, gm, x, w_sb, w_nsa, w_out, g_post, *, tm, tn):
    m, d = x.shape
    assert m % tm == 0 and d % tn == 0 and gm.shape == (m, 2 * d)
    nj = d // tn
    return pl.pallas_call(
        _merge_body,
        name="merge_out",
        grid=(m // tm, nj),
        in_specs=[
            pl.BlockSpec((tm, sb.shape[1]), lambda i, j: (i, 0)),
            pl.BlockSpec((tm, nsa.shape[1]), lambda i, j: (i, 0)),
            pl.BlockSpec((tm, tn), lambda i, j: (i, j)),
            pl.BlockSpec((tm, tn), lambda i, j: (i, j + nj)),
            pl.BlockSpec((tm, d), lambda i, j: (i, 0)),
            pl.BlockSpec((sb.shape[1], tn), lambda i, j: (0, j)),
            pl.BlockSpec((nsa.shape[1], tn), lambda i, j: (0, j)),
            pl.BlockSpec((tn, d), lambda i, j: (j, 0)),
            pl.BlockSpec((1, d), lambda i, j: (0, 0)),
        ],
        out_specs=pl.BlockSpec((tm, d), lambda i, j: (i, 0)),
        out_shape=jax.ShapeDtypeStruct((m, d), F32),
        scratch_shapes=[pltpu.VMEM((tm, d), F32)],
        compiler_params=_params(("parallel", "arbitrary")),
    )(sb, nsa, gm, gm, x, w_sb, w_nsa, w_out, g_post)


def kernel(x, ffn1_pre_g, ffn1_w_in, ffn1_w_out, ffn1_post_g, mix_pre_g, w_in, cmp_pos_k, cmp_k_w1, cmp_k_w2, cmp_pos_v, cmp_v_w1, cmp_v_w2, w_branch_sb, w_branch_nsa, w_out, mix_post_g, ffn2_pre_g, ffn2_w_in, ffn2_w_out, ffn2_post_g):
    b, s, d = x.shape
    m = b * s
    depth = ffn1_pre_g.shape[0]
    h = x.reshape(m, d)
    for l in range(depth):
        h = _ffn(h, ffn1_pre_g[l][None], ffn1_w_in[l].astype(BF16), ffn1_w_out[l].astype(BF16),
                 ffn1_post_g[l][None], tm=512, tf=512)

        w = w_in[l]
        merge_logit_w = w[:, QKV_COLS + N_GATE_LOGITS:]
        w_gates = jnp.pad(w[:, QKV_COLS:QKV_COLS + N_GATE_LOGITS], ((0, 0), (0, GATE_PAD - N_GATE_LOGITS)))
        g_mix = mix_pre_g[l][None]
        is_q = np.zeros((QKV_COLS // HEAD_DIM,), bool)
        is_q[COL_Q_SB:COL_Q_SB + SB_HEADS] = True
        is_q[COL_Q_NSA:COL_Q_NSA + NSA_HEADS] = True
        col_scale = jnp.asarray(np.repeat(np.where(is_q, HEAD_DIM ** -0.5, 1.0), HEAD_DIM), F32)
        w_qkv = (w[:, :QKV_COLS] * col_scale[None, :]).astype(BF16)
        proj = _norm_matmul(h, g_mix, w_qkv, BF16, tm=512, tn=512, name="in_proj_qkv")
        gates = _norm_matmul(h, g_mix, w_gates.astype(BF16), F32, tm=512, tn=GATE_PAD, name="in_proj_gates")
        merge = _norm_matmul(h, g_mix, merge_logit_w.astype(BF16), F32, tm=512, tn=512, name="in_proj_merge")
        proj = proj.reshape(b, s, QKV_COLS)

        sb = _sb_attention(proj, tq=256, heads=2)

        n_chunk = s // CMP_STRIDE
        kv = proj[:, :, COL_K_CMP * HEAD_DIM:COL_K_SLC * HEAD_DIM]
        kv = kv.reshape(b, n_chunk, CMP_STRIDE, 2, NSA_GROUPS, HEAD_DIM)
        kv = jnp.transpose(kv, (3, 0, 4, 1, 2, 5)).reshape(2, b * NSA_GROUPS, n_chunk, CMP_STRIDE * HEAD_DIM)
        half = CMP_STRIDE * HEAD_DIM
        pos = jnp.stack([cmp_pos_k[l], cmp_pos_v[l]]).reshape(2, 2, half)
        w1 = jnp.stack([cmp_k_w1[l], cmp_v_w1[l]]).astype(BF16).reshape(2, 2, half, HEAD_DIM)
        w2 = jnp.stack([cmp_k_w2[l], cmp_v_w2[l]]).astype(BF16)
        kvc = _compress(kv, pos, w1, w2)

        nsa = _nsa_attention(proj, kvc, gates.reshape(b, s, GATE_PAD), tq=256)

        h = _merge_out(sb.reshape(m, -1), nsa.reshape(m, -1), merge, h,
                       w_branch_sb[l].astype(BF16), w_branch_nsa[l].astype(BF16), w_out[l].astype(BF16),
                       mix_post_g[l][None], tm=512, tn=512)

        h = _ffn(h, ffn2_pre_g[l][None], ffn2_w_in[l].astype(BF16), ffn2_w_out[l].astype(BF16),
                 ffn2_post_g[l][None], tm=512, tf=512)
    return h.reshape(b, s, d)
```

```python
import functools

import numpy as np
import jax
import jax.numpy as jnp
from jax import lax
from jax.experimental import pallas as pl
from jax.experimental.pallas import tpu as pltpu

HEAD_DIM = 128
SB_HEADS = 8
NSA_HEADS = 8
NSA_GROUPS = 2
NSA_HPG = NSA_HEADS // NSA_GROUPS
CMP_BLOCK = 32
CMP_STRIDE = 16
SLC_BLOCK = 64
N_SELECT = 16
WINDOW = 512
NORM_EPS = 1e-6
NEG_INF = -1e30
SCORE_SCALE = 2.0 ** 64
FORCED_SCORE = 1e30

QKV_COLS = (3 * SB_HEADS + NSA_HEADS + 6 * NSA_GROUPS) * HEAD_DIM
COL_Q_SB = 0
COL_K_SB = SB_HEADS
COL_V_SB = 2 * SB_HEADS
COL_Q_NSA = 3 * SB_HEADS
COL_K_CMP = COL_Q_NSA + NSA_HEADS
COL_K_SLC = COL_K_CMP + 2 * NSA_GROUPS
COL_V_SLC = COL_K_SLC + NSA_GROUPS
COL_K_WIN = COL_V_SLC + NSA_GROUPS
COL_V_WIN = COL_K_WIN + NSA_GROUPS
N_GATE_LOGITS = 3 * NSA_HEADS
GATE_PAD = 128

LANES = 128
V7X_VMEM_LIMIT = 56 * 1024 * 1024

F32 = jnp.float32
BF16 = jnp.bfloat16
NT_DIMS = (((1,), (1,)), ((), ()))


def _params(semantics):
    return pltpu.CompilerParams(dimension_semantics=semantics,
                                vmem_limit_bytes=V7X_VMEM_LIMIT)


def _rms(x, g):
    ms = jnp.mean(x * x, axis=-1, keepdims=True)
    return x * lax.rsqrt(ms + NORM_EPS) * g


def _ffn_body(x_ref, gpre_ref, wg_ref, wu_ref, wo_ref, gpost_ref, o_ref, h_ref, acc_ref):
    j = pl.program_id(1)

    @pl.when(j == 0)
    def _():
        h_ref[...] = _rms(x_ref[...], gpre_ref[...]).astype(BF16)
        acc_ref[...] = jnp.zeros_like(acc_ref)

    h = h_ref[...]
    gate = jnp.dot(h, wg_ref[...], preferred_element_type=F32)
    up = jnp.dot(h, wu_ref[...], preferred_element_type=F32)
    act = (gate * jax.nn.sigmoid(gate)) * up
    acc_ref[...] += jnp.dot(act.astype(BF16), wo_ref[...], preferred_element_type=F32)

    @pl.when(j == pl.num_programs(1) - 1)
    def _():
        o_ref[...] = x_ref[...] + 0.5 * _rms(acc_ref[...], gpost_ref[...])


def _ffn(x, g_pre, w_in, w_out, g_post, *, tm, tf):
    m, d = x.shape
    f = w_out.shape[0]
    nf = f // tf
    assert m % tm == 0 and f % tf == 0 and w_in.shape == (d, 2 * f)
    return pl.pallas_call(
        _ffn_body,
        name="ffn",
        grid=(m // tm, nf),
        in_specs=[
            pl.BlockSpec((tm, d), lambda i, j: (i, 0)),
            pl.BlockSpec((1, d), lambda i, j: (0, 0)),
            pl.BlockSpec((d, tf), lambda i, j: (0, j)),
            pl.BlockSpec((d, tf), lambda i, j: (0, j + nf)),
            pl.BlockSpec((tf, d), lambda i, j: (j, 0)),
            pl.BlockSpec((1, d), lambda i, j: (0, 0)),
        ],
        out_specs=pl.BlockSpec((tm, d), lambda i, j: (i, 0)),
        out_shape=jax.ShapeDtypeStruct((m, d), F32),
        scratch_shapes=[pltpu.VMEM((tm, d), BF16), pltpu.VMEM((tm, d), F32)],
        compiler_params=_params(("parallel", "arbitrary")),
    )(x, g_pre, w_in, w_in, w_out, g_post)


def _norm_matmul_body(x_ref, g_ref, w_ref, o_ref, h_ref):
    @pl.when(pl.program_id(1) == 0)
    def _():
        h_ref[...] = _rms(x_ref[...], g_ref[...]).astype(BF16)

    o_ref[...] = jnp.dot(h_ref[...], w_ref[...], preferred_element_type=F32).astype(o_ref.dtype)


def _norm_matmul(x, g, w, out_dtype, *, tm, tn, name):
    m, d = x.shape
    n = w.shape[1]
    assert m % tm == 0 and n % tn == 0
    return pl.pallas_call(
        _norm_matmul_body,
        name=name,
        grid=(m // tm, n // tn),
        in_specs=[
            pl.BlockSpec((tm, d), lambda i, j: (i, 0)),
            pl.BlockSpec((1, d), lambda i, j: (0, 0)),
            pl.BlockSpec((d, tn), lambda i, j: (0, j)),
        ],
        out_specs=pl.BlockSpec((tm, tn), lambda i, j: (i, j)),
        out_shape=jax.ShapeDtypeStruct((m, n), out_dtype),
        scratch_shapes=[pltpu.VMEM((tm, d), BF16)],
        compiler_params=_params(("parallel", "arbitrary")),
    )(x, g, w)


EXP_UNDERFLOW = 104.0


def _sb_body(q_ref, k_ref, v_ref, tri_ref, o_ref, *, tq, heads):
    i = pl.program_id(2)
    tri = tri_ref[...]
    row = lax.broadcasted_iota(jnp.int32, (tq, tq), 0)
    col = lax.broadcasted_iota(jnp.int32, (tq, tq), 1)
    past = col < row

    def tile(h, j, c, acc, diag):
        lanes = slice(h * HEAD_DIM, (h + 1) * HEAD_DIM)
        k0 = pl.multiple_of(j * tq, tq)
        k = k_ref[0, pl.ds(k0, tq), lanes]
        v = v_ref[0, pl.ds(k0, tq), lanes]
        z = lax.dot_general(q_ref[0, :, lanes], k, NT_DIMS, preferred_element_type=F32)
        softplus = jnp.maximum(z, 0.0) + jnp.log(1.0 + jnp.exp(-jnp.abs(z)))
        log_beta = z - softplus
        log_1m = -softplus
        if diag:
            log_1m = jnp.where(past, log_1m, 0.0)
        hi = log_1m.astype(BF16)
        lo = (log_1m - hi.astype(F32)).astype(BF16)
        suffix = (jnp.dot(hi, tri, preferred_element_type=F32)
                  + jnp.dot(lo, tri, preferred_element_type=F32))
        w = jnp.exp(log_beta + c + suffix)
        if diag:
            w = jnp.where(past, w, 0.0)
        acc = acc + jnp.dot(w.astype(BF16), v, preferred_element_type=F32)
        c = c + jnp.sum(log_1m, axis=-1, keepdims=True)
        return c, acc

    def any_alive(state):
        c_max = functools.reduce(jnp.maximum, [jnp.max(c) for c, _ in state])
        return (c_max >= -EXP_UNDERFLOW).astype(jnp.int32)

    state = tuple(tile(h, i, jnp.zeros((tq, 1), F32), jnp.zeros((tq, HEAD_DIM), F32), True)
                  for h in range(heads))

    def keep_going(carry):
        j, alive, _ = carry
        return jnp.logical_and(j >= 0, alive > 0)

    def step(carry):
        j, _, st = carry
        st = tuple(tile(h, j, st[h][0], st[h][1], False) for h in range(heads))
        return j - 1, any_alive(st), st

    _, _, state = lax.while_loop(keep_going, step, (i - 1, any_alive(state), state))
    for h in range(heads):
        o_ref[0, :, h * HEAD_DIM:(h + 1) * HEAD_DIM] = state[h][1].astype(o_ref.dtype)


def _sb_attention(proj, *, tq, heads):
    b, s, _ = proj.shape
    assert s % tq == 0 and SB_HEADS % heads == 0
    tri = jnp.asarray(np.tril(np.ones((tq, tq), np.float32), -1), BF16)
    width = heads * HEAD_DIM
    return pl.pallas_call(
        functools.partial(_sb_body, tq=tq, heads=heads),
        name="sb_attn",
        grid=(b, SB_HEADS // heads, s // tq),
        in_specs=[
            pl.BlockSpec((1, tq, width), lambda bb, h, i: (bb, i, COL_Q_SB // heads + h)),
            pl.BlockSpec((1, s, width), lambda bb, h, i: (bb, 0, COL_K_SB // heads + h)),
            pl.BlockSpec((1, s, width), lambda bb, h, i: (bb, 0, COL_V_SB // heads + h)),
            pl.BlockSpec((tq, tq), lambda bb, h, i: (0, 0)),
        ],
        out_specs=pl.BlockSpec((1, tq, width), lambda bb, h, i: (bb, i, h)),
        out_shape=jax.ShapeDtypeStruct((b, s, SB_HEADS * HEAD_DIM), BF16),
        compiler_params=_params(("parallel", "parallel", "arbitrary")),
    )(proj, proj, proj, tri)


def _compress_body(x_ref, pos_ref, w1_ref, w2_ref, o_ref):
    x = x_ref[0, 0].astype(F32)
    pos = pos_ref[0]
    xa = (x + pos[0:1]).astype(BF16)
    xb = (x + pos[1:2]).astype(BF16)
    first = jnp.dot(xa, w1_ref[0, 0], preferred_element_type=F32)
    second = jnp.dot(xb, w1_ref[0, 1], preferred_element_type=F32)
    n_chunk = x.shape[0]
    pre = first + pltpu.roll(second, n_chunk - 1, 0)
    y = jax.nn.gelu(pre).astype(BF16)
    o_ref[0, 0] = jnp.dot(y, w2_ref[0], preferred_element_type=F32).astype(o_ref.dtype)


def _compress(x, pos, w1, w2):
    _, bg, n_chunk, width = x.shape
    return pl.pallas_call(
        _compress_body,
        name="compress",
        grid=(2, bg),
        in_specs=[
            pl.BlockSpec((1, 1, n_chunk, width), lambda a, n: (a, n, 0, 0)),
            pl.BlockSpec((1, 2, width), lambda a, n: (a, 0, 0)),
            pl.BlockSpec((1, 2, width, HEAD_DIM), lambda a, n: (a, 0, 0, 0)),
            pl.BlockSpec((1, HEAD_DIM, HEAD_DIM), lambda a, n: (a, 0, 0)),
        ],
        out_specs=pl.BlockSpec((1, 1, n_chunk, HEAD_DIM), lambda a, n: (a, n, 0, 0)),
        out_shape=jax.ShapeDtypeStruct((2, bg, n_chunk, HEAD_DIM), BF16),
        compiler_params=_params(("parallel", "parallel")),
    )(x, pos, w1, w2)


POS_HI_COL = 0
POS_LO_COL = 1
BLOCK_COL0 = HEAD_DIM // 2


def _key_extra_columns(s, tk):
    pos = np.arange(s)
    extra = np.zeros((s, HEAD_DIM), np.float32)
    extra[:, POS_HI_COL] = (pos % tk) // SLC_BLOCK
    extra[:, POS_LO_COL] = pos % SLC_BLOCK
    extra[pos, BLOCK_COL0 + pos // SLC_BLOCK] = 1.0
    return jnp.asarray(extra, BF16)


def _nsa_body(q_ref, kc_ref, vct_ref, ks_ref, vst_ref, kw_ref, vwt_ref, gate_ref,
              pool_ref, kextra_ref, o_ref, ksa_ref, kwa_ref, *, tq, n_slc):
    g = pl.program_id(1)
    i = pl.program_id(2)
    q0 = i * tq
    n_cmp_pad = kc_ref.shape[2]

    @pl.when(i == 0)
    def _():
        ksa_ref[:, :HEAD_DIM] = ks_ref[0]
        ksa_ref[:, HEAD_DIM:] = kextra_ref[...]
        kwa_ref[:, :HEAD_DIM] = kw_ref[0]
        kwa_ref[:, HEAD_DIM:] = kextra_ref[...]

    slopes = [jnp.where(g == 0, 2.0 ** -(r + 1), 2.0 ** -(r + 1 + NSA_HPG)) for r in range(NSA_HPG)]
    q_heads = [q_ref[0, :, r * HEAD_DIM:(r + 1) * HEAD_DIM] for r in range(NSA_HPG)]
    t_row = q0 + lax.broadcasted_iota(jnp.int32, (1, tq), 1)

    wide = NSA_HPG * tq
    head_of_lane = lax.broadcasted_iota(jnp.int32, (1, wide), 1) // tq
    slope_row = jnp.zeros((1, wide), F32)
    for r in range(NSA_HPG):
        slope_row = jnp.where(head_of_lane == r, slopes[r], slope_row)

    def split_heads(x):
        return [x[:, r * tq:(r + 1) * tq] for r in range(NSA_HPG)]

    kc = kc_ref[0, 0]
    vct = vct_ref[0, 0]
    c_idx = lax.broadcasted_iota(jnp.int32, (n_cmp_pad, wide), 0)
    t_wide = q0 + lax.rem(lax.broadcasted_iota(jnp.int32, (1, wide), 1), tq)
    cmp_end = c_idx * CMP_STRIDE + (CMP_BLOCK - 1)
    valid_c = jnp.logical_and(t_wide >= cmp_end, c_idx < n_cmp_pad - 1)
    s = lax.dot_general(kc, jnp.concatenate(q_heads, axis=0), NT_DIMS, preferred_element_type=F32)
    s = jnp.where(valid_c, s + slope_row * (cmp_end - q0).astype(F32), NEG_INF)
    m = jnp.max(s, axis=0, keepdims=True)
    p = jnp.where(valid_c, jnp.exp(s - m), 0.0)
    l = jnp.sum(p, axis=0, keepdims=True)
    p = p * (1.0 / jnp.where(l > 0.0, l, 1.0))
    o_cmp = split_heads(jnp.dot(vct, p.astype(BF16), preferred_element_type=F32))
    p_grp = functools.reduce(lambda a, b: a + b, split_heads(p))

    pool = pool_ref[...]
    p_grp = p_grp * SCORE_SCALE
    p1 = p_grp.astype(BF16)
    r1 = p_grp - p1.astype(F32)
    p2 = r1.astype(BF16)
    p3 = (r1 - p2.astype(F32)).astype(BF16)
    score = (jnp.dot(pool, p1, preferred_element_type=F32)
             + jnp.dot(pool, p2, preferred_element_type=F32)
             + jnp.dot(pool, p3, preferred_element_type=F32))
    blk = lax.broadcasted_iota(jnp.int32, (n_slc, tq), 0)
    cur = t_row // SLC_BLOCK
    forced = jnp.logical_or(blk == 0, jnp.logical_or(blk == cur, blk == cur - 1))
    score = jnp.where(forced, FORCED_SCORE, score)
    score = jnp.where(blk <= cur, score, NEG_INF)

    rank = jnp.zeros((n_slc, tq), F32)
    for ii in range(n_slc):
        row = score[ii:ii + 1, :]
        ge = jnp.where(row >= score, 1.0, 0.0)
        gt = jnp.where(row > score, 1.0, 0.0)
        rank = rank + jnp.where(blk > ii, ge, gt)
    sel_bias_t = jnp.where(rank < float(N_SELECT), 0.0, NEG_INF)

    pieces = [jnp.zeros((BLOCK_COL0, tq), F32), sel_bias_t]
    if BLOCK_COL0 + n_slc < HEAD_DIM:
        pieces.append(jnp.zeros((HEAD_DIM - BLOCK_COL0 - n_slc, tq), F32))
    sel_extra = jnp.concatenate(pieces, axis=0).T
    lane = lax.broadcasted_iota(jnp.int32, (tq, HEAD_DIM), 1)

    def query_slab(r, extra):
        pos_cols = jnp.where(lane == POS_HI_COL, slopes[r] * SLC_BLOCK,
                             jnp.where(lane == POS_LO_COL, slopes[r], extra))
        return jnp.concatenate([q_heads[r], pos_cols.astype(BF16)], axis=1)

    q_slc = jnp.concatenate([query_slab(r, sel_extra) for r in range(NSA_HPG)], axis=0)
    q_win = jnp.concatenate([query_slab(r, 0.0) for r in range(NSA_HPG)], axis=0)

    key_i = lax.broadcasted_iota(jnp.int32, (tq, wide), 0)
    qry_i = lax.rem(lax.broadcasted_iota(jnp.int32, (tq, wide), 1), tq)

    def attend(ka_ref, vt_ref, q_all, j, mask, state):
        m, l, acc = state
        k0 = pl.multiple_of(j * tq, tq)
        ka = ka_ref[pl.ds(k0, tq), :]
        vt = vt_ref[0, :, pl.ds(k0, tq)]
        s = lax.dot_general(ka, q_all, NT_DIMS, preferred_element_type=F32)
        if mask is not None:
            s = jnp.where(mask, s, NEG_INF)
        off = slope_row * (k0 - q0).astype(F32)
        m_new = jnp.maximum(m, jnp.max(s, axis=0, keepdims=True) + off)
        alpha = jnp.exp(m - m_new)
        p = jnp.exp(s - (m_new - off))
        l = alpha * l + jnp.sum(p, axis=0, keepdims=True)
        acc = alpha * acc + jnp.dot(vt, p.astype(BF16), preferred_element_type=F32)
        return m_new, l, acc

    def init_state():
        return (jnp.full((1, wide), NEG_INF, F32), jnp.zeros((1, wide), F32),
                jnp.zeros((HEAD_DIM, wide), F32))

    def finish(state):
        _, l, acc = state
        return split_heads(acc * (1.0 / l))

    causal = key_i <= qry_i

    st = attend(ksa_ref, vst_ref, q_slc, i, causal, init_state())
    st = lax.fori_loop(0, i, lambda j, state: attend(ksa_ref, vst_ref, q_slc, j, None, state), st)
    o_slc = finish(st)

    st = attend(kwa_ref, vwt_ref, q_win, i, causal, init_state())
    for d in range(1, (WINDOW - 1) // tq + 2):
        dist = d * tq + qry_i - key_i
        in_window = jnp.logical_and(dist < WINDOW, i - d >= 0)
        st = attend(kwa_ref, vwt_ref, q_win, jnp.maximum(i - d, 0), in_window, st)
    o_win = finish(st)

    sg_t = jax.nn.sigmoid(gate_ref[0]).T

    def gate(branch, r):
        c0 = branch * NSA_HEADS + r
        c1 = c0 + NSA_HPG
        return jnp.where(g == 0, sg_t[c0:c0 + 1, :], sg_t[c1:c1 + 1, :])

    for r in range(NSA_HPG):
        out_t = gate(0, r) * o_cmp[r] + gate(1, r) * o_slc[r] + gate(2, r) * o_win[r]
        o_ref[0, :, r * HEAD_DIM:(r + 1) * HEAD_DIM] = out_t.T.astype(o_ref.dtype)


def _transposed_values(proj, kvc):
    b, s, _ = proj.shape
    grp_cols = NSA_GROUPS * HEAD_DIM
    vc_t = jnp.transpose(kvc[1].reshape(b, NSA_GROUPS, s // CMP_STRIDE, HEAD_DIM), (0, 1, 3, 2))
    vs_t = jnp.transpose(proj[:, :, COL_V_SLC * HEAD_DIM:COL_V_SLC * HEAD_DIM + grp_cols], (0, 2, 1))
    vw_t = jnp.transpose(proj[:, :, COL_V_WIN * HEAD_DIM:COL_V_WIN * HEAD_DIM + grp_cols], (0, 2, 1))
    return vc_t, vs_t, vw_t


def _nsa_attention(proj, kvc, vc_t, vs_t, vw_t, gm, *, tq):
    b, s, _ = proj.shape
    n_cmp_pad = s // CMP_STRIDE
    n_slc = s // SLC_BLOCK
    assert s % tq == 0 and tq % SLC_BLOCK == 0 and n_cmp_pad % LANES == 0
    assert BLOCK_COL0 + n_slc <= HEAD_DIM and tq // SLC_BLOCK <= 256 and n_slc % 8 == 0
    ratio = SLC_BLOCK // CMP_STRIDE
    span = CMP_BLOCK // CMP_STRIDE
    pool = np.zeros((n_slc, n_cmp_pad), np.float32)
    for jj in range(n_slc):
        for mm in range(ratio):
            for nn in range(span):
                c = ratio * jj + mm + nn
                if c < n_cmp_pad:
                    pool[jj, c] += 1.0
    grp_w = NSA_HPG * HEAD_DIM
    k_spec = lambda col: pl.BlockSpec((1, s, HEAD_DIM), lambda bb, g, i: (bb, 0, col + g))
    vt_spec = pl.BlockSpec((1, HEAD_DIM, s), lambda bb, g, i: (bb, g, 0))
    return pl.pallas_call(
        functools.partial(_nsa_body, tq=tq, n_slc=n_slc),
        name="nsa_attn",
        grid=(b, NSA_GROUPS, s // tq),
        in_specs=[
            pl.BlockSpec((1, tq, grp_w), lambda bb, g, i: (bb, i, COL_Q_NSA // NSA_HPG + g)),
            pl.BlockSpec((1, 1, n_cmp_pad, HEAD_DIM), lambda bb, g, i: (0, bb * NSA_GROUPS + g, 0, 0)),
            pl.BlockSpec((1, 1, HEAD_DIM, n_cmp_pad), lambda bb, g, i: (bb, g, 0, 0)),
            k_spec(COL_K_SLC), vt_spec, k_spec(COL_K_WIN), vt_spec,
            pl.BlockSpec((1, tq, GATE_PAD), lambda bb, g, i: (bb, i, 0)),
            pl.BlockSpec((n_slc, n_cmp_pad), lambda bb, g, i: (0, 0)),
            pl.BlockSpec((s, HEAD_DIM), lambda bb, g, i: (0, 0)),
        ],
        out_specs=pl.BlockSpec((1, tq, grp_w), lambda bb, g, i: (bb, i, g)),
        out_shape=jax.ShapeDtypeStruct((b, s, NSA_HEADS * HEAD_DIM), BF16),
        scratch_shapes=[pltpu.VMEM((s, 2 * HEAD_DIM), BF16), pltpu.VMEM((s, 2 * HEAD_DIM), BF16)],
        compiler_params=_params(("parallel", "parallel", "arbitrary")),
    )(proj, kvc, vc_t, proj, vs_t, proj, vw_t, gm, jnp.asarray(pool, BF16), _key_extra_columns(s, tq))


def _merge_body(sb_ref, nsa_ref, m0_ref, m1_ref, x_ref, wsb_ref, wnsa_ref, wout_ref, g_ref,
                o_ref, acc_ref):
    j = pl.program_id(1)

    @pl.when(j == 0)
    def _():
        acc_ref[...] = jnp.zeros_like(acc_ref)

    y_sb = jnp.dot(sb_ref[...], wsb_ref[...], preferred_element_type=F32)
    y_nsa = jnp.dot(nsa_ref[...], wnsa_ref[...], preferred_element_type=F32)
    merged = jax.nn.sigmoid(m0_ref[...]) * y_sb + jax.nn.sigmoid(m1_ref[...]) * y_nsa
    acc_ref[...] += jnp.dot(merged.astype(BF16), wout_ref[...], preferred_element_type=F32)

    @pl.when(j == pl.num_programs(1) - 1)
    def _():
        o_ref[...] = x_ref[...] + _rms(acc_ref[...], g_ref[...])


def _merge_out(sb, nsa, gm, x, w_sb, w_nsa, w_out, g_post, *, tm, tn):
    m, d = x.shape
    assert m % tm == 0 and d % tn == 0 and gm.shape == (m, 2 * d)
    nj = d // tn
    return pl.pallas_call(
        _merge_body,
        name="merge_out",
        grid=(m // tm, nj),
        in_specs=[
            pl.BlockSpec((tm, sb.shape[1]), lambda i, j: (i, 0)),
            pl.BlockSpec((tm, nsa.shape[1]), lambda i, j: (i, 0)),
            pl.BlockSpec((tm, tn), lambda i, j: (i, j)),
            pl.BlockSpec((tm, tn), lambda i, j: (i, j + nj)),
            pl.BlockSpec((tm, d), lambda i, j: (i, 0)),
            pl.BlockSpec((sb.shape[1], tn), lambda i, j: (0, j)),
            pl.BlockSpec((nsa.shape[1], tn), lambda i, j: (0, j)),
            pl.BlockSpec((tn, d), lambda i, j: (j, 0)),
            pl.BlockSpec((1, d), lambda i, j: (0, 0)),
        ],
        out_specs=pl.BlockSpec((tm, d), lambda i, j: (i, 0)),
        out_shape=jax.ShapeDtypeStruct((m, d), F32),
        scratch_shapes=[pltpu.VMEM((tm, d), F32)],
        compiler_params=_params(("parallel", "arbitrary")),
    )(sb, nsa, gm, gm, x, w_sb, w_nsa, w_out, g_post)


def kernel(x, ffn1_pre_g, ffn1_w_in, ffn1_w_out, ffn1_post_g, mix_pre_g, w_in, cmp_pos_k, cmp_k_w1, cmp_k_w2, cmp_pos_v, cmp_v_w1, cmp_v_w2, w_branch_sb, w_branch_nsa, w_out, mix_post_g, ffn2_pre_g, ffn2_w_in, ffn2_w_out, ffn2_post_g):
    b, s, d = x.shape
    m = b * s
    depth = ffn1_pre_g.shape[0]
    h = x.reshape(m, d)
    for l in range(depth):
        h = _ffn(h, ffn1_pre_g[l][None], ffn1_w_in[l].astype(BF16), ffn1_w_out[l].astype(BF16),
                 ffn1_post_g[l][None], tm=512, tf=512)

        w = w_in[l]
        merge_logit_w = w[:, QKV_COLS + N_GATE_LOGITS:]
        w_gates = jnp.pad(w[:, QKV_COLS:QKV_COLS + N_GATE_LOGITS], ((0, 0), (0, GATE_PAD - N_GATE_LOGITS)))
        g_mix = mix_pre_g[l][None]
        is_q = np.zeros((QKV_COLS // HEAD_DIM,), bool)
        is_q[COL_Q_SB:COL_Q_SB + SB_HEADS] = True
        is_q[COL_Q_NSA:COL_Q_NSA + NSA_HEADS] = True
        col_scale = jnp.asarray(np.repeat(np.where(is_q, HEAD_DIM ** -0.5, 1.0), HEAD_DIM), F32)
        w_qkv = (w[:, :QKV_COLS] * col_scale[None, :]).astype(BF16)
        proj = _norm_matmul(h, g_mix, w_qkv, BF16, tm=512, tn=512, name="in_proj_qkv")
        gates = _norm_matmul(h, g_mix, w_gates.astype(BF16), F32, tm=512, tn=GATE_PAD, name="in_proj_gates")
        merge = _norm_matmul(h, g_mix, merge_logit_w.astype(BF16), F32, tm=512, tn=512, name="in_proj_merge")
        proj = proj.reshape(b, s, QKV_COLS)

        sb = _sb_attention(proj, tq=256, heads=2)

        n_chunk = s // CMP_STRIDE
        kv = proj[:, :, COL_K_CMP * HEAD_DIM:COL_K_SLC * HEAD_DIM]
        kv = kv.reshape(b, n_chunk, CMP_STRIDE, 2, NSA_GROUPS, HEAD_DIM)
        kv = jnp.transpose(kv, (3, 0, 4, 1, 2, 5)).reshape(2, b * NSA_GROUPS, n_chunk, CMP_STRIDE * HEAD_DIM)
        half = CMP_STRIDE * HEAD_DIM
        pos = jnp.stack([cmp_pos_k[l], cmp_pos_v[l]]).reshape(2, 2, half)
        w1 = jnp.stack([cmp_k_w1[l], cmp_v_w1[l]]).astype(BF16).reshape(2, 2, half, HEAD_DIM)
        w2 = jnp.stack([cmp_k_w2[l], cmp_v_w2[l]]).astype(BF16)
        kvc = _compress(kv, pos, w1, w2)

        nsa = _nsa_attention(proj, kvc, *_transposed_values(proj, kvc), gates.reshape(b, s, GATE_PAD), tq=256)

        h = _merge_out(sb.reshape(m, -1), nsa.reshape(m, -1), merge, h,
                       w_branch_sb[l].astype(BF16), w_branch_nsa[l].astype(BF16), w_out[l].astype(BF16),
                       mix_post_g[l][None], tm=512, tn=512)

        h = _ffn(h, ffn2_pre_g[l][None], ffn2_w_in[l].astype(BF16), ffn2_w_out[l].astype(BF16),
                 ffn2_post_g[l][None], tm=512, tf=512)
    return h.reshape(b, s, d)
```

```python
import functools

import numpy as np
import jax
import jax.numpy as jnp
from jax import lax
from jax.experimental import pallas as pl
from jax.experimental.pallas import tpu as pltpu

HEAD_DIM = 128
SB_HEADS = 8
NSA_HEADS = 8
NSA_GROUPS = 2
NSA_HPG = NSA_HEADS // NSA_GROUPS
CMP_BLOCK = 32
CMP_STRIDE = 16
SLC_BLOCK = 64
N_SELECT = 16
WINDOW = 512
NORM_EPS = 1e-6
NEG_INF = -1e30
SCORE_SCALE = 2.0 ** 64
FORCED_SCORE = 1e30

QKV_COLS = (3 * SB_HEADS + NSA_HEADS + 6 * NSA_GROUPS) * HEAD_DIM
COL_Q_SB = 0
COL_K_SB = SB_HEADS
COL_V_SB = 2 * SB_HEADS
COL_Q_NSA = 3 * SB_HEADS
COL_K_CMP = COL_Q_NSA + NSA_HEADS
COL_K_SLC = COL_K_CMP + 2 * NSA_GROUPS
COL_V_SLC = COL_K_SLC + NSA_GROUPS
COL_K_WIN = COL_V_SLC + NSA_GROUPS
COL_V_WIN = COL_K_WIN + NSA_GROUPS
N_GATE_LOGITS = 3 * NSA_HEADS
GATE_PAD = 128

LANES = 128
V7X_VMEM_LIMIT = 56 * 1024 * 1024

F32 = jnp.float32
BF16 = jnp.bfloat16
NT_DIMS = (((1,), (1,)), ((), ()))


def _params(semantics):
    return pltpu.CompilerParams(dimension_semantics=semantics,
                                vmem_limit_bytes=V7X_VMEM_LIMIT)


def _rms(x, g):
    ms = jnp.mean(x * x, axis=-1, keepdims=True)
    return x * lax.rsqrt(ms + NORM_EPS) * g


def _ffn_body(x_ref, gpre_ref, wg_ref, wu_ref, wo_ref, gpost_ref, o_ref, h_ref, acc_ref):
    j = pl.program_id(1)

    @pl.when(j == 0)
    def _():
        h_ref[...] = _rms(x_ref[...], gpre_ref[...]).astype(BF16)
        acc_ref[...] = jnp.zeros_like(acc_ref)

    h = h_ref[...]
    gate = jnp.dot(h, wg_ref[...], preferred_element_type=F32)
    up = jnp.dot(h, wu_ref[...], preferred_element_type=F32)
    act = (gate * jax.nn.sigmoid(gate)) * up
    acc_ref[...] += jnp.dot(act.astype(BF16), wo_ref[...], preferred_element_type=F32)

    @pl.when(j == pl.num_programs(1) - 1)
    def _():
        o_ref[...] = x_ref[...] + 0.5 * _rms(acc_ref[...], gpost_ref[...])


def _ffn(x, g_pre, w_in, w_out, g_post, *, tm, tf):
    m, d = x.shape
    f = w_out.shape[0]
    nf = f // tf
    assert m % tm == 0 and f % tf == 0 and w_in.shape == (d, 2 * f)
    return pl.pallas_call(
        _ffn_body,
        name="ffn",
        grid=(m // tm, nf),
        in_specs=[
            pl.BlockSpec((tm, d), lambda i, j: (i, 0)),
            pl.BlockSpec((1, d), lambda i, j: (0, 0)),
            pl.BlockSpec((d, tf), lambda i, j: (0, j)),
            pl.BlockSpec((d, tf), lambda i, j: (0, j + nf)),
            pl.BlockSpec((tf, d), lambda i, j: (j, 0)),
            pl.BlockSpec((1, d), lambda i, j: (0, 0)),
        ],
        out_specs=pl.BlockSpec((tm, d), lambda i, j: (i, 0)),
        out_shape=jax.ShapeDtypeStruct((m, d), F32),
        scratch_shapes=[pltpu.VMEM((tm, d), BF16), pltpu.VMEM((tm, d), F32)],
        compiler_params=_params(("parallel", "arbitrary")),
    )(x, g_pre, w_in, w_in, w_out, g_post)


def _norm_matmul_body(x_ref, g_ref, w_ref, o_ref, h_ref):
    @pl.when(pl.program_id(1) == 0)
    def _():
        h_ref[...] = _rms(x_ref[...], g_ref[...]).astype(BF16)

    o_ref[...] = jnp.dot(h_ref[...], w_ref[...], preferred_element_type=F32).astype(o_ref.dtype)


MAX_COL_TILE = 1536


def _col_tile(n):
    assert n % LANES == 0
    blocks = n // LANES
    best = max(k for k in range(1, MAX_COL_TILE // LANES + 1) if blocks % k == 0)
    return best * LANES


def _norm_matmul(x, g, w, out_dtype, *, tm, tn, name):
    m, d = x.shape
    n = w.shape[1]
    assert m % tm == 0 and n % tn == 0
    return pl.pallas_call(
        _norm_matmul_body,
        name=name,
        grid=(m // tm, n // tn),
        in_specs=[
            pl.BlockSpec((tm, d), lambda i, j: (i, 0)),
            pl.BlockSpec((1, d), lambda i, j: (0, 0)),
            pl.BlockSpec((d, tn), lambda i, j: (0, j)),
        ],
        out_specs=pl.BlockSpec((tm, tn), lambda i, j: (i, j)),
        out_shape=jax.ShapeDtypeStruct((m, n), out_dtype),
        scratch_shapes=[pltpu.VMEM((tm, d), BF16)],
        compiler_params=_params(("parallel", "arbitrary")),
    )(x, g, w)


EXP_UNDERFLOW = 104.0


def _sb_body(q_ref, k_ref, v_ref, tri_ref, o_ref, *, tq, heads):
    i = pl.program_id(2)
    tri = tri_ref[...]
    row = lax.broadcasted_iota(jnp.int32, (tq, tq), 0)
    col = lax.broadcasted_iota(jnp.int32, (tq, tq), 1)
    past = col < row

    def tile(h, j, c, acc, diag):
        lanes = slice(h * HEAD_DIM, (h + 1) * HEAD_DIM)
        k0 = pl.multiple_of(j * tq, tq)
        k = k_ref[0, pl.ds(k0, tq), lanes]
        v = v_ref[0, pl.ds(k0, tq), lanes]
        z = lax.dot_general(q_ref[0, :, lanes], k, NT_DIMS, preferred_element_type=F32)
        softplus = jnp.maximum(z, 0.0) + jnp.log(1.0 + jnp.exp(-jnp.abs(z)))
        log_beta = z - softplus
        log_1m = -softplus
        if diag:
            log_1m = jnp.where(past, log_1m, 0.0)
        hi = log_1m.astype(BF16)
        lo = (log_1m - hi.astype(F32)).astype(BF16)
        suffix = (jnp.dot(hi, tri, preferred_element_type=F32)
                  + jnp.dot(lo, tri, preferred_element_type=F32))
        w = jnp.exp(log_beta + c + suffix)
        if diag:
            w = jnp.where(past, w, 0.0)
        acc = acc + jnp.dot(w.astype(BF16), v, preferred_element_type=F32)
        c = c + jnp.sum(log_1m, axis=-1, keepdims=True)
        return c, acc

    def any_alive(state):
        c_max = functools.reduce(jnp.maximum, [jnp.max(c) for c, _ in state])
        return (c_max >= -EXP_UNDERFLOW).astype(jnp.int32)

    state = tuple(tile(h, i, jnp.zeros((tq, 1), F32), jnp.zeros((tq, HEAD_DIM), F32), True)
                  for h in range(heads))

    def keep_going(carry):
        j, alive, _ = carry
        return jnp.logical_and(j >= 0, alive > 0)

    def step(carry):
        j, _, st = carry
        st = tuple(tile(h, j, st[h][0], st[h][1], False) for h in range(heads))
        return j - 1, any_alive(st), st

    _, _, state = lax.while_loop(keep_going, step, (i - 1, any_alive(state), state))
    for h in range(heads):
        o_ref[0, :, h * HEAD_DIM:(h + 1) * HEAD_DIM] = state[h][1].astype(o_ref.dtype)


def _sb_attention(proj, *, tq, heads):
    b, s, _ = proj.shape
    assert s % tq == 0 and SB_HEADS % heads == 0
    tri = jnp.asarray(np.tril(np.ones((tq, tq), np.float32), -1), BF16)
    width = heads * HEAD_DIM
    return pl.pallas_call(
        functools.partial(_sb_body, tq=tq, heads=heads),
        name="sb_attn",
        grid=(b, SB_HEADS // heads, s // tq),
        in_specs=[
            pl.BlockSpec((1, tq, width), lambda bb, h, i: (bb, i, COL_Q_SB // heads + h)),
            pl.BlockSpec((1, s, width), lambda bb, h, i: (bb, 0, COL_K_SB // heads + h)),
            pl.BlockSpec((1, s, width), lambda bb, h, i: (bb, 0, COL_V_SB // heads + h)),
            pl.BlockSpec((tq, tq), lambda bb, h, i: (0, 0)),
        ],
        out_specs=pl.BlockSpec((1, tq, width), lambda bb, h, i: (bb, i, h)),
        out_shape=jax.ShapeDtypeStruct((b, s, SB_HEADS * HEAD_DIM), BF16),
        compiler_params=_params(("parallel", "parallel", "arbitrary")),
    )(proj, proj, proj, tri)


def _compress_body(x_ref, pos_ref, w1_ref, w2_ref, o_ref):
    x = x_ref[0, 0].astype(F32)
    pos = pos_ref[0]
    xa = (x + pos[0:1]).astype(BF16)
    xb = (x + pos[1:2]).astype(BF16)
    first = jnp.dot(xa, w1_ref[0, 0], preferred_element_type=F32)
    second = jnp.dot(xb, w1_ref[0, 1], preferred_element_type=F32)
    n_chunk = x.shape[0]
    pre = first + pltpu.roll(second, n_chunk - 1, 0)
    y = jax.nn.gelu(pre).astype(BF16)
    o_ref[0, 0] = jnp.dot(y, w2_ref[0], preferred_element_type=F32).astype(o_ref.dtype)


def _compress(x, pos, w1, w2):
    _, bg, n_chunk, width = x.shape
    return pl.pallas_call(
        _compress_body,
        name="compress",
        grid=(2, bg),
        in_specs=[
            pl.BlockSpec((1, 1, n_chunk, width), lambda a, n: (a, n, 0, 0)),
            pl.BlockSpec((1, 2, width), lambda a, n: (a, 0, 0)),
            pl.BlockSpec((1, 2, width, HEAD_DIM), lambda a, n: (a, 0, 0, 0)),
            pl.BlockSpec((1, HEAD_DIM, HEAD_DIM), lambda a, n: (a, 0, 0)),
        ],
        out_specs=pl.BlockSpec((1, 1, n_chunk, HEAD_DIM), lambda a, n: (a, n, 0, 0)),
        out_shape=jax.ShapeDtypeStruct((2, bg, n_chunk, HEAD_DIM), BF16),
        compiler_params=_params(("parallel", "parallel")),
    )(x, pos, w1, w2)


POS_HI_COL = 0
POS_LO_COL = 1
BLOCK_COL0 = HEAD_DIM // 2


def _key_extra_columns(s, tk):
    pos = np.arange(s)
    extra = np.zeros((s, HEAD_DIM), np.float32)
    extra[:, POS_HI_COL] = (pos % tk) // SLC_BLOCK
    extra[:, POS_LO_COL] = pos % SLC_BLOCK
    extra[pos, BLOCK_COL0 + pos // SLC_BLOCK] = 1.0
    return jnp.asarray(extra, BF16)


def _nsa_body(q_ref, kc_ref, vct_ref, ks_ref, vst_ref, kw_ref, vwt_ref, gate_ref,
              pool_ref, kextra_ref, o_ref, ksa_ref, kwa_ref, m_ref, l_ref, acc_ref, *, tq, n_slc):
    g = pl.program_id(1)
    i = pl.program_id(2)
    q0 = i * tq
    n_cmp_pad = kc_ref.shape[2]

    @pl.when(i == 0)
    def _():
        ksa_ref[:, :HEAD_DIM] = ks_ref[0]
        ksa_ref[:, HEAD_DIM:] = kextra_ref[...]
        kwa_ref[:, :HEAD_DIM] = kw_ref[0]
        kwa_ref[:, HEAD_DIM:] = kextra_ref[...]

    slopes = [jnp.where(g == 0, 2.0 ** -(r + 1), 2.0 ** -(r + 1 + NSA_HPG)) for r in range(NSA_HPG)]
    q_heads = [q_ref[0, :, r * HEAD_DIM:(r + 1) * HEAD_DIM] for r in range(NSA_HPG)]
    t_row = q0 + lax.broadcasted_iota(jnp.int32, (1, tq), 1)

    wide = NSA_HPG * tq
    head_of_lane = lax.broadcasted_iota(jnp.int32, (1, wide), 1) // tq
    slope_row = jnp.zeros((1, wide), F32)
    for r in range(NSA_HPG):
        slope_row = jnp.where(head_of_lane == r, slopes[r], slope_row)

    def split_heads(x):
        return [x[:, r * tq:(r + 1) * tq] for r in range(NSA_HPG)]

    kc = kc_ref[0, 0]
    vct = vct_ref[0, 0]
    c_idx = lax.broadcasted_iota(jnp.int32, (n_cmp_pad, wide), 0)
    t_wide = q0 + lax.rem(lax.broadcasted_iota(jnp.int32, (1, wide), 1), tq)
    cmp_end = c_idx * CMP_STRIDE + (CMP_BLOCK - 1)
    valid_c = jnp.logical_and(t_wide >= cmp_end, c_idx < n_cmp_pad - 1)
    s = lax.dot_general(kc, jnp.concatenate(q_heads, axis=0), NT_DIMS, preferred_element_type=F32)
    s = jnp.where(valid_c, s + slope_row * (cmp_end - q0).astype(F32), NEG_INF)
    m = jnp.max(s, axis=0, keepdims=True)
    p = jnp.where(valid_c, jnp.exp(s - m), 0.0)
    l = jnp.sum(p, axis=0, keepdims=True)
    p = p * (1.0 / jnp.where(l > 0.0, l, 1.0))
    o_cmp = split_heads(jnp.dot(vct, p.astype(BF16), preferred_element_type=F32))
    p_grp = functools.reduce(lambda a, b: a + b, split_heads(p))

    pool = pool_ref[...]
    p_grp = p_grp * SCORE_SCALE
    p1 = p_grp.astype(BF16)
    r1 = p_grp - p1.astype(F32)
    p2 = r1.astype(BF16)
    p3 = (r1 - p2.astype(F32)).astype(BF16)
    score = (jnp.dot(pool, p1, preferred_element_type=F32)
             + jnp.dot(pool, p2, preferred_element_type=F32)
             + jnp.dot(pool, p3, preferred_element_type=F32))
    blk = lax.broadcasted_iota(jnp.int32, (n_slc, tq), 0)
    cur = t_row // SLC_BLOCK
    forced = jnp.logical_or(blk == 0, jnp.logical_or(blk == cur, blk == cur - 1))
    score = jnp.where(forced, FORCED_SCORE, score)
    score = jnp.where(blk <= cur, score, NEG_INF)

    sub = 8
    sub_i = lax.broadcasted_iota(jnp.int32, (sub, tq), 0)
    groups = [score[a:a + sub, :] for a in range(0, n_slc, sub)]
    ranks = [jnp.zeros((sub, tq), F32) for _ in groups]
    for ii in range(n_slc):
        row = score[ii:ii + 1, :]
        for gi, grp in enumerate(groups):
            ge = jnp.where(row >= grp, 1.0, 0.0)
            gt = jnp.where(row > grp, 1.0, 0.0)
            if ii < gi * sub:
                beats = ge
            elif ii >= (gi + 1) * sub:
                beats = gt
            else:
                beats = jnp.where(sub_i > ii - gi * sub, ge, gt)
            ranks[gi] = ranks[gi] + beats
    rank = jnp.concatenate(ranks, axis=0)
    sel_bias_t = jnp.where(rank < float(N_SELECT), 0.0, NEG_INF)

    pieces = [jnp.zeros((BLOCK_COL0, tq), F32), sel_bias_t]
    if BLOCK_COL0 + n_slc < HEAD_DIM:
        pieces.append(jnp.zeros((HEAD_DIM - BLOCK_COL0 - n_slc, tq), F32))
    sel_extra = jnp.concatenate(pieces, axis=0).T
    lane = lax.broadcasted_iota(jnp.int32, (tq, HEAD_DIM), 1)

    def query_slab(r, extra):
        pos_cols = jnp.where(lane == POS_HI_COL, slopes[r] * SLC_BLOCK,
                             jnp.where(lane == POS_LO_COL, slopes[r], extra))
        return jnp.concatenate([q_heads[r], pos_cols.astype(BF16)], axis=1)

    q_slc = jnp.concatenate([query_slab(r, sel_extra) for r in range(NSA_HPG)], axis=0)
    q_win = jnp.concatenate([query_slab(r, 0.0) for r in range(NSA_HPG)], axis=0)

    key_i = lax.broadcasted_iota(jnp.int32, (tq, wide), 0)
    qry_i = lax.rem(lax.broadcasted_iota(jnp.int32, (tq, wide), 1), tq)

    def scores(ka_ref, q_all, j):
        ka = ka_ref[pl.ds(pl.multiple_of(j * tq, tq), tq), :]
        return lax.dot_general(ka, q_all, NT_DIMS, preferred_element_type=F32)

    def reset():
        m_ref[...] = jnp.full((1, wide), NEG_INF, F32)
        l_ref[...] = jnp.zeros((1, wide), F32)
        acc_ref[...] = jnp.zeros((HEAD_DIM, wide), F32)

    def absorb(vt_ref, s, j, mask):
        k0 = pl.multiple_of(j * tq, tq)
        vt = vt_ref[0, :, pl.ds(k0, tq)]
        if mask is not None:
            s = jnp.where(mask, s, NEG_INF)
        off = slope_row * (k0 - q0).astype(F32)
        m = m_ref[...]
        m_new = jnp.maximum(m, jnp.max(s, axis=0, keepdims=True) + off)
        alpha = jnp.exp(m - m_new)
        p = jnp.exp(s - (m_new - off))
        m_ref[...] = m_new
        l_ref[...] = alpha * l_ref[...] + jnp.sum(p, axis=0, keepdims=True)
        acc_ref[...] = alpha * acc_ref[...] + jnp.dot(vt, p.astype(BF16), preferred_element_type=F32)

    def finish():
        return split_heads(acc_ref[...] * (1.0 / l_ref[...]))

    causal = key_i <= qry_i

    reset()
    absorb(vst_ref, scores(ksa_ref, q_slc, i), i, causal)

    @pl.loop(0, i)
    def _(j):
        absorb(vst_ref, scores(ksa_ref, q_slc, j), j, None)

    o_slc = finish()

    reset()
    win_tiles = [jnp.maximum(i - d, 0) for d in range((WINDOW - 1) // tq + 2)]
    win_scores = [scores(kwa_ref, q_win, j) for j in win_tiles]
    absorb(vwt_ref, win_scores[0], i, causal)
    for d in range(1, len(win_tiles)):
        dist = d * tq + qry_i - key_i
        in_window = jnp.logical_and(dist < WINDOW, i - d >= 0)
        absorb(vwt_ref, win_scores[d], win_tiles[d], in_window)
    o_win = finish()

    sg_t = jax.nn.sigmoid(gate_ref[0]).T

    def gate(branch, r):
        c0 = branch * NSA_HEADS + r
        c1 = c0 + NSA_HPG
        return jnp.where(g == 0, sg_t[c0:c0 + 1, :], sg_t[c1:c1 + 1, :])

    for r in range(NSA_HPG):
        out_t = gate(0, r) * o_cmp[r] + gate(1, r) * o_slc[r] + gate(2, r) * o_win[r]
        o_ref[0, :, r * HEAD_DIM:(r + 1) * HEAD_DIM] = out_t.T.astype(o_ref.dtype)


def _transposed_values(proj, kvc):
    b, s, _ = proj.shape
    grp_cols = NSA_GROUPS * HEAD_DIM
    vc_t = jnp.transpose(kvc[1].reshape(b, NSA_GROUPS, s // CMP_STRIDE, HEAD_DIM), (0, 1, 3, 2))
    vs_t = jnp.transpose(proj[:, :, COL_V_SLC * HEAD_DIM:COL_V_SLC * HEAD_DIM + grp_cols], (0, 2, 1))
    vw_t = jnp.transpose(proj[:, :, COL_V_WIN * HEAD_DIM:COL_V_WIN * HEAD_DIM + grp_cols], (0, 2, 1))
    return vc_t, vs_t, vw_t


def _nsa_attention(proj, kvc, vc_t, vs_t, vw_t, gm, *, tq):
    b, s, _ = proj.shape
    n_cmp_pad = s // CMP_STRIDE
    n_slc = s // SLC_BLOCK
    assert s % tq == 0 and tq % SLC_BLOCK == 0 and n_cmp_pad % LANES == 0
    assert BLOCK_COL0 + n_slc <= HEAD_DIM and tq // SLC_BLOCK <= 256 and n_slc % 8 == 0
    ratio = SLC_BLOCK // CMP_STRIDE
    span = CMP_BLOCK // CMP_STRIDE
    pool = np.zeros((n_slc, n_cmp_pad), np.float32)
    for jj in range(n_slc):
        for mm in range(ratio):
            for nn in range(span):
                c = ratio * jj + mm + nn
                if c < n_cmp_pad:
                    pool[jj, c] += 1.0
    grp_w = NSA_HPG * HEAD_DIM
    k_spec = lambda col: pl.BlockSpec((1, s, HEAD_DIM), lambda bb, g, i: (bb, 0, col + g))
    vt_spec = pl.BlockSpec((1, HEAD_DIM, s), lambda bb, g, i: (bb, g, 0))
    return pl.pallas_call(
        functools.partial(_nsa_body, tq=tq, n_slc=n_slc),
        name="nsa_attn",
        grid=(b, NSA_GROUPS, s // tq),
        in_specs=[
            pl.BlockSpec((1, tq, grp_w), lambda bb, g, i: (bb, i, COL_Q_NSA // NSA_HPG + g)),
            pl.BlockSpec((1, 1, n_cmp_pad, HEAD_DIM), lambda bb, g, i: (0, bb * NSA_GROUPS + g, 0, 0)),
            pl.BlockSpec((1, 1, HEAD_DIM, n_cmp_pad), lambda bb, g, i: (bb, g, 0, 0)),
            k_spec(COL_K_SLC), vt_spec, k_spec(COL_K_WIN), vt_spec,
            pl.BlockSpec((1, tq, GATE_PAD), lambda bb, g, i: (bb, i, gm.shape[2] // GATE_PAD - 1)),
            pl.BlockSpec((n_slc, n_cmp_pad), lambda bb, g, i: (0, 0)),
            pl.BlockSpec((s, HEAD_DIM), lambda bb, g, i: (0, 0)),
        ],
        out_specs=pl.BlockSpec((1, tq, grp_w), lambda bb, g, i: (bb, i, g)),
        out_shape=jax.ShapeDtypeStruct((b, s, NSA_HEADS * HEAD_DIM), BF16),
        scratch_shapes=[pltpu.VMEM((s, 2 * HEAD_DIM), BF16), pltpu.VMEM((s, 2 * HEAD_DIM), BF16),
                        pltpu.VMEM((1, NSA_HPG * tq), F32), pltpu.VMEM((1, NSA_HPG * tq), F32),
                        pltpu.VMEM((HEAD_DIM, NSA_HPG * tq), F32)],
        compiler_params=_params(("parallel", "parallel", "arbitrary")),
    )(proj, kvc, vc_t, proj, vs_t, proj, vw_t, gm, jnp.asarray(pool, BF16), _key_extra_columns(s, tq))


def _merge_body(sb_ref, nsa_ref, m0_ref, m1_ref, x_ref, wsb_ref, wnsa_ref, wout_ref, g_ref,
                o_ref, acc_ref):
    j = pl.program_id(1)

    @pl.when(j == 0)
    def _():
        acc_ref[...] = jnp.zeros_like(acc_ref)

    y_sb = jnp.dot(sb_ref[...], wsb_ref[...], preferred_element_type=F32)
    y_nsa = jnp.dot(nsa_ref[...], wnsa_ref[...], preferred_element_type=F32)
    merged = jax.nn.sigmoid(m0_ref[...]) * y_sb + jax.nn.sigmoid(m1_ref[...]) * y_nsa
    acc_ref[...] += jnp.dot(merged.astype(BF16), wout_ref[...], preferred_element_type=F32)

    @pl.when(j == pl.num_programs(1) - 1)
    def _():
        o_ref[...] = x_ref[...] + _rms(acc_ref[...], g_ref[...])


def _merge_out(sb, nsa, gm, x, w_sb, w_nsa, w_out, g_post, *, tm, tn):
    m, d = x.shape
    assert m % tm == 0 and d % tn == 0 and gm.shape[0] == m and gm.shape[1] >= 2 * d
    nj = d // tn
    return pl.pallas_call(
        _merge_body,
        name="merge_out",
        grid=(m // tm, nj),
        in_specs=[
            pl.BlockSpec((tm, sb.shape[1]), lambda i, j: (i, 0)),
            pl.BlockSpec((tm, nsa.shape[1]), lambda i, j: (i, 0)),
            pl.BlockSpec((tm, tn), lambda i, j: (i, j)),
            pl.BlockSpec((tm, tn), lambda i, j: (i, j + nj)),
            pl.BlockSpec((tm, d), lambda i, j: (i, 0)),
            pl.BlockSpec((sb.shape[1], tn), lambda i, j: (0, j)),
            pl.BlockSpec((nsa.shape[1], tn), lambda i, j: (0, j)),
            pl.BlockSpec((tn, d), lambda i, j: (j, 0)),
            pl.BlockSpec((1, d), lambda i, j: (0, 0)),
        ],
        out_specs=pl.BlockSpec((tm, d), lambda i, j: (i, 0)),
        out_shape=jax.ShapeDtypeStruct((m, d), F32),
        scratch_shapes=[pltpu.VMEM((tm, d), F32)],
        compiler_params=_params(("parallel", "arbitrary")),
    )(sb, nsa, gm, gm, x, w_sb, w_nsa, w_out, g_post)


def kernel(x, ffn1_pre_g, ffn1_w_in, ffn1_w_out, ffn1_post_g, mix_pre_g, w_in, cmp_pos_k, cmp_k_w1, cmp_k_w2, cmp_pos_v, cmp_v_w1, cmp_v_w2, w_branch_sb, w_branch_nsa, w_out, mix_post_g, ffn2_pre_g, ffn2_w_in, ffn2_w_out, ffn2_post_g):
    b, s, d = x.shape
    m = b * s
    depth = ffn1_pre_g.shape[0]
    h = x.reshape(m, d)
    for l in range(depth):
        h = _ffn(h, ffn1_pre_g[l][None], ffn1_w_in[l].astype(BF16), ffn1_w_out[l].astype(BF16),
                 ffn1_post_g[l][None], tm=512, tf=512)

        w = w_in[l]
        merge_logit_w = w[:, QKV_COLS + N_GATE_LOGITS:]
        w_gates = jnp.pad(w[:, QKV_COLS:QKV_COLS + N_GATE_LOGITS], ((0, 0), (0, GATE_PAD - N_GATE_LOGITS)))
        g_mix = mix_pre_g[l][None]
        is_q = np.zeros((QKV_COLS // HEAD_DIM,), bool)
        is_q[COL_Q_SB:COL_Q_SB + SB_HEADS] = True
        is_q[COL_Q_NSA:COL_Q_NSA + NSA_HEADS] = True
        col_scale = jnp.asarray(np.repeat(np.where(is_q, HEAD_DIM ** -0.5, 1.0), HEAD_DIM), F32)
        w_qkv = (w[:, :QKV_COLS] * col_scale[None, :]).astype(BF16)
        w_logits = jnp.concatenate([merge_logit_w, w_gates], axis=1).astype(BF16)
        proj = _norm_matmul(h, g_mix, w_qkv, BF16, tm=512, tn=_col_tile(QKV_COLS), name="in_proj_qkv")
        logits = _norm_matmul(h, g_mix, w_logits, F32, tm=512, tn=_col_tile(2 * d + GATE_PAD),
                              name="in_proj_logits")
        proj = proj.reshape(b, s, QKV_COLS)

        sb = _sb_attention(proj, tq=256, heads=2)

        n_chunk = s // CMP_STRIDE
        kv = proj[:, :, COL_K_CMP * HEAD_DIM:COL_K_SLC * HEAD_DIM]
        kv = kv.reshape(b, n_chunk, CMP_STRIDE, 2, NSA_GROUPS, HEAD_DIM)
        kv = jnp.transpose(kv, (3, 0, 4, 1, 2, 5)).reshape(2, b * NSA_GROUPS, n_chunk, CMP_STRIDE * HEAD_DIM)
        half = CMP_STRIDE * HEAD_DIM
        pos = jnp.stack([cmp_pos_k[l], cmp_pos_v[l]]).reshape(2, 2, half)
        w1 = jnp.stack([cmp_k_w1[l], cmp_v_w1[l]]).astype(BF16).reshape(2, 2, half, HEAD_DIM)
        w2 = jnp.stack([cmp_k_w2[l], cmp_v_w2[l]]).astype(BF16)
        kvc = _compress(kv, pos, w1, w2)

        nsa = _nsa_attention(proj, kvc, *_transposed_values(proj, kvc),
                             logits.reshape(b, s, 2 * d + GATE_PAD), tq=512)

        h = _merge_out(sb.reshape(m, -1), nsa.reshape(m, -1), logits, h,
                       w_branch_sb[l].astype(BF16), w_branch_nsa[l].astype(BF16), w_out[l].astype(BF16),
                       mix_post_g[l][None], tm=512, tn=1024)

        h = _ffn(h, ffn2_pre_g[l][None], ffn2_w_in[l].astype(BF16), ffn2_w_out[l].astype(BF16),
                 ffn2_post_g[l][None], tm=512, tf=512)
    return h.reshape(b, s, d)
```

```python
import functools

import numpy as np
import jax
import jax.numpy as jnp
from jax import lax
from jax.experimental import pallas as pl
from jax.experimental.pallas import tpu as pltpu

HEAD_DIM = 128
SB_HEADS = 8
NSA_HEADS = 8
NSA_GROUPS = 2
NSA_HPG = NSA_HEADS // NSA_GROUPS
CMP_BLOCK = 32
CMP_STRIDE = 16
SLC_BLOCK = 64
N_SELECT = 16
WINDOW = 512
NORM_EPS = 1e-6
NEG_INF = -1e30
SCORE_SCALE = 2.0 ** 64
FORCED_SCORE = 1e30

QKV_COLS = (3 * SB_HEADS + NSA_HEADS + 6 * NSA_GROUPS) * HEAD_DIM
COL_Q_SB = 0
COL_K_SB = SB_HEADS
COL_V_SB = 2 * SB_HEADS
COL_Q_NSA = 3 * SB_HEADS
COL_K_CMP = COL_Q_NSA + NSA_HEADS
COL_K_SLC = COL_K_CMP + 2 * NSA_GROUPS
COL_V_SLC = COL_K_SLC + NSA_GROUPS
COL_K_WIN = COL_V_SLC + NSA_GROUPS
COL_V_WIN = COL_K_WIN + NSA_GROUPS
N_GATE_LOGITS = 3 * NSA_HEADS
GATE_PAD = 128

LANES = 128
V7X_VMEM_LIMIT = 56 * 1024 * 1024

F32 = jnp.float32
BF16 = jnp.bfloat16
NT_DIMS = (((1,), (1,)), ((), ()))


def _params(semantics):
    return pltpu.CompilerParams(dimension_semantics=semantics,
                                vmem_limit_bytes=V7X_VMEM_LIMIT)


def _rms(x, g):
    ms = jnp.mean(x * x, axis=-1, keepdims=True)
    return x * lax.rsqrt(ms + NORM_EPS) * g


def _ffn_body(x_ref, gpre_ref, wg_ref, wu_ref, wo_ref, gpost_ref, o_ref, h_ref, acc_ref):
    j = pl.program_id(1)

    @pl.when(j == 0)
    def _():
        h_ref[...] = _rms(x_ref[...], gpre_ref[...]).astype(BF16)
        acc_ref[...] = jnp.zeros_like(acc_ref)

    h = h_ref[...]
    gate = jnp.dot(h, wg_ref[...], preferred_element_type=F32)
    up = jnp.dot(h, wu_ref[...], preferred_element_type=F32)
    act = (gate * jax.nn.sigmoid(gate)) * up
    acc_ref[...] += jnp.dot(act.astype(BF16), wo_ref[...], preferred_element_type=F32)

    @pl.when(j == pl.num_programs(1) - 1)
    def _():
        o_ref[...] = x_ref[...] + 0.5 * _rms(acc_ref[...], gpost_ref[...])


def _ffn(x, g_pre, w_in, w_out, g_post, *, tm, tf):
    m, d = x.shape
    f = w_out.shape[0]
    nf = f // tf
    assert m % tm == 0 and f % tf == 0 and w_in.shape == (d, 2 * f)
    return pl.pallas_call(
        _ffn_body,
        name="ffn",
        grid=(m // tm, nf),
        in_specs=[
            pl.BlockSpec((tm, d), lambda i, j: (i, 0)),
            pl.BlockSpec((1, d), lambda i, j: (0, 0)),
            pl.BlockSpec((d, tf), lambda i, j: (0, j)),
            pl.BlockSpec((d, tf), lambda i, j: (0, j + nf)),
            pl.BlockSpec((tf, d), lambda i, j: (j, 0)),
            pl.BlockSpec((1, d), lambda i, j: (0, 0)),
        ],
        out_specs=pl.BlockSpec((tm, d), lambda i, j: (i, 0)),
        out_shape=jax.ShapeDtypeStruct((m, d), F32),
        scratch_shapes=[pltpu.VMEM((tm, d), BF16), pltpu.VMEM((tm, d), F32)],
        compiler_params=_params(("parallel", "arbitrary")),
    )(x, g_pre, w_in, w_in, w_out, g_post)


def _norm_matmul_body(x_ref, g_ref, w_ref, o_ref, h_ref):
    @pl.when(pl.program_id(1) == 0)
    def _():
        h_ref[...] = _rms(x_ref[...], g_ref[...]).astype(BF16)

    o_ref[...] = jnp.dot(h_ref[...], w_ref[...], preferred_element_type=F32).astype(o_ref.dtype)


MAX_COL_TILE = 1536


def _col_tile(n):
    assert n % LANES == 0
    blocks = n // LANES
    best = max(k for k in range(1, MAX_COL_TILE // LANES + 1) if blocks % k == 0)
    return best * LANES


def _norm_matmul(x, g, w, out_dtype, *, tm, tn, name):
    m, d = x.shape
    n = w.shape[1]
    assert m % tm == 0 and n % tn == 0
    return pl.pallas_call(
        _norm_matmul_body,
        name=name,
        grid=(m // tm, n // tn),
        in_specs=[
            pl.BlockSpec((tm, d), lambda i, j: (i, 0)),
            pl.BlockSpec((1, d), lambda i, j: (0, 0)),
            pl.BlockSpec((d, tn), lambda i, j: (0, j)),
        ],
        out_specs=pl.BlockSpec((tm, tn), lambda i, j: (i, j)),
        out_shape=jax.ShapeDtypeStruct((m, n), out_dtype),
        scratch_shapes=[pltpu.VMEM((tm, d), BF16)],
        compiler_params=_params(("parallel", "arbitrary")),
    )(x, g, w)


EXP_UNDERFLOW = 104.0
TN_DIMS = (((0,), (0,)), ((), ()))
LOG2_E = 1.4426950408889634


def _sb_body(q_ref, k_ref, v_ref, tri_ref, o_ref, c_ref, acc_ref, *, tq, heads):
    i = pl.program_id(2)
    wide = heads * tq
    tri = tri_ref[...]
    key_i = lax.broadcasted_iota(jnp.int32, (tq, wide), 0)
    qry_i = lax.rem(lax.broadcasted_iota(jnp.int32, (tq, wide), 1), tq)
    past = key_i < qry_i
    head_lanes = [slice(h * HEAD_DIM, (h + 1) * HEAD_DIM) for h in range(heads)]

    def tile(j, diag):
        k0 = pl.multiple_of(j * tq, tq)
        z = jnp.concatenate(
            [lax.dot_general(k_ref[0, pl.ds(k0, tq), hl], q_ref[0, :, hl], NT_DIMS,
                             preferred_element_type=F32) for hl in head_lanes], axis=1)
        softplus = jnp.maximum(z, 0.0) + jnp.log(1.0 + jnp.exp2(jnp.abs(z) * (-LOG2_E)))
        log_beta = z - softplus
        if diag:
            softplus = jnp.where(past, softplus, 0.0)
        hi = softplus.astype(BF16)
        lo = (softplus - hi.astype(F32)).astype(BF16)
        later = jnp.dot(tri, jnp.concatenate([hi, lo], axis=0), preferred_element_type=F32)
        c = c_ref[...]
        w = jnp.exp(log_beta - c - later)
        if diag:
            w = jnp.where(past, w, 0.0)
        w = w.astype(BF16)
        for h, hl in enumerate(head_lanes):
            cols = slice(h * tq, (h + 1) * tq)
            acc_ref[:, cols] += lax.dot_general(v_ref[0, pl.ds(k0, tq), hl], w[:, cols], TN_DIMS,
                                                preferred_element_type=F32)
        c = c + jnp.sum(softplus, axis=0, keepdims=True)
        c_ref[...] = c
        return (jnp.min(c) <= EXP_UNDERFLOW).astype(jnp.int32)

    c_ref[...] = jnp.zeros_like(c_ref)
    acc_ref[...] = jnp.zeros_like(acc_ref)
    alive = tile(i, True)

    def keep_going(carry):
        j, alive = carry
        return jnp.logical_and(j >= 0, alive > 0)

    lax.while_loop(keep_going, lambda carry: (carry[0] - 1, tile(carry[0], False)), (i - 1, alive))
    for h, hl in enumerate(head_lanes):
        o_ref[0, :, hl] = acc_ref[:, h * tq:(h + 1) * tq].T.astype(o_ref.dtype)


def _sb_attention(proj, *, tq, heads):
    b, s, _ = proj.shape
    assert s % tq == 0 and SB_HEADS % heads == 0
    upper = np.triu(np.ones((tq, tq), np.float32), 1)
    tri = jnp.asarray(np.concatenate([upper, upper], axis=1), BF16)
    width = heads * HEAD_DIM
    return pl.pallas_call(
        functools.partial(_sb_body, tq=tq, heads=heads),
        name="sb_attn",
        grid=(b, SB_HEADS // heads, s // tq),
        in_specs=[
            pl.BlockSpec((1, tq, width), lambda bb, h, i: (bb, i, COL_Q_SB // heads + h)),
            pl.BlockSpec((1, s, width), lambda bb, h, i: (bb, 0, COL_K_SB // heads + h)),
            pl.BlockSpec((1, s, width), lambda bb, h, i: (bb, 0, COL_V_SB // heads + h)),
            pl.BlockSpec((tq, 2 * tq), lambda bb, h, i: (0, 0)),
        ],
        out_specs=pl.BlockSpec((1, tq, width), lambda bb, h, i: (bb, i, h)),
        out_shape=jax.ShapeDtypeStruct((b, s, SB_HEADS * HEAD_DIM), BF16),
        scratch_shapes=[pltpu.VMEM((1, heads * tq), F32), pltpu.VMEM((HEAD_DIM, heads * tq), F32)],
        compiler_params=_params(("parallel", "parallel", "arbitrary")),
    )(proj, proj, proj, tri)


def _compress_body(x_ref, pos_ref, w1_ref, w2_ref, o_ref):
    x = x_ref[0, 0].astype(F32)
    pos = pos_ref[0]
    xa = (x + pos[0:1]).astype(BF16)
    xb = (x + pos[1:2]).astype(BF16)
    first = jnp.dot(xa, w1_ref[0, 0], preferred_element_type=F32)
    second = jnp.dot(xb, w1_ref[0, 1], preferred_element_type=F32)
    n_chunk = x.shape[0]
    pre = first + pltpu.roll(second, n_chunk - 1, 0)
    y = jax.nn.gelu(pre).astype(BF16)
    o_ref[0, 0] = jnp.dot(y, w2_ref[0], preferred_element_type=F32).astype(o_ref.dtype)


def _compress(x, pos, w1, w2):
    _, bg, n_chunk, width = x.shape
    return pl.pallas_call(
        _compress_body,
        name="compress",
        grid=(2, bg),
        in_specs=[
            pl.BlockSpec((1, 1, n_chunk, width), lambda a, n: (a, n, 0, 0)),
            pl.BlockSpec((1, 2, width), lambda a, n: (a, 0, 0)),
            pl.BlockSpec((1, 2, width, HEAD_DIM), lambda a, n: (a, 0, 0, 0)),
            pl.BlockSpec((1, HEAD_DIM, HEAD_DIM), lambda a, n: (a, 0, 0)),
        ],
        out_specs=pl.BlockSpec((1, 1, n_chunk, HEAD_DIM), lambda a, n: (a, n, 0, 0)),
        out_shape=jax.ShapeDtypeStruct((2, bg, n_chunk, HEAD_DIM), BF16),
        compiler_params=_params(("parallel", "parallel")),
    )(x, pos, w1, w2)


POS_HI_COL = 0
POS_LO_COL = 1
BLOCK_COL0 = HEAD_DIM // 2


def _key_extra_columns(s, tk):
    pos = np.arange(s)
    extra = np.zeros((s, HEAD_DIM), np.float32)
    extra[:, POS_HI_COL] = (pos % tk) // SLC_BLOCK
    extra[:, POS_LO_COL] = pos % SLC_BLOCK
    extra[pos, BLOCK_COL0 + pos // SLC_BLOCK] = 1.0
    return jnp.asarray(extra, BF16)


def _nsa_body(q_ref, kc_ref, vct_ref, ks_ref, vst_ref, kw_ref, vwt_ref, gate_ref,
              pool_ref, kextra_ref, o_ref, ksa_ref, kwa_ref, m_ref, l_ref, acc_ref, *, tq, n_slc):
    g = pl.program_id(1)
    i = pl.program_id(2)
    q0 = i * tq
    n_cmp_pad = kc_ref.shape[2]

    @pl.when(i == 0)
    def _():
        ksa_ref[:, :HEAD_DIM] = ks_ref[0]
        ksa_ref[:, HEAD_DIM:] = kextra_ref[...]
        kwa_ref[:, :HEAD_DIM] = kw_ref[0]
        kwa_ref[:, HEAD_DIM:] = kextra_ref[...]

    slopes = [jnp.where(g == 0, 2.0 ** -(r + 1), 2.0 ** -(r + 1 + NSA_HPG)) for r in range(NSA_HPG)]
    q_heads = [q_ref[0, :, r * HEAD_DIM:(r + 1) * HEAD_DIM] for r in range(NSA_HPG)]
    t_row = q0 + lax.broadcasted_iota(jnp.int32, (1, tq), 1)

    wide = NSA_HPG * tq
    head_of_lane = lax.broadcasted_iota(jnp.int32, (1, wide), 1) // tq
    slope_row = jnp.zeros((1, wide), F32)
    for r in range(NSA_HPG):
        slope_row = jnp.where(head_of_lane == r, slopes[r], slope_row)

    def split_heads(x):
        return [x[:, r * tq:(r + 1) * tq] for r in range(NSA_HPG)]

    kc = kc_ref[0, 0]
    vct = vct_ref[0, 0]
    c_idx = lax.broadcasted_iota(jnp.int32, (n_cmp_pad, wide), 0)
    t_wide = q0 + lax.rem(lax.broadcasted_iota(jnp.int32, (1, wide), 1), tq)
    cmp_end = c_idx * CMP_STRIDE + (CMP_BLOCK - 1)
    valid_c = jnp.logical_and(t_wide >= cmp_end, c_idx < n_cmp_pad - 1)
    s = lax.dot_general(kc, jnp.concatenate(q_heads, axis=0), NT_DIMS, preferred_element_type=F32)
    s = jnp.where(valid_c, s + slope_row * (cmp_end - q0).astype(F32), NEG_INF)
    m = jnp.max(s, axis=0, keepdims=True)
    p = jnp.where(valid_c, jnp.exp(s - m), 0.0)
    l = jnp.sum(p, axis=0, keepdims=True)
    p = p * (1.0 / jnp.where(l > 0.0, l, 1.0))
    o_cmp = split_heads(jnp.dot(vct, p.astype(BF16), preferred_element_type=F32))
    p_grp = functools.reduce(lambda a, b: a + b, split_heads(p))

    pool = pool_ref[...]
    p_grp = p_grp * SCORE_SCALE
    p1 = p_grp.astype(BF16)
    r1 = p_grp - p1.astype(F32)
    p2 = r1.astype(BF16)
    p3 = (r1 - p2.astype(F32)).astype(BF16)
    score = (jnp.dot(pool, p1, preferred_element_type=F32)
             + jnp.dot(pool, p2, preferred_element_type=F32)
             + jnp.dot(pool, p3, preferred_element_type=F32))
    blk = lax.broadcasted_iota(jnp.int32, (n_slc, tq), 0)
    cur = t_row // SLC_BLOCK
    forced = jnp.logical_or(blk == 0, jnp.logical_or(blk == cur, blk == cur - 1))
    score = jnp.where(forced, FORCED_SCORE, score)
    score = jnp.where(blk <= cur, score, NEG_INF)

    sub = 8
    sub_i = lax.broadcasted_iota(jnp.int32, (sub, tq), 0)
    groups = [score[a:a + sub, :] for a in range(0, n_slc, sub)]
    ranks = [jnp.zeros((sub, tq), F32) for _ in groups]
    for ii in range(n_slc):
        row = score[ii:ii + 1, :]
        for gi, grp in enumerate(groups):
            ge = jnp.where(row >= grp, 1.0, 0.0)
            gt = jnp.where(row > grp, 1.0, 0.0)
            if ii < gi * sub:
                beats = ge
            elif ii >= (gi + 1) * sub:
                beats = gt
            else:
                beats = jnp.where(sub_i > ii - gi * sub, ge, gt)
            ranks[gi] = ranks[gi] + beats
    rank = jnp.concatenate(ranks, axis=0)
    sel_bias_t = jnp.where(rank < float(N_SELECT), 0.0, NEG_INF)

    pieces = [jnp.zeros((BLOCK_COL0, tq), F32), sel_bias_t]
    if BLOCK_COL0 + n_slc < HEAD_DIM:
        pieces.append(jnp.zeros((HEAD_DIM - BLOCK_COL0 - n_slc, tq), F32))
    sel_extra = jnp.concatenate(pieces, axis=0).T
    lane = lax.broadcasted_iota(jnp.int32, (tq, HEAD_DIM), 1)

    def query_slab(r, extra):
        pos_cols = jnp.where(lane == POS_HI_COL, slopes[r] * SLC_BLOCK,
                             jnp.where(lane == POS_LO_COL, slopes[r], extra))
        return jnp.concatenate([q_heads[r], pos_cols.astype(BF16)], axis=1)

    q_slc = jnp.concatenate([query_slab(r, sel_extra) for r in range(NSA_HPG)], axis=0)
    q_win = jnp.concatenate([query_slab(r, 0.0) for r in range(NSA_HPG)], axis=0)

    key_i = lax.broadcasted_iota(jnp.int32, (tq, wide), 0)
    qry_i = lax.rem(lax.broadcasted_iota(jnp.int32, (tq, wide), 1), tq)

    def scores(ka_ref, q_all, j):
        ka = ka_ref[pl.ds(pl.multiple_of(j * tq, tq), tq), :]
        return lax.dot_general(ka, q_all, NT_DIMS, preferred_element_type=F32)

    def reset():
        m_ref[...] = jnp.full((1, wide), NEG_INF, F32)
        l_ref[...] = jnp.zeros((1, wide), F32)
        acc_ref[...] = jnp.zeros((HEAD_DIM, wide), F32)

    def absorb(vt_ref, s, j, mask):
        k0 = pl.multiple_of(j * tq, tq)
        vt = vt_ref[0, :, pl.ds(k0, tq)]
        if mask is not None:
            s = jnp.where(mask, s, NEG_INF)
        off = slope_row * (k0 - q0).astype(F32)
        m = m_ref[...]
        m_new = jnp.maximum(m, jnp.max(s, axis=0, keepdims=True) + off)
        alpha = jnp.exp(m - m_new)
        p = jnp.exp(s - (m_new - off))
        m_ref[...] = m_new
        l_ref[...] = alpha * l_ref[...] + jnp.sum(p, axis=0, keepdims=True)
        acc_ref[...] = alpha * acc_ref[...] + jnp.dot(vt, p.astype(BF16), preferred_element_type=F32)

    def finish():
        return split_heads(acc_ref[...] * (1.0 / l_ref[...]))

    causal = key_i <= qry_i

    reset()
    absorb(vst_ref, scores(ksa_ref, q_slc, i), i, causal)

    @pl.loop(0, i)
    def _(j):
        absorb(vst_ref, scores(ksa_ref, q_slc, j), j, None)

    o_slc = finish()

    reset()
    win_tiles = [jnp.maximum(i - d, 0) for d in range((WINDOW - 1) // tq + 2)]
    win_scores = [scores(kwa_ref, q_win, j) for j in win_tiles]
    absorb(vwt_ref, win_scores[0], i, causal)
    for d in range(1, len(win_tiles)):
        dist = d * tq + qry_i - key_i
        in_window = jnp.logical_and(dist < WINDOW, i - d >= 0)
        absorb(vwt_ref, win_scores[d], win_tiles[d], in_window)
    o_win = finish()

    sg_t = jax.nn.sigmoid(gate_ref[0]).T

    def gate(branch, r):
        c0 = branch * NSA_HEADS + r
        c1 = c0 + NSA_HPG
        return jnp.where(g == 0, sg_t[c0:c0 + 1, :], sg_t[c1:c1 + 1, :])

    for r in range(NSA_HPG):
        out_t = gate(0, r) * o_cmp[r] + gate(1, r) * o_slc[r] + gate(2, r) * o_win[r]
        o_ref[0, :, r * HEAD_DIM:(r + 1) * HEAD_DIM] = out_t.T.astype(o_ref.dtype)


def _transposed_values(proj, kvc):
    b, s, _ = proj.shape
    grp_cols = NSA_GROUPS * HEAD_DIM
    vc_t = jnp.transpose(kvc[1].reshape(b, NSA_GROUPS, s // CMP_STRIDE, HEAD_DIM), (0, 1, 3, 2))
    vs_t = jnp.transpose(proj[:, :, COL_V_SLC * HEAD_DIM:COL_V_SLC * HEAD_DIM + grp_cols], (0, 2, 1))
    vw_t = jnp.transpose(proj[:, :, COL_V_WIN * HEAD_DIM:COL_V_WIN * HEAD_DIM + grp_cols], (0, 2, 1))
    return vc_t, vs_t, vw_t


def _nsa_attention(proj, kvc, vc_t, vs_t, vw_t, gm, *, tq):
    b, s, _ = proj.shape
    n_cmp_pad = s // CMP_STRIDE
    n_slc = s // SLC_BLOCK
    assert s % tq == 0 and tq % SLC_BLOCK == 0 and n_cmp_pad % LANES == 0
    assert BLOCK_COL0 + n_slc <= HEAD_DIM and tq // SLC_BLOCK <= 256 and n_slc % 8 == 0
    ratio = SLC_BLOCK // CMP_STRIDE
    span = CMP_BLOCK // CMP_STRIDE
    pool = np.zeros((n_slc, n_cmp_pad), np.float32)
    for jj in range(n_slc):
        for mm in range(ratio):
            for nn in range(span):
                c = ratio * jj + mm + nn
                if c < n_cmp_pad:
                    pool[jj, c] += 1.0
    grp_w = NSA_HPG * HEAD_DIM
    k_spec = lambda col: pl.BlockSpec((1, s, HEAD_DIM), lambda bb, g, i: (bb, 0, col + g))
    vt_spec = pl.BlockSpec((1, HEAD_DIM, s), lambda bb, g, i: (bb, g, 0))
    return pl.pallas_call(
        functools.partial(_nsa_body, tq=tq, n_slc=n_slc),
        name="nsa_attn",
        grid=(b, NSA_GROUPS, s // tq),
        in_specs=[
            pl.BlockSpec((1, tq, grp_w), lambda bb, g, i: (bb, i, COL_Q_NSA // NSA_HPG + g)),
            pl.BlockSpec((1, 1, n_cmp_pad, HEAD_DIM), lambda bb, g, i: (0, bb * NSA_GROUPS + g, 0, 0)),
            pl.BlockSpec((1, 1, HEAD_DIM, n_cmp_pad), lambda bb, g, i: (bb, g, 0, 0)),
            k_spec(COL_K_SLC), vt_spec, k_spec(COL_K_WIN), vt_spec,
            pl.BlockSpec((1, tq, GATE_PAD), lambda bb, g, i: (bb, i, gm.shape[2] // GATE_PAD - 1)),
            pl.BlockSpec((n_slc, n_cmp_pad), lambda bb, g, i: (0, 0)),
            pl.BlockSpec((s, HEAD_DIM), lambda bb, g, i: (0, 0)),
        ],
        out_specs=pl.BlockSpec((1, tq, grp_w), lambda bb, g, i: (bb, i, g)),
        out_shape=jax.ShapeDtypeStruct((b, s, NSA_HEADS * HEAD_DIM), BF16),
        scratch_shapes=[pltpu.VMEM((s, 2 * HEAD_DIM), BF16), pltpu.VMEM((s, 2 * HEAD_DIM), BF16),
                        pltpu.VMEM((1, NSA_HPG * tq), F32), pltpu.VMEM((1, NSA_HPG * tq), F32),
                        pltpu.VMEM((HEAD_DIM, NSA_HPG * tq), F32)],
        compiler_params=_params(("parallel", "parallel", "arbitrary")),
    )(proj, kvc, vc_t, proj, vs_t, proj, vw_t, gm, jnp.asarray(pool, BF16), _key_extra_columns(s, tq))


def _merge_body(sb_ref, nsa_ref, m0_ref, m1_ref, x_ref, wsb_ref, wnsa_ref, wout_ref, g_ref,
                o_ref, acc_ref):
    j = pl.program_id(1)

    @pl.when(j == 0)
    def _():
        acc_ref[...] = jnp.zeros_like(acc_ref)

    y_sb = jnp.dot(sb_ref[...], wsb_ref[...], preferred_element_type=F32)
    y_nsa = jnp.dot(nsa_ref[...], wnsa_ref[...], preferred_element_type=F32)
    merged = jax.nn.sigmoid(m0_ref[...]) * y_sb + jax.nn.sigmoid(m1_ref[...]) * y_nsa
    acc_ref[...] += jnp.dot(merged.astype(BF16), wout_ref[...], preferred_element_type=F32)

    @pl.when(j == pl.num_programs(1) - 1)
    def _():
        o_ref[...] = x_ref[...] + _rms(acc_ref[...], g_ref[...])


def _merge_out(sb, nsa, gm, x, w_sb, w_nsa, w_out, g_post, *, tm, tn):
    m, d = x.shape
    assert m % tm == 0 and d % tn == 0 and gm.shape[0] == m and gm.shape[1] >= 2 * d
    nj = d // tn
    return pl.pallas_call(
        _merge_body,
        name="merge_out",
        grid=(m // tm, nj),
        in_specs=[
            pl.BlockSpec((tm, sb.shape[1]), lambda i, j: (i, 0)),
            pl.BlockSpec((tm, nsa.shape[1]), lambda i, j: (i, 0)),
            pl.BlockSpec((tm, tn), lambda i, j: (i, j)),
            pl.BlockSpec((tm, tn), lambda i, j: (i, j + nj)),
            pl.BlockSpec((tm, d), lambda i, j: (i, 0)),
            pl.BlockSpec((sb.shape[1], tn), lambda i, j: (0, j)),
            pl.BlockSpec((nsa.shape[1], tn), lambda i, j: (0, j)),
            pl.BlockSpec((tn, d), lambda i, j: (j, 0)),
            pl.BlockSpec((1, d), lambda i, j: (0, 0)),
        ],
        out_specs=pl.BlockSpec((tm, d), lambda i, j: (i, 0)),
        out_shape=jax.ShapeDtypeStruct((m, d), F32),
        scratch_shapes=[pltpu.VMEM((tm, d), F32)],
        compiler_params=_params(("parallel", "arbitrary")),
    )(sb, nsa, gm, gm, x, w_sb, w_nsa, w_out, g_post)


def kernel(x, ffn1_pre_g, ffn1_w_in, ffn1_w_out, ffn1_post_g, mix_pre_g, w_in, cmp_pos_k, cmp_k_w1, cmp_k_w2, cmp_pos_v, cmp_v_w1, cmp_v_w2, w_branch_sb, w_branch_nsa, w_out, mix_post_g, ffn2_pre_g, ffn2_w_in, ffn2_w_out, ffn2_post_g):
    b, s, d = x.shape
    m = b * s
    depth = ffn1_pre_g.shape[0]
    h = x.reshape(m, d)
    for l in range(depth):
        h = _ffn(h, ffn1_pre_g[l][None], ffn1_w_in[l].astype(BF16), ffn1_w_out[l].astype(BF16),
                 ffn1_post_g[l][None], tm=512, tf=512)

        w = w_in[l]
        merge_logit_w = w[:, QKV_COLS + N_GATE_LOGITS:]
        w_gates = jnp.pad(w[:, QKV_COLS:QKV_COLS + N_GATE_LOGITS], ((0, 0), (0, GATE_PAD - N_GATE_LOGITS)))
        g_mix = mix_pre_g[l][None]
        is_q = np.zeros((QKV_COLS // HEAD_DIM,), bool)
        is_q[COL_Q_SB:COL_Q_SB + SB_HEADS] = True
        is_q[COL_Q_NSA:COL_Q_NSA + NSA_HEADS] = True
        col_scale = jnp.asarray(np.repeat(np.where(is_q, HEAD_DIM ** -0.5, 1.0), HEAD_DIM), F32)
        w_qkv = (w[:, :QKV_COLS] * col_scale[None, :]).astype(BF16)
        w_logits = jnp.concatenate([merge_logit_w, w_gates], axis=1).astype(BF16)
        proj = _norm_matmul(h, g_mix, w_qkv, BF16, tm=1024, tn=_col_tile(QKV_COLS), name="in_proj_qkv")
        logits = _norm_matmul(h, g_mix, w_logits, F32, tm=1024, tn=_col_tile(2 * d + GATE_PAD),
                              name="in_proj_logits")
        proj = proj.reshape(b, s, QKV_COLS)

        sb = _sb_attention(proj, tq=256, heads=4)

        n_chunk = s // CMP_STRIDE
        kv = proj[:, :, COL_K_CMP * HEAD_DIM:COL_K_SLC * HEAD_DIM]
        kv = kv.reshape(b, n_chunk, CMP_STRIDE, 2, NSA_GROUPS, HEAD_DIM)
        kv = jnp.transpose(kv, (3, 0, 4, 1, 2, 5)).reshape(2, b * NSA_GROUPS, n_chunk, CMP_STRIDE * HEAD_DIM)
        half = CMP_STRIDE * HEAD_DIM
        pos = jnp.stack([cmp_pos_k[l], cmp_pos_v[l]]).reshape(2, 2, half)
        w1 = jnp.stack([cmp_k_w1[l], cmp_v_w1[l]]).astype(BF16).reshape(2, 2, half, HEAD_DIM)
        w2 = jnp.stack([cmp_k_w2[l], cmp_v_w2[l]]).astype(BF16)
        kvc = _compress(kv, pos, w1, w2)

        nsa = _nsa_attention(proj, kvc, *_transposed_values(proj, kvc),
                             logits.reshape(b, s, 2 * d + GATE_PAD), tq=512)

        h = _merge_out(sb.reshape(m, -1), nsa.reshape(m, -1), logits, h,
                       w_branch_sb[l].astype(BF16), w_branch_nsa[l].astype(BF16), w_out[l].astype(BF16),
                       mix_post_g[l][None], tm=512, tn=1024)

        h = _ffn(h, ffn2_pre_g[l][None], ffn2_w_in[l].astype(BF16), ffn2_w_out[l].astype(BF16),
                 ffn2_post_g[l][None], tm=512, tf=512)
    return h.reshape(b, s, d)
```

```python
import functools

import numpy as np
import jax
import jax.numpy as jnp
from jax import lax
from jax.experimental import pallas as pl
from jax.experimental.pallas import tpu as pltpu

HEAD_DIM = 128
SB_HEADS = 8
NSA_HEADS = 8
NSA_GROUPS = 2
NSA_HPG = NSA_HEADS // NSA_GROUPS
CMP_BLOCK = 32
CMP_STRIDE = 16
SLC_BLOCK = 64
N_SELECT = 16
WINDOW = 512
NORM_EPS = 1e-6
NEG_INF = -1e30
SCORE_SCALE = 2.0 ** 64
FORCED_SCORE = 1e30

QKV_COLS = (3 * SB_HEADS + NSA_HEADS + 6 * NSA_GROUPS) * HEAD_DIM
COL_Q_SB = 0
COL_K_SB = SB_HEADS
COL_V_SB = 2 * SB_HEADS
COL_Q_NSA = 3 * SB_HEADS
COL_K_CMP = COL_Q_NSA + NSA_HEADS
COL_K_SLC = COL_K_CMP + 2 * NSA_GROUPS
COL_V_SLC = COL_K_SLC + NSA_GROUPS
COL_K_WIN = COL_V_SLC + NSA_GROUPS
COL_V_WIN = COL_K_WIN + NSA_GROUPS
N_GATE_LOGITS = 3 * NSA_HEADS
GATE_PAD = 128

LANES = 128
V7X_VMEM_LIMIT = 56 * 1024 * 1024

F32 = jnp.float32
BF16 = jnp.bfloat16
NT_DIMS = (((1,), (1,)), ((), ()))


def _params(semantics):
    return pltpu.CompilerParams(dimension_semantics=semantics,
                                vmem_limit_bytes=V7X_VMEM_LIMIT)


def _rms(x, g):
    ms = jnp.mean(x * x, axis=-1, keepdims=True)
    return x * lax.rsqrt(ms + NORM_EPS) * g


def _ffn_body(x_ref, gpre_ref, wg_ref, wu_ref, wo_ref, gpost_ref, o_ref, h_ref, acc_ref):
    j = pl.program_id(1)

    @pl.when(j == 0)
    def _():
        h_ref[...] = _rms(x_ref[...], gpre_ref[...]).astype(BF16)
        acc_ref[...] = jnp.zeros_like(acc_ref)

    h = h_ref[...]
    gate = jnp.dot(h, wg_ref[...], preferred_element_type=F32)
    up = jnp.dot(h, wu_ref[...], preferred_element_type=F32)
    act = (gate * jax.nn.sigmoid(gate)) * up
    acc_ref[...] += jnp.dot(act.astype(BF16), wo_ref[...], preferred_element_type=F32)

    @pl.when(j == pl.num_programs(1) - 1)
    def _():
        o_ref[...] = x_ref[...] + 0.5 * _rms(acc_ref[...], gpost_ref[...])


def _ffn(x, g_pre, w_in, w_out, g_post, *, tm, tf):
    m, d = x.shape
    f = w_out.shape[0]
    nf = f // tf
    assert m % tm == 0 and f % tf == 0 and w_in.shape == (d, 2 * f)
    return pl.pallas_call(
        _ffn_body,
        name="ffn",
        grid=(m // tm, nf),
        in_specs=[
            pl.BlockSpec((tm, d), lambda i, j: (i, 0)),
            pl.BlockSpec((1, d), lambda i, j: (0, 0)),
            pl.BlockSpec((d, tf), lambda i, j: (0, j)),
            pl.BlockSpec((d, tf), lambda i, j: (0, j + nf)),
            pl.BlockSpec((tf, d), lambda i, j: (j, 0)),
            pl.BlockSpec((1, d), lambda i, j: (0, 0)),
        ],
        out_specs=pl.BlockSpec((tm, d), lambda i, j: (i, 0)),
        out_shape=jax.ShapeDtypeStruct((m, d), F32),
        scratch_shapes=[pltpu.VMEM((tm, d), BF16), pltpu.VMEM((tm, d), F32)],
        compiler_params=_params(("parallel", "arbitrary")),
    )(x, g_pre, w_in, w_in, w_out, g_post)


def _norm_matmul_body(x_ref, g_ref, w_ref, o_ref, h_ref):
    @pl.when(pl.program_id(1) == 0)
    def _():
        h_ref[...] = _rms(x_ref[...], g_ref[...]).astype(BF16)

    o_ref[...] = jnp.dot(h_ref[...], w_ref[...], preferred_element_type=F32).astype(o_ref.dtype)


MAX_COL_TILE = 1536


def _col_tile(n):
    assert n % LANES == 0
    blocks = n // LANES
    best = max(k for k in range(1, MAX_COL_TILE // LANES + 1) if blocks % k == 0)
    return best * LANES


def _norm_matmul(x, g, w, out_dtype, *, tm, tn, n_out, name):
    m, d = x.shape
    n = n_out
    assert m % tm == 0 and n % tn == 0 and n <= w.shape[1]
    return pl.pallas_call(
        _norm_matmul_body,
        name=name,
        grid=(m // tm, n // tn),
        in_specs=[
            pl.BlockSpec((tm, d), lambda i, j: (i, 0)),
            pl.BlockSpec((1, d), lambda i, j: (0, 0)),
            pl.BlockSpec((d, tn), lambda i, j: (0, j)),
        ],
        out_specs=pl.BlockSpec((tm, tn), lambda i, j: (i, j)),
        out_shape=jax.ShapeDtypeStruct((m, n), out_dtype),
        scratch_shapes=[pltpu.VMEM((tm, d), BF16)],
        compiler_params=_params(("parallel", "arbitrary")),
    )(x, g, w)


EXP_UNDERFLOW = 104.0
TN_DIMS = (((0,), (0,)), ((), ()))
LOG2_E = 1.4426950408889634


def _sb_body(q_ref, k_ref, v_ref, tri_ref, o_ref, c_ref, acc_ref, *, tq, heads):
    i = pl.program_id(2)
    wide = heads * tq
    tri = tri_ref[...]
    key_i = lax.broadcasted_iota(jnp.int32, (tq, wide), 0)
    qry_i = lax.rem(lax.broadcasted_iota(jnp.int32, (tq, wide), 1), tq)
    past = key_i < qry_i
    head_lanes = [slice(h * HEAD_DIM, (h + 1) * HEAD_DIM) for h in range(heads)]

    def tile(j, diag):
        k0 = pl.multiple_of(j * tq, tq)
        z = jnp.concatenate(
            [lax.dot_general(k_ref[0, pl.ds(k0, tq), hl], q_ref[0, :, hl], NT_DIMS,
                             preferred_element_type=F32) for hl in head_lanes], axis=1)
        softplus = jnp.maximum(z, 0.0) + jnp.log(1.0 + jnp.exp2(jnp.abs(z) * (-LOG2_E)))
        log_beta = z - softplus
        if diag:
            softplus = jnp.where(past, softplus, 0.0)
        hi = softplus.astype(BF16)
        lo = (softplus - hi.astype(F32)).astype(BF16)
        later = jnp.dot(tri, jnp.concatenate([hi, lo], axis=0), preferred_element_type=F32)
        c = c_ref[...]
        w = jnp.exp(log_beta - c - later)
        if diag:
            w = jnp.where(past, w, 0.0)
        w = w.astype(BF16)
        for h, hl in enumerate(head_lanes):
            cols = slice(h * tq, (h + 1) * tq)
            acc_ref[:, cols] += lax.dot_general(v_ref[0, pl.ds(k0, tq), hl], w[:, cols], TN_DIMS,
                                                preferred_element_type=F32)
        c = c + jnp.sum(softplus, axis=0, keepdims=True)
        c_ref[...] = c
        return (jnp.min(c) <= EXP_UNDERFLOW).astype(jnp.int32)

    c_ref[...] = jnp.zeros_like(c_ref)
    acc_ref[...] = jnp.zeros_like(acc_ref)
    alive = tile(i, True)

    def keep_going(carry):
        j, alive = carry
        return jnp.logical_and(j >= 0, alive > 0)

    lax.while_loop(keep_going, lambda carry: (carry[0] - 1, tile(carry[0], False)), (i - 1, alive))
    for h, hl in enumerate(head_lanes):
        o_ref[0, :, hl] = acc_ref[:, h * tq:(h + 1) * tq].T.astype(o_ref.dtype)


def _sb_attention(proj, *, tq, heads):
    b, s, _ = proj.shape
    assert s % tq == 0 and SB_HEADS % heads == 0
    upper = np.triu(np.ones((tq, tq), np.float32), 1)
    tri = jnp.asarray(np.concatenate([upper, upper], axis=1), BF16)
    width = heads * HEAD_DIM
    return pl.pallas_call(
        functools.partial(_sb_body, tq=tq, heads=heads),
        name="sb_attn",
        grid=(b, SB_HEADS // heads, s // tq),
        in_specs=[
            pl.BlockSpec((1, tq, width), lambda bb, h, i: (bb, i, COL_Q_SB // heads + h)),
            pl.BlockSpec((1, s, width), lambda bb, h, i: (bb, 0, COL_K_SB // heads + h)),
            pl.BlockSpec((1, s, width), lambda bb, h, i: (bb, 0, COL_V_SB // heads + h)),
            pl.BlockSpec((tq, 2 * tq), lambda bb, h, i: (0, 0)),
        ],
        out_specs=pl.BlockSpec((1, tq, width), lambda bb, h, i: (bb, i, h)),
        out_shape=jax.ShapeDtypeStruct((b, s, SB_HEADS * HEAD_DIM), BF16),
        scratch_shapes=[pltpu.VMEM((1, heads * tq), F32), pltpu.VMEM((HEAD_DIM, heads * tq), F32)],
        compiler_params=_params(("parallel", "parallel", "arbitrary")),
    )(proj, proj, proj, tri)


def _compress_body(x_ref, pos_ref, w1_ref, w2_ref, o_ref):
    x = x_ref[0, 0].astype(F32)
    pos = pos_ref[0]
    xa = (x + pos[0:1]).astype(BF16)
    xb = (x + pos[1:2]).astype(BF16)
    first = jnp.dot(xa, w1_ref[0, 0], preferred_element_type=F32)
    second = jnp.dot(xb, w1_ref[0, 1], preferred_element_type=F32)
    n_chunk = x.shape[0]
    pre = first + pltpu.roll(second, n_chunk - 1, 0)
    y = jax.nn.gelu(pre).astype(BF16)
    o_ref[0, 0] = jnp.dot(y, w2_ref[0], preferred_element_type=F32).astype(o_ref.dtype)


def _compress(x, pos, w1, w2):
    _, bg, n_chunk, width = x.shape
    return pl.pallas_call(
        _compress_body,
        name="compress",
        grid=(2, bg),
        in_specs=[
            pl.BlockSpec((1, 1, n_chunk, width), lambda a, n: (a, n, 0, 0)),
            pl.BlockSpec((1, 2, width), lambda a, n: (a, 0, 0)),
            pl.BlockSpec((1, 2, width, HEAD_DIM), lambda a, n: (a, 0, 0, 0)),
            pl.BlockSpec((1, HEAD_DIM, HEAD_DIM), lambda a, n: (a, 0, 0)),
        ],
        out_specs=pl.BlockSpec((1, 1, n_chunk, HEAD_DIM), lambda a, n: (a, n, 0, 0)),
        out_shape=jax.ShapeDtypeStruct((2, bg, n_chunk, HEAD_DIM), BF16),
        compiler_params=_params(("parallel", "parallel")),
    )(x, pos, w1, w2)


POS_HI_COL = 0
POS_LO_COL = 1
BLOCK_COL0 = HEAD_DIM // 2
SLAB_PARTS = 4


def _key_extra_columns(s, tk):
    pos = np.arange(s)
    extra = np.zeros((s, HEAD_DIM), np.float32)
    extra[:, POS_HI_COL] = (pos % tk) // SLC_BLOCK
    extra[:, POS_LO_COL] = pos % SLC_BLOCK
    extra[pos, BLOCK_COL0 + pos // SLC_BLOCK] = 1.0
    return jnp.asarray(extra, BF16)


def _nsa_body(q_ref, kc_ref, vc_ref, ks_ref, vs_ref, kw_ref, vw_ref, gate_ref,
              pool_ref, kextra_ref, o_ref, ksa_ref, kwa_ref, m_ref, l_ref, acc_ref, *, tq, n_slc):
    g = pl.program_id(1)
    i = pl.program_id(2)
    q0 = i * tq
    n_cmp_pad = kc_ref.shape[2]

    @pl.when(i == 0)
    def _():
        ksa_ref[:, :HEAD_DIM] = ks_ref[0]
        ksa_ref[:, HEAD_DIM:] = kextra_ref[...]
        kwa_ref[:, :HEAD_DIM] = kw_ref[0]
        kwa_ref[:, HEAD_DIM:] = kextra_ref[...]

    slopes = [jnp.where(g == 0, 2.0 ** -(r + 1), 2.0 ** -(r + 1 + NSA_HPG)) for r in range(NSA_HPG)]
    q_heads = [q_ref[0, :, r * HEAD_DIM:(r + 1) * HEAD_DIM] for r in range(NSA_HPG)]
    t_row = q0 + lax.broadcasted_iota(jnp.int32, (1, tq), 1)

    wide = NSA_HPG * tq
    head_of_lane = lax.broadcasted_iota(jnp.int32, (1, wide), 1) // tq
    slope_row = jnp.zeros((1, wide), F32)
    for r in range(NSA_HPG):
        slope_row = jnp.where(head_of_lane == r, slopes[r], slope_row)

    def split_heads(x):
        return [x[:, r * tq:(r + 1) * tq] for r in range(NSA_HPG)]

    kc = kc_ref[0, 0]
    vc = vc_ref[0, 0]
    c_idx = lax.broadcasted_iota(jnp.int32, (n_cmp_pad, wide), 0)
    t_wide = q0 + lax.rem(lax.broadcasted_iota(jnp.int32, (1, wide), 1), tq)
    cmp_end = c_idx * CMP_STRIDE + (CMP_BLOCK - 1)
    valid_c = jnp.logical_and(t_wide >= cmp_end, c_idx < n_cmp_pad - 1)
    s = lax.dot_general(kc, jnp.concatenate(q_heads, axis=0), NT_DIMS, preferred_element_type=F32)
    s = jnp.where(valid_c, s + slope_row * (cmp_end - q0).astype(F32), NEG_INF)
    m = jnp.max(s, axis=0, keepdims=True)
    p = jnp.where(valid_c, jnp.exp(s - m), 0.0)
    l = jnp.sum(p, axis=0, keepdims=True)
    p = p * (1.0 / jnp.where(l > 0.0, l, 1.0))
    o_cmp = split_heads(lax.dot_general(vc, p.astype(BF16), TN_DIMS,
                                        preferred_element_type=F32))
    p_grp = functools.reduce(lambda a, b: a + b, split_heads(p))

    pool = pool_ref[...]
    p_grp = p_grp * SCORE_SCALE
    p1 = p_grp.astype(BF16)
    r1 = p_grp - p1.astype(F32)
    p2 = r1.astype(BF16)
    p3 = (r1 - p2.astype(F32)).astype(BF16)
    score = (jnp.dot(pool, p1, preferred_element_type=F32)
             + jnp.dot(pool, p2, preferred_element_type=F32)
             + jnp.dot(pool, p3, preferred_element_type=F32))
    blk = lax.broadcasted_iota(jnp.int32, (n_slc, tq), 0)
    cur = t_row // SLC_BLOCK
    forced = jnp.logical_or(blk == 0, jnp.logical_or(blk == cur, blk == cur - 1))
    score = jnp.where(forced, FORCED_SCORE, score)
    score = jnp.where(blk <= cur, score, NEG_INF)

    sub = 8
    sub_i = lax.broadcasted_iota(jnp.int32, (sub, tq), 0)
    groups = [score[a:a + sub, :] for a in range(0, n_slc, sub)]
    ranks = [jnp.zeros((sub, tq), F32) for _ in groups]
    for ii in range(n_slc):
        row = score[ii:ii + 1, :]
        for gi, grp in enumerate(groups):
            ge = jnp.where(row >= grp, 1.0, 0.0)
            gt = jnp.where(row > grp, 1.0, 0.0)
            if ii < gi * sub:
                beats = ge
            elif ii >= (gi + 1) * sub:
                beats = gt
            else:
                beats = jnp.where(sub_i > ii - gi * sub, ge, gt)
            ranks[gi] = ranks[gi] + beats
    rank = jnp.concatenate(ranks, axis=0)
    sel_bias_t = jnp.where(rank < float(N_SELECT), 0.0, NEG_INF)

    pieces = [jnp.zeros((BLOCK_COL0, tq), F32), sel_bias_t]
    if BLOCK_COL0 + n_slc < HEAD_DIM:
        pieces.append(jnp.zeros((HEAD_DIM - BLOCK_COL0 - n_slc, tq), F32))
    sel_extra = jnp.concatenate(pieces, axis=0).T
    lane = lax.broadcasted_iota(jnp.int32, (tq, HEAD_DIM), 1)

    def query_slab(r, extra):
        pos_cols = jnp.where(lane == POS_HI_COL, slopes[r] * SLC_BLOCK,
                             jnp.where(lane == POS_LO_COL, slopes[r], extra))
        return jnp.concatenate([q_heads[r], pos_cols.astype(BF16)], axis=1)

    q_slc = jnp.concatenate([query_slab(r, sel_extra) for r in range(NSA_HPG)], axis=0)
    q_win = jnp.concatenate([query_slab(r, 0.0) for r in range(NSA_HPG)], axis=0)

    part_w = wide // SLAB_PARTS
    parts = [slice(a * part_w, (a + 1) * part_w) for a in range(SLAB_PARTS)]
    key_i = lax.broadcasted_iota(jnp.int32, (tq, part_w), 0)
    qry_i = lax.rem(lax.broadcasted_iota(jnp.int32, (tq, part_w), 1), tq)

    def scores(ka_ref, q_all, j):
        ka = ka_ref[pl.ds(pl.multiple_of(j * tq, tq), tq), :]
        return [lax.dot_general(ka, q_all[a * part_w:(a + 1) * part_w, :], NT_DIMS,
                                preferred_element_type=F32) for a in range(SLAB_PARTS)]

    def reset():
        m_ref[...] = jnp.full((1, wide), NEG_INF, F32)
        l_ref[...] = jnp.zeros((1, wide), F32)
        acc_ref[...] = jnp.zeros((HEAD_DIM, wide), F32)

    def absorb(v_ref, s_parts, j, mask):
        k0 = pl.multiple_of(j * tq, tq)
        v = v_ref[0, pl.ds(k0, tq), :]
        tile_off = (k0 - q0).astype(F32)
        for cols, s in zip(parts, s_parts):
            if mask is not None:
                s = jnp.where(mask, s, NEG_INF)
            off = slope_row[:, cols] * tile_off
            m = m_ref[:, cols]
            m_new = jnp.maximum(m, jnp.max(s, axis=0, keepdims=True) + off)
            alpha = jnp.exp(m - m_new)
            p = jnp.exp(s - (m_new - off))
            m_ref[:, cols] = m_new
            l_ref[:, cols] = alpha * l_ref[:, cols] + jnp.sum(p, axis=0, keepdims=True)
            acc_ref[:, cols] = alpha * acc_ref[:, cols] + lax.dot_general(
                v, p.astype(BF16), TN_DIMS, preferred_element_type=F32)

    def finish():
        return split_heads(acc_ref[...] * (1.0 / l_ref[...]))

    causal = key_i <= qry_i

    reset()
    absorb(vs_ref, scores(ksa_ref, q_slc, i), i, causal)

    @pl.loop(0, i)
    def _(j):
        absorb(vs_ref, scores(ksa_ref, q_slc, j), j, None)

    o_slc = finish()

    reset()
    win_tiles = [jnp.maximum(i - d, 0) for d in range((WINDOW - 1) // tq + 2)]
    win_scores = [scores(kwa_ref, q_win, j) for j in win_tiles]
    absorb(vw_ref, win_scores[0], i, causal)
    for d in range(1, len(win_tiles)):
        dist = d * tq + qry_i - key_i
        in_window = jnp.logical_and(dist < WINDOW, i - d >= 0)
        absorb(vw_ref, win_scores[d], win_tiles[d], in_window)
    o_win = finish()

    sg_t = jax.nn.sigmoid(gate_ref[0]).T

    def gate(branch, r):
        c0 = branch * NSA_HEADS + r
        c1 = c0 + NSA_HPG
        return jnp.where(g == 0, sg_t[c0:c0 + 1, :], sg_t[c1:c1 + 1, :])

    for r in range(NSA_HPG):
        out_t = gate(0, r) * o_cmp[r] + gate(1, r) * o_slc[r] + gate(2, r) * o_win[r]
        o_ref[0, :, r * HEAD_DIM:(r + 1) * HEAD_DIM] = out_t.T.astype(o_ref.dtype)


def _nsa_attention(proj, kvc, gm, *, tq):
    b, s, _ = proj.shape
    n_cmp_pad = s // CMP_STRIDE
    n_slc = s // SLC_BLOCK
    assert s % tq == 0 and tq % SLC_BLOCK == 0 and n_cmp_pad % LANES == 0
    assert BLOCK_COL0 + n_slc <= HEAD_DIM and tq // SLC_BLOCK <= 256 and n_slc % 8 == 0
    ratio = SLC_BLOCK // CMP_STRIDE
    span = CMP_BLOCK // CMP_STRIDE
    pool = np.zeros((n_slc, n_cmp_pad), np.float32)
    for jj in range(n_slc):
        for mm in range(ratio):
            for nn in range(span):
                c = ratio * jj + mm + nn
                if c < n_cmp_pad:
                    pool[jj, c] += 1.0
    grp_w = NSA_HPG * HEAD_DIM
    kv_spec = lambda col: pl.BlockSpec((1, s, HEAD_DIM), lambda bb, g, i: (bb, 0, col + g))
    cmp_spec = lambda a: pl.BlockSpec((1, 1, n_cmp_pad, HEAD_DIM),
                                      lambda bb, g, i: (a, bb * NSA_GROUPS + g, 0, 0))
    return pl.pallas_call(
        functools.partial(_nsa_body, tq=tq, n_slc=n_slc),
        name="nsa_attn",
        grid=(b, NSA_GROUPS, s // tq),
        in_specs=[
            pl.BlockSpec((1, tq, grp_w), lambda bb, g, i: (bb, i, COL_Q_NSA // NSA_HPG + g)),
            cmp_spec(0), cmp_spec(1),
            kv_spec(COL_K_SLC), kv_spec(COL_V_SLC), kv_spec(COL_K_WIN), kv_spec(COL_V_WIN),
            pl.BlockSpec((1, tq, GATE_PAD), lambda bb, g, i: (bb, i, gm.shape[2] // GATE_PAD - 1)),
            pl.BlockSpec((n_slc, n_cmp_pad), lambda bb, g, i: (0, 0)),
            pl.BlockSpec((s, HEAD_DIM), lambda bb, g, i: (0, 0)),
        ],
        out_specs=pl.BlockSpec((1, tq, grp_w), lambda bb, g, i: (bb, i, g)),
        out_shape=jax.ShapeDtypeStruct((b, s, NSA_HEADS * HEAD_DIM), BF16),
        scratch_shapes=[pltpu.VMEM((s, 2 * HEAD_DIM), BF16), pltpu.VMEM((s, 2 * HEAD_DIM), BF16),
                        pltpu.VMEM((1, NSA_HPG * tq), F32), pltpu.VMEM((1, NSA_HPG * tq), F32),
                        pltpu.VMEM((HEAD_DIM, NSA_HPG * tq), F32)],
        compiler_params=_params(("parallel", "parallel", "arbitrary")),
    )(proj, kvc, kvc, proj, proj, proj, proj, gm, jnp.asarray(pool, BF16), _key_extra_columns(s, tq))


def _merge_body(sb_ref, nsa_ref, m0_ref, m1_ref, x_ref, wsb_ref, wnsa_ref, wout_ref, g_ref,
                o_ref, acc_ref):
    j = pl.program_id(1)

    @pl.when(j == 0)
    def _():
        acc_ref[...] = jnp.zeros_like(acc_ref)

    y_sb = jnp.dot(sb_ref[...], wsb_ref[...], preferred_element_type=F32)
    y_nsa = jnp.dot(nsa_ref[...], wnsa_ref[...], preferred_element_type=F32)
    merged = jax.nn.sigmoid(m0_ref[...]) * y_sb + jax.nn.sigmoid(m1_ref[...]) * y_nsa
    acc_ref[...] += jnp.dot(merged.astype(BF16), wout_ref[...], preferred_element_type=F32)

    @pl.when(j == pl.num_programs(1) - 1)
    def _():
        o_ref[...] = x_ref[...] + _rms(acc_ref[...], g_ref[...])


def _merge_out(sb, nsa, gm, x, w_sb, w_nsa, w_out, g_post, *, tm, tn):
    m, d = x.shape
    assert m % tm == 0 and d % tn == 0 and gm.shape[0] == m and gm.shape[1] >= 2 * d
    nj = d // tn
    return pl.pallas_call(
        _merge_body,
        name="merge_out",
        grid=(m // tm, nj),
        in_specs=[
            pl.BlockSpec((tm, sb.shape[1]), lambda i, j: (i, 0)),
            pl.BlockSpec((tm, nsa.shape[1]), lambda i, j: (i, 0)),
            pl.BlockSpec((tm, tn), lambda i, j: (i, j)),
            pl.BlockSpec((tm, tn), lambda i, j: (i, j + nj)),
            pl.BlockSpec((tm, d), lambda i, j: (i, 0)),
            pl.BlockSpec((sb.shape[1], tn), lambda i, j: (0, j)),
            pl.BlockSpec((nsa.shape[1], tn), lambda i, j: (0, j)),
            pl.BlockSpec((tn, d), lambda i, j: (j, 0)),
            pl.BlockSpec((1, d), lambda i, j: (0, 0)),
        ],
        out_specs=pl.BlockSpec((tm, d), lambda i, j: (i, 0)),
        out_shape=jax.ShapeDtypeStruct((m, d), F32),
        scratch_shapes=[pltpu.VMEM((tm, d), F32)],
        compiler_params=_params(("parallel", "arbitrary")),
    )(sb, nsa, gm, gm, x, w_sb, w_nsa, w_out, g_post)


def kernel(x, ffn1_pre_g, ffn1_w_in, ffn1_w_out, ffn1_post_g, mix_pre_g, w_in, cmp_pos_k, cmp_k_w1, cmp_k_w2, cmp_pos_v, cmp_v_w1, cmp_v_w2, w_branch_sb, w_branch_nsa, w_out, mix_post_g, ffn2_pre_g, ffn2_w_in, ffn2_w_out, ffn2_post_g):
    b, s, d = x.shape
    m = b * s
    depth = ffn1_pre_g.shape[0]
    h = x.reshape(m, d)
    for l in range(depth):
        h = _ffn(h, ffn1_pre_g[l][None], ffn1_w_in[l].astype(BF16), ffn1_w_out[l].astype(BF16),
                 ffn1_post_g[l][None], tm=512, tf=512)

        g_mix = mix_pre_g[l][None]
        col_scale = np.ones((w_in.shape[2],), np.float32)
        col_scale[COL_Q_SB * HEAD_DIM:(COL_Q_SB + SB_HEADS) * HEAD_DIM] = HEAD_DIM ** -0.5
        col_scale[COL_Q_NSA * HEAD_DIM:(COL_Q_NSA + NSA_HEADS) * HEAD_DIM] = HEAD_DIM ** -0.5
        w_bf = (w_in[l] * jnp.asarray(col_scale)[None, :]).astype(BF16)
        gate_end = QKV_COLS + N_GATE_LOGITS
        w_logits = jnp.concatenate([w_bf[:, gate_end:], w_bf[:, QKV_COLS:gate_end],
                                    jnp.zeros((d, GATE_PAD - N_GATE_LOGITS), BF16)], axis=1)
        proj = _norm_matmul(h, g_mix, w_bf, BF16, tm=1024, tn=_col_tile(QKV_COLS), n_out=QKV_COLS,
                            name="in_proj_qkv")
        logits = _norm_matmul(h, g_mix, w_logits, F32, tm=1024, tn=_col_tile(2 * d + GATE_PAD),
                              n_out=2 * d + GATE_PAD, name="in_proj_logits")
        proj = proj.reshape(b, s, QKV_COLS)

        sb = _sb_attention(proj, tq=256, heads=8)

        n_chunk = s // CMP_STRIDE
        kv = proj[:, :, COL_K_CMP * HEAD_DIM:COL_K_SLC * HEAD_DIM]
        kv = kv.reshape(b, n_chunk, CMP_STRIDE, 2, NSA_GROUPS, HEAD_DIM)
        kv = jnp.transpose(kv, (3, 0, 4, 1, 2, 5)).reshape(2, b * NSA_GROUPS, n_chunk, CMP_STRIDE * HEAD_DIM)
        half = CMP_STRIDE * HEAD_DIM
        pos = jnp.stack([cmp_pos_k[l], cmp_pos_v[l]]).reshape(2, 2, half)
        w1 = jnp.stack([cmp_k_w1[l], cmp_v_w1[l]]).astype(BF16).reshape(2, 2, half, HEAD_DIM)
        w2 = jnp.stack([cmp_k_w2[l], cmp_v_w2[l]]).astype(BF16)
        kvc = _compress(kv, pos, w1, w2)

        nsa = _nsa_attention(proj, kvc,
                             logits.reshape(b, s, 2 * d + GATE_PAD), tq=512)

        h = _merge_out(sb.reshape(m, -1), nsa.reshape(m, -1), logits, h,
                       w_branch_sb[l].astype(BF16), w_branch_nsa[l].astype(BF16), w_out[l].astype(BF16),
                       mix_post_g[l][None], tm=512, tn=1024)

        h = _ffn(h, ffn2_pre_g[l][None], ffn2_w_in[l].astype(BF16), ffn2_w_out[l].astype(BF16),
                 ffn2_post_g[l][None], tm=512, tf=512)
    return h.reshape(b, s, d)
```

```python
import functools

import numpy as np
import jax
import jax.numpy as jnp
from jax import lax
from jax.experimental import pallas as pl
from jax.experimental.pallas import tpu as pltpu

HEAD_DIM = 128
SB_HEADS = 8
NSA_HEADS = 8
NSA_GROUPS = 2
NSA_HPG = NSA_HEADS // NSA_GROUPS
CMP_BLOCK = 32
CMP_STRIDE = 16
SLC_BLOCK = 64
N_SELECT = 16
WINDOW = 512
NORM_EPS = 1e-6
NEG_INF = -1e30
SCORE_SCALE = 2.0 ** 64
FORCED_SCORE = 1e30

QKV_COLS = (3 * SB_HEADS + NSA_HEADS + 6 * NSA_GROUPS) * HEAD_DIM
COL_Q_SB = 0
COL_K_SB = SB_HEADS
COL_V_SB = 2 * SB_HEADS
COL_Q_NSA = 3 * SB_HEADS
COL_K_CMP = COL_Q_NSA + NSA_HEADS
COL_K_SLC = COL_K_CMP + 2 * NSA_GROUPS
COL_V_SLC = COL_K_SLC + NSA_GROUPS
COL_K_WIN = COL_V_SLC + NSA_GROUPS
COL_V_WIN = COL_K_WIN + NSA_GROUPS
N_GATE_LOGITS = 3 * NSA_HEADS
GATE_PAD = 128

LANES = 128
V7X_VMEM_LIMIT = 56 * 1024 * 1024

F32 = jnp.float32
BF16 = jnp.bfloat16
NT_DIMS = (((1,), (1,)), ((), ()))


def _params(semantics):
    return pltpu.CompilerParams(dimension_semantics=semantics,
                                vmem_limit_bytes=V7X_VMEM_LIMIT)


def _rms(x, g):
    ms = jnp.mean(x * x, axis=-1, keepdims=True)
    return x * lax.rsqrt(ms + NORM_EPS) * g


def _ffn_body(x_ref, gpre_ref, wg_ref, wu_ref, wo_ref, gpost_ref, o_ref, h_ref, acc_ref):
    j = pl.program_id(1)

    @pl.when(j == 0)
    def _():
        h_ref[...] = _rms(x_ref[...], gpre_ref[...]).astype(BF16)
        acc_ref[...] = jnp.zeros_like(acc_ref)

    h = h_ref[...]
    gate = jnp.dot(h, wg_ref[...], preferred_element_type=F32)
    up = jnp.dot(h, wu_ref[...], preferred_element_type=F32)
    act = (gate * jax.nn.sigmoid(gate)) * up
    acc_ref[...] += jnp.dot(act.astype(BF16), wo_ref[...], preferred_element_type=F32)

    @pl.when(j == pl.num_programs(1) - 1)
    def _():
        o_ref[...] = x_ref[...] + 0.5 * _rms(acc_ref[...], gpost_ref[...])


def _ffn(x, g_pre, w_in, w_out, g_post, *, tm, tf):
    m, d = x.shape
    f = w_out.shape[0]
    nf = f // tf
    assert m % tm == 0 and f % tf == 0 and w_in.shape == (d, 2 * f)
    return pl.pallas_call(
        _ffn_body,
        name="ffn",
        grid=(m // tm, nf),
        in_specs=[
            pl.BlockSpec((tm, d), lambda i, j: (i, 0)),
            pl.BlockSpec((1, d), lambda i, j: (0, 0)),
            pl.BlockSpec((d, tf), lambda i, j: (0, j)),
            pl.BlockSpec((d, tf), lambda i, j: (0, j + nf)),
            pl.BlockSpec((tf, d), lambda i, j: (j, 0)),
            pl.BlockSpec((1, d), lambda i, j: (0, 0)),
        ],
        out_specs=pl.BlockSpec((tm, d), lambda i, j: (i, 0)),
        out_shape=jax.ShapeDtypeStruct((m, d), F32),
        scratch_shapes=[pltpu.VMEM((tm, d), BF16), pltpu.VMEM((tm, d), F32)],
        compiler_params=_params(("parallel", "arbitrary")),
    )(x, g_pre, w_in, w_in, w_out, g_post)


def _norm_matmul_body(x_ref, g_ref, w_ref, o_ref, h_ref):
    @pl.when(pl.program_id(1) == 0)
    def _():
        h_ref[...] = _rms(x_ref[...], g_ref[...]).astype(BF16)

    o_ref[...] = jnp.dot(h_ref[...], w_ref[...], preferred_element_type=F32).astype(o_ref.dtype)


MAX_COL_TILE = 1536


def _col_tile(n):
    assert n % LANES == 0
    blocks = n // LANES
    best = max(k for k in range(1, MAX_COL_TILE // LANES + 1) if blocks % k == 0)
    return best * LANES


def _norm_matmul(x, g, w, out_dtype, *, tm, tn, n_out, name):
    m, d = x.shape
    n = n_out
    assert m % tm == 0 and n % tn == 0 and n <= w.shape[1]
    return pl.pallas_call(
        _norm_matmul_body,
        name=name,
        grid=(m // tm, n // tn),
        in_specs=[
            pl.BlockSpec((tm, d), lambda i, j: (i, 0)),
            pl.BlockSpec((1, d), lambda i, j: (0, 0)),
            pl.BlockSpec((d, tn), lambda i, j: (0, j)),
        ],
        out_specs=pl.BlockSpec((tm, tn), lambda i, j: (i, j)),
        out_shape=jax.ShapeDtypeStruct((m, n), out_dtype),
        scratch_shapes=[pltpu.VMEM((tm, d), BF16)],
        compiler_params=_params(("parallel", "arbitrary")),
    )(x, g, w)


EXP_UNDERFLOW = 104.0
TN_DIMS = (((0,), (0,)), ((), ()))
LOG2_E = 1.4426950408889634


def _sb_body(q_ref, k_ref, v_ref, tri_ref, o_ref, c_ref, acc_ref, *, tq, heads):
    i = pl.program_id(2)
    wide = heads * tq
    tri = tri_ref[...]
    key_i = lax.broadcasted_iota(jnp.int32, (tq, wide), 0)
    qry_i = lax.rem(lax.broadcasted_iota(jnp.int32, (tq, wide), 1), tq)
    past = key_i < qry_i
    head_lanes = [slice(h * HEAD_DIM, (h + 1) * HEAD_DIM) for h in range(heads)]

    def tile(j, diag):
        k0 = pl.multiple_of(j * tq, tq)
        z = jnp.concatenate(
            [lax.dot_general(k_ref[0, pl.ds(k0, tq), hl], q_ref[0, :, hl], NT_DIMS,
                             preferred_element_type=F32) for hl in head_lanes], axis=1)
        softplus = jnp.maximum(z, 0.0) + jnp.log(1.0 + jnp.exp2(jnp.abs(z) * (-LOG2_E)))
        log_beta = z - softplus
        if diag:
            softplus = jnp.where(past, softplus, 0.0)
        hi = softplus.astype(BF16)
        lo = (softplus - hi.astype(F32)).astype(BF16)
        later = jnp.dot(tri, jnp.concatenate([hi, lo], axis=0), preferred_element_type=F32)
        c = c_ref[...]
        w = jnp.exp(log_beta - c - later)
        if diag:
            w = jnp.where(past, w, 0.0)
        w = w.astype(BF16)
        for h, hl in enumerate(head_lanes):
            cols = slice(h * tq, (h + 1) * tq)
            acc_ref[:, cols] += lax.dot_general(v_ref[0, pl.ds(k0, tq), hl], w[:, cols], TN_DIMS,
                                                preferred_element_type=F32)
        c = c + jnp.sum(softplus, axis=0, keepdims=True)
        c_ref[...] = c
        return (jnp.min(c) <= EXP_UNDERFLOW).astype(jnp.int32)

    c_ref[...] = jnp.zeros_like(c_ref)
    acc_ref[...] = jnp.zeros_like(acc_ref)
    alive = tile(i, True)

    def keep_going(carry):
        j, alive = carry
        return jnp.logical_and(j >= 0, alive > 0)

    lax.while_loop(keep_going, lambda carry: (carry[0] - 1, tile(carry[0], False)), (i - 1, alive))
    for h, hl in enumerate(head_lanes):
        o_ref[0, :, hl] = acc_ref[:, h * tq:(h + 1) * tq].T.astype(o_ref.dtype)


def _sb_attention(proj, *, tq, heads):
    b, s, _ = proj.shape
    assert s % tq == 0 and SB_HEADS % heads == 0
    upper = np.triu(np.ones((tq, tq), np.float32), 1)
    tri = jnp.asarray(np.concatenate([upper, upper], axis=1), BF16)
    width = heads * HEAD_DIM
    return pl.pallas_call(
        functools.partial(_sb_body, tq=tq, heads=heads),
        name="sb_attn",
        grid=(b, SB_HEADS // heads, s // tq),
        in_specs=[
            pl.BlockSpec((1, tq, width), lambda bb, h, i: (bb, i, COL_Q_SB // heads + h)),
            pl.BlockSpec((1, s, width), lambda bb, h, i: (bb, 0, COL_K_SB // heads + h)),
            pl.BlockSpec((1, s, width), lambda bb, h, i: (bb, 0, COL_V_SB // heads + h)),
            pl.BlockSpec((tq, 2 * tq), lambda bb, h, i: (0, 0)),
        ],
        out_specs=pl.BlockSpec((1, tq, width), lambda bb, h, i: (bb, i, h)),
        out_shape=jax.ShapeDtypeStruct((b, s, SB_HEADS * HEAD_DIM), BF16),
        scratch_shapes=[pltpu.VMEM((1, heads * tq), F32), pltpu.VMEM((HEAD_DIM, heads * tq), F32)],
        compiler_params=_params(("parallel", "parallel", "arbitrary")),
    )(proj, proj, proj, tri)


def _compress_body(x_ref, pos_ref, w1_ref, w2_ref, o_ref):
    x = x_ref[0, 0].astype(F32)
    pos = pos_ref[0]
    xa = (x + pos[0:1]).astype(BF16)
    xb = (x + pos[1:2]).astype(BF16)
    first = jnp.dot(xa, w1_ref[0, 0], preferred_element_type=F32)
    second = jnp.dot(xb, w1_ref[0, 1], preferred_element_type=F32)
    n_chunk = x.shape[0]
    pre = first + pltpu.roll(second, n_chunk - 1, 0)
    y = jax.nn.gelu(pre).astype(BF16)
    o_ref[0, 0] = jnp.dot(y, w2_ref[0], preferred_element_type=F32).astype(o_ref.dtype)


def _compress(x, pos, w1, w2):
    _, bg, n_chunk, width = x.shape
    return pl.pallas_call(
        _compress_body,
        name="compress",
        grid=(2, bg),
        in_specs=[
            pl.BlockSpec((1, 1, n_chunk, width), lambda a, n: (a, n, 0, 0)),
            pl.BlockSpec((1, 2, width), lambda a, n: (a, 0, 0)),
            pl.BlockSpec((1, 2, width, HEAD_DIM), lambda a, n: (a, 0, 0, 0)),
            pl.BlockSpec((1, HEAD_DIM, HEAD_DIM), lambda a, n: (a, 0, 0)),
        ],
        out_specs=pl.BlockSpec((1, 1, n_chunk, HEAD_DIM), lambda a, n: (a, n, 0, 0)),
        out_shape=jax.ShapeDtypeStruct((2, bg, n_chunk, HEAD_DIM), BF16),
        compiler_params=_params(("parallel", "parallel")),
    )(x, pos, w1, w2)


POS_HI_COL = 0
POS_LO_COL = 1
BLOCK_COL0 = HEAD_DIM // 2
SLAB_PARTS_PER_HEAD = 2


def _key_extra_columns(s, tk):
    pos = np.arange(s)
    extra = np.zeros((s, HEAD_DIM), np.float32)
    extra[:, POS_HI_COL] = (pos % tk) // SLC_BLOCK
    extra[:, POS_LO_COL] = pos % SLC_BLOCK
    extra[pos, BLOCK_COL0 + pos // SLC_BLOCK] = 1.0
    return jnp.asarray(extra, BF16)


def _nsa_body(q_ref, kc_ref, vc_ref, ks_ref, vs_ref, kw_ref, vw_ref, gate_ref,
              pool_ref, kextra_ref, o_ref, ksa_ref, kwa_ref, m_ref, l_ref, acc_ref, *, tq, n_slc):
    g = pl.program_id(1)
    i = pl.program_id(2)
    q0 = i * tq
    n_cmp_pad = kc_ref.shape[2]

    @pl.when(i == 0)
    def _():
        ksa_ref[:, :HEAD_DIM] = ks_ref[0]
        ksa_ref[:, HEAD_DIM:] = kextra_ref[...]
        kwa_ref[:, :HEAD_DIM] = kw_ref[0]
        kwa_ref[:, HEAD_DIM:] = kextra_ref[...]

    slopes = [jnp.where(g == 0, 2.0 ** -(r + 1), 2.0 ** -(r + 1 + NSA_HPG)) for r in range(NSA_HPG)]
    q_heads = [q_ref[0, :, r * HEAD_DIM:(r + 1) * HEAD_DIM] for r in range(NSA_HPG)]
    t_row = q0 + lax.broadcasted_iota(jnp.int32, (1, tq), 1)

    wide = NSA_HPG * tq
    head_of_lane = lax.broadcasted_iota(jnp.int32, (1, wide), 1) // tq
    slope_row = jnp.zeros((1, wide), F32)
    for r in range(NSA_HPG):
        slope_row = jnp.where(head_of_lane == r, slopes[r], slope_row)

    def split_heads(x):
        return [x[:, r * tq:(r + 1) * tq] for r in range(NSA_HPG)]

    kc = kc_ref[0, 0]
    vc = vc_ref[0, 0]
    c_idx = lax.broadcasted_iota(jnp.int32, (n_cmp_pad, wide), 0)
    t_wide = q0 + lax.rem(lax.broadcasted_iota(jnp.int32, (1, wide), 1), tq)
    cmp_end = c_idx * CMP_STRIDE + (CMP_BLOCK - 1)
    valid_c = jnp.logical_and(t_wide >= cmp_end, c_idx < n_cmp_pad - 1)
    s = lax.dot_general(kc, jnp.concatenate(q_heads, axis=0), NT_DIMS, preferred_element_type=F32)
    s = jnp.where(valid_c, s + slope_row * (cmp_end - q0).astype(F32), NEG_INF)
    m = jnp.max(s, axis=0, keepdims=True)
    p = jnp.where(valid_c, jnp.exp(s - m), 0.0)
    l = jnp.sum(p, axis=0, keepdims=True)
    p = p * (1.0 / jnp.where(l > 0.0, l, 1.0))
    o_cmp = split_heads(lax.dot_general(vc, p.astype(BF16), TN_DIMS,
                                        preferred_element_type=F32))
    p_grp = functools.reduce(lambda a, b: a + b, split_heads(p))

    pool = pool_ref[...]
    p_grp = p_grp * SCORE_SCALE
    p1 = p_grp.astype(BF16)
    r1 = p_grp - p1.astype(F32)
    p2 = r1.astype(BF16)
    p3 = (r1 - p2.astype(F32)).astype(BF16)
    score = (jnp.dot(pool, p1, preferred_element_type=F32)
             + jnp.dot(pool, p2, preferred_element_type=F32)
             + jnp.dot(pool, p3, preferred_element_type=F32))
    blk = lax.broadcasted_iota(jnp.int32, (n_slc, tq), 0)
    cur = t_row // SLC_BLOCK
    forced = jnp.logical_or(blk == 0, jnp.logical_or(blk == cur, blk == cur - 1))
    score = jnp.where(forced, FORCED_SCORE, score)
    score = jnp.where(blk <= cur, score, NEG_INF)

    sub = 8
    sub_i = lax.broadcasted_iota(jnp.int32, (sub, tq), 0)
    groups = [score[a:a + sub, :] for a in range(0, n_slc, sub)]
    ranks = [jnp.zeros((sub, tq), F32) for _ in groups]
    for ii in range(n_slc):
        row = score[ii:ii + 1, :]
        for gi, grp in enumerate(groups):
            ge = jnp.where(row >= grp, 1.0, 0.0)
            gt = jnp.where(row > grp, 1.0, 0.0)
            if ii < gi * sub:
                beats = ge
            elif ii >= (gi + 1) * sub:
                beats = gt
            else:
                beats = jnp.where(sub_i > ii - gi * sub, ge, gt)
            ranks[gi] = ranks[gi] + beats
    rank = jnp.concatenate(ranks, axis=0)
    sel_bias_t = jnp.where(rank < float(N_SELECT), 0.0, NEG_INF)

    pieces = [jnp.zeros((BLOCK_COL0, tq), F32), sel_bias_t]
    if BLOCK_COL0 + n_slc < HEAD_DIM:
        pieces.append(jnp.zeros((HEAD_DIM - BLOCK_COL0 - n_slc, tq), F32))
    sel_extra = jnp.concatenate(pieces, axis=0).T
    lane = lax.broadcasted_iota(jnp.int32, (tq, HEAD_DIM), 1)

    def query_slab(r, extra):
        pos_cols = jnp.where(lane == POS_HI_COL, slopes[r] * SLC_BLOCK,
                             jnp.where(lane == POS_LO_COL, slopes[r], extra))
        return jnp.concatenate([q_heads[r], pos_cols.astype(BF16)], axis=1)

    q_slc = jnp.concatenate([query_slab(r, sel_extra) for r in range(NSA_HPG)], axis=0)
    q_win = jnp.concatenate([query_slab(r, 0.0) for r in range(NSA_HPG)], axis=0)

    part_w = tq // SLAB_PARTS_PER_HEAD
    parts = [(r * tq + a * part_w, a) for r in range(NSA_HPG) for a in range(SLAB_PARTS_PER_HEAD)]

    def key_rows(d, a, limit):
        lo = max(0, d * tq + a * part_w - limit)
        hi = min(tq - 1, d * tq + (a + 1) * part_w - 1)
        return (lo // part_w) * part_w, (hi // part_w + 1) * part_w

    def distance_mask(d, a, limit, rows):
        lo, hi = rows
        dist_min = d * tq + a * part_w - (hi - 1)
        dist_max = d * tq + (a + 1) * part_w - 1 - lo
        if dist_min >= 0 and dist_max <= limit:
            return None
        key = lo + lax.broadcasted_iota(jnp.int32, (hi - lo, part_w), 0)
        qry = a * part_w + lax.broadcasted_iota(jnp.int32, (hi - lo, part_w), 1)
        dist = d * tq + qry - key
        return jnp.logical_and(dist >= 0, dist <= limit)

    def scores(ka_ref, q_all, j, rows_of):
        k0 = pl.multiple_of(j * tq, tq)
        out = []
        for c0, a in parts:
            lo, hi = rows_of(a)
            ka = ka_ref[pl.ds(k0 + lo, hi - lo), :]
            out.append(lax.dot_general(ka, q_all[c0:c0 + part_w, :], NT_DIMS, preferred_element_type=F32))
        return out

    def reset():
        m_ref[...] = jnp.full((1, wide), NEG_INF, F32)
        l_ref[...] = jnp.zeros((1, wide), F32)
        acc_ref[...] = jnp.zeros((HEAD_DIM, wide), F32)

    def absorb(v_ref, s_parts, j, rows_of, mask_of, live=None):
        k0 = pl.multiple_of(j * tq, tq)
        tile_off = (k0 - q0).astype(F32)
        for (c0, a), s in zip(parts, s_parts):
            cols = slice(c0, c0 + part_w)
            lo, hi = rows_of(a)
            mask = mask_of(a)
            if mask is not None:
                s = jnp.where(mask, s, NEG_INF)
            if live is not None:
                s = jnp.where(live, s, NEG_INF)
            off = slope_row[:, cols] * tile_off
            m = m_ref[:, cols]
            m_new = jnp.maximum(m, jnp.max(s, axis=0, keepdims=True) + off)
            alpha = jnp.exp(m - m_new)
            p = jnp.exp(s - (m_new - off))
            m_ref[:, cols] = m_new
            l_ref[:, cols] = alpha * l_ref[:, cols] + jnp.sum(p, axis=0, keepdims=True)
            acc_ref[:, cols] = alpha * acc_ref[:, cols] + lax.dot_general(
                v_ref[0, pl.ds(k0 + lo, hi - lo), :], p.astype(BF16), TN_DIMS, preferred_element_type=F32)

    def finish():
        return split_heads(acc_ref[...] * (1.0 / l_ref[...]))

    def tile_plan(d, limit):
        rows_of = lambda a: key_rows(d, a, limit)
        return rows_of, (lambda a: distance_mask(d, a, limit, rows_of(a)))

    all_rows = lambda a: (0, tq)
    no_mask = lambda a: None

    reset()
    rows_of, mask_of = tile_plan(0, tq)
    absorb(vs_ref, scores(ksa_ref, q_slc, i, rows_of), i, rows_of, mask_of)

    @pl.loop(0, i)
    def _(j):
        absorb(vs_ref, scores(ksa_ref, q_slc, j, all_rows), j, all_rows, no_mask)

    o_slc = finish()

    reset()
    for d in range((WINDOW - 1) // tq + 2):
        rows_of, mask_of = tile_plan(d, WINDOW - 1)
        j = jnp.maximum(i - d, 0)
        absorb(vw_ref, scores(kwa_ref, q_win, j, rows_of), j, rows_of, mask_of,
               live=None if d == 0 else i - d >= 0)
    o_win = finish()

    sg_t = jax.nn.sigmoid(gate_ref[0]).T

    def gate(branch, r):
        c0 = branch * NSA_HEADS + r
        c1 = c0 + NSA_HPG
        return jnp.where(g == 0, sg_t[c0:c0 + 1, :], sg_t[c1:c1 + 1, :])

    for r in range(NSA_HPG):
        out_t = gate(0, r) * o_cmp[r] + gate(1, r) * o_slc[r] + gate(2, r) * o_win[r]
        o_ref[0, :, r * HEAD_DIM:(r + 1) * HEAD_DIM] = out_t.T.astype(o_ref.dtype)


def _nsa_attention(proj, kvc, gm, *, tq):
    b, s, _ = proj.shape
    n_cmp_pad = s // CMP_STRIDE
    n_slc = s // SLC_BLOCK
    assert s % tq == 0 and tq % SLC_BLOCK == 0 and n_cmp_pad % LANES == 0
    assert BLOCK_COL0 + n_slc <= HEAD_DIM and tq // SLC_BLOCK <= 256 and n_slc % 8 == 0
    ratio = SLC_BLOCK // CMP_STRIDE
    span = CMP_BLOCK // CMP_STRIDE
    pool = np.zeros((n_slc, n_cmp_pad), np.float32)
    for jj in range(n_slc):
        for mm in range(ratio):
            for nn in range(span):
                c = ratio * jj + mm + nn
                if c < n_cmp_pad:
                    pool[jj, c] += 1.0
    grp_w = NSA_HPG * HEAD_DIM
    kv_spec = lambda col: pl.BlockSpec((1, s, HEAD_DIM), lambda bb, g, i: (bb, 0, col + g))
    cmp_spec = lambda a: pl.BlockSpec((1, 1, n_cmp_pad, HEAD_DIM),
                                      lambda bb, g, i: (a, bb * NSA_GROUPS + g, 0, 0))
    return pl.pallas_call(
        functools.partial(_nsa_body, tq=tq, n_slc=n_slc),
        name="nsa_attn",
        grid=(b, NSA_GROUPS, s // tq),
        in_specs=[
            pl.BlockSpec((1, tq, grp_w), lambda bb, g, i: (bb, i, COL_Q_NSA // NSA_HPG + g)),
            cmp_spec(0), cmp_spec(1),
            kv_spec(COL_K_SLC), kv_spec(COL_V_SLC), kv_spec(COL_K_WIN), kv_spec(COL_V_WIN),
            pl.BlockSpec((1, tq, GATE_PAD), lambda bb, g, i: (bb, i, gm.shape[2] // GATE_PAD - 1)),
            pl.BlockSpec((n_slc, n_cmp_pad), lambda bb, g, i: (0, 0)),
            pl.BlockSpec((s, HEAD_DIM), lambda bb, g, i: (0, 0)),
        ],
        out_specs=pl.BlockSpec((1, tq, grp_w), lambda bb, g, i: (bb, i, g)),
        out_shape=jax.ShapeDtypeStruct((b, s, NSA_HEADS * HEAD_DIM), BF16),
        scratch_shapes=[pltpu.VMEM((s, 2 * HEAD_DIM), BF16), pltpu.VMEM((s, 2 * HEAD_DIM), BF16),
                        pltpu.VMEM((1, NSA_HPG * tq), F32), pltpu.VMEM((1, NSA_HPG * tq), F32),
                        pltpu.VMEM((HEAD_DIM, NSA_HPG * tq), F32)],
        compiler_params=_params(("parallel", "parallel", "arbitrary")),
    )(proj, kvc, kvc, proj, proj, proj, proj, gm, jnp.asarray(pool, BF16), _key_extra_columns(s, tq))


def _merge_body(sb_ref, nsa_ref, m0_ref, m1_ref, x_ref, wsb_ref, wnsa_ref, wout_ref, g_ref, o_ref):
    y_sb = jnp.dot(sb_ref[...], wsb_ref[...], preferred_element_type=F32)
    y_nsa = jnp.dot(nsa_ref[...], wnsa_ref[...], preferred_element_type=F32)
    merged = jax.nn.sigmoid(m0_ref[...]) * y_sb + jax.nn.sigmoid(m1_ref[...]) * y_nsa
    y = jnp.dot(merged.astype(BF16), wout_ref[...], preferred_element_type=F32)
    o_ref[...] = x_ref[...] + _rms(y, g_ref[...])


def _merge_out(sb, nsa, gm, x, w_sb, w_nsa, w_out, g_post, *, tm):
    m, d = x.shape
    assert m % tm == 0 and gm.shape[0] == m and gm.shape[1] >= 2 * d

    def resident(shape):
        return pl.BlockSpec(shape, lambda i: (0, 0), pipeline_mode=pl.Buffered(1))

    return pl.pallas_call(
        _merge_body,
        name="merge_out",
        grid=(m // tm,),
        in_specs=[
            pl.BlockSpec((tm, sb.shape[1]), lambda i: (i, 0)),
            pl.BlockSpec((tm, nsa.shape[1]), lambda i: (i, 0)),
            pl.BlockSpec((tm, d), lambda i: (i, 0)),
            pl.BlockSpec((tm, d), lambda i: (i, 1)),
            pl.BlockSpec((tm, d), lambda i: (i, 0)),
            resident(w_sb.shape), resident(w_nsa.shape), resident(w_out.shape),
            pl.BlockSpec((1, d), lambda i: (0, 0)),
        ],
        out_specs=pl.BlockSpec((tm, d), lambda i: (i, 0)),
        out_shape=jax.ShapeDtypeStruct((m, d), F32),
        compiler_params=_params(("parallel",)),
    )(sb, nsa, gm, gm, x, w_sb, w_nsa, w_out, g_post)


def kernel(x, ffn1_pre_g, ffn1_w_in, ffn1_w_out, ffn1_post_g, mix_pre_g, w_in, cmp_pos_k, cmp_k_w1, cmp_k_w2, cmp_pos_v, cmp_v_w1, cmp_v_w2, w_branch_sb, w_branch_nsa, w_out, mix_post_g, ffn2_pre_g, ffn2_w_in, ffn2_w_out, ffn2_post_g):
    b, s, d = x.shape
    m = b * s
    depth = ffn1_pre_g.shape[0]
    h = x.reshape(m, d)
    for l in range(depth):
        h = _ffn(h, ffn1_pre_g[l][None], ffn1_w_in[l].astype(BF16), ffn1_w_out[l].astype(BF16),
                 ffn1_post_g[l][None], tm=512, tf=512)

        g_mix = mix_pre_g[l][None]
        col_scale = np.ones((w_in.shape[2],), np.float32)
        col_scale[COL_Q_SB * HEAD_DIM:(COL_Q_SB + SB_HEADS) * HEAD_DIM] = HEAD_DIM ** -0.5
        col_scale[COL_Q_NSA * HEAD_DIM:(COL_Q_NSA + NSA_HEADS) * HEAD_DIM] = HEAD_DIM ** -0.5
        w_bf = (w_in[l] * jnp.asarray(col_scale)[None, :]).astype(BF16)
        gate_end = QKV_COLS + N_GATE_LOGITS
        w_logits = jnp.concatenate([w_bf[:, gate_end:], w_bf[:, QKV_COLS:gate_end],
                                    jnp.zeros((d, GATE_PAD - N_GATE_LOGITS), BF16)], axis=1)
        proj = _norm_matmul(h, g_mix, w_bf, BF16, tm=1024, tn=_col_tile(QKV_COLS), n_out=QKV_COLS,
                            name="in_proj_qkv")
        logits = _norm_matmul(h, g_mix, w_logits, F32, tm=1024, tn=_col_tile(2 * d + GATE_PAD),
                              n_out=2 * d + GATE_PAD, name="in_proj_logits")
        proj = proj.reshape(b, s, QKV_COLS)

        sb = _sb_attention(proj, tq=256, heads=8)

        n_chunk = s // CMP_STRIDE
        kv = proj[:, :, COL_K_CMP * HEAD_DIM:COL_K_SLC * HEAD_DIM]
        kv = kv.reshape(b, n_chunk, CMP_STRIDE, 2, NSA_GROUPS, HEAD_DIM)
        kv = jnp.transpose(kv, (3, 0, 4, 1, 2, 5)).reshape(2, b * NSA_GROUPS, n_chunk, CMP_STRIDE * HEAD_DIM)
        half = CMP_STRIDE * HEAD_DIM
        pos = jnp.stack([cmp_pos_k[l], cmp_pos_v[l]]).reshape(2, 2, half)
        w1 = jnp.stack([cmp_k_w1[l], cmp_v_w1[l]]).astype(BF16).reshape(2, 2, half, HEAD_DIM)
        w2 = jnp.stack([cmp_k_w2[l], cmp_v_w2[l]]).astype(BF16)
        kvc = _compress(kv, pos, w1, w2)

        nsa = _nsa_attention(proj, kvc,
                             logits.reshape(b, s, 2 * d + GATE_PAD), tq=512)

        h = _merge_out(sb.reshape(m, -1), nsa.reshape(m, -1), logits, h,
                       w_branch_sb[l].astype(BF16), w_branch_nsa[l].astype(BF16), w_out[l].astype(BF16),
                       mix_post_g[l][None], tm=256)

        h = _ffn(h, ffn2_pre_g[l][None], ffn2_w_in[l].astype(BF16), ffn2_w_out[l].astype(BF16),
                 ffn2_post_g[l][None], tm=512, tf=512)
    return h.reshape(b, s, d)
```

```python
import functools

import numpy as np
import jax
import jax.numpy as jnp
from jax import lax
from jax.experimental import pallas as pl
from jax.experimental.pallas import tpu as pltpu

HEAD_DIM = 128
SB_HEADS = 8
NSA_HEADS = 8
NSA_GROUPS = 2
NSA_HPG = NSA_HEADS // NSA_GROUPS
CMP_BLOCK = 32
CMP_STRIDE = 16
SLC_BLOCK = 64
N_SELECT = 16
WINDOW = 512
NORM_EPS = 1e-6
NEG_INF = -1e30
SCORE_SCALE = 2.0 ** 64
FORCED_SCORE = 1e30

QKV_COLS = (3 * SB_HEADS + NSA_HEADS + 6 * NSA_GROUPS) * HEAD_DIM
COL_Q_SB = 0
COL_K_SB = SB_HEADS
COL_V_SB = 2 * SB_HEADS
COL_Q_NSA = 3 * SB_HEADS
COL_K_CMP = COL_Q_NSA + NSA_HEADS
COL_K_SLC = COL_K_CMP + 2 * NSA_GROUPS
COL_V_SLC = COL_K_SLC + NSA_GROUPS
COL_K_WIN = COL_V_SLC + NSA_GROUPS
COL_V_WIN = COL_K_WIN + NSA_GROUPS
N_GATE_LOGITS = 3 * NSA_HEADS
GATE_PAD = 128

LANES = 128
V7X_VMEM_LIMIT = 56 * 1024 * 1024

F32 = jnp.float32
BF16 = jnp.bfloat16
NT_DIMS = (((1,), (1,)), ((), ()))


def _params(semantics):
    return pltpu.CompilerParams(dimension_semantics=semantics,
                                vmem_limit_bytes=V7X_VMEM_LIMIT)


def _rms(x, g):
    ms = jnp.mean(x * x, axis=-1, keepdims=True)
    return x * lax.rsqrt(ms + NORM_EPS) * g


def _ffn_body(x_ref, gpre_ref, wg_ref, wu_ref, wo_ref, gpost_ref, o_ref, h_ref, acc_ref):
    j = pl.program_id(1)

    @pl.when(j == 0)
    def _():
        h_ref[...] = _rms(x_ref[...], gpre_ref[...]).astype(BF16)
        acc_ref[...] = jnp.zeros_like(acc_ref)

    h = h_ref[...]
    gate = jnp.dot(h, wg_ref[...], preferred_element_type=F32)
    up = jnp.dot(h, wu_ref[...], preferred_element_type=F32)
    act = (gate * jax.nn.sigmoid(gate)) * up
    acc_ref[...] += jnp.dot(act.astype(BF16), wo_ref[...], preferred_element_type=F32)

    @pl.when(j == pl.num_programs(1) - 1)
    def _():
        o_ref[...] = x_ref[...] + 0.5 * _rms(acc_ref[...], gpost_ref[...])


def _ffn(x, g_pre, w_in, w_out, g_post, *, tm, tf):
    m, d = x.shape
    f = w_out.shape[0]
    nf = f // tf
    assert m % tm == 0 and f % tf == 0 and w_in.shape == (d, 2 * f)
    return pl.pallas_call(
        _ffn_body,
        name="ffn",
        grid=(m // tm, nf),
        in_specs=[
            pl.BlockSpec((tm, d), lambda i, j: (i, 0)),
            pl.BlockSpec((1, d), lambda i, j: (0, 0)),
            pl.BlockSpec((d, tf), lambda i, j: (0, j)),
            pl.BlockSpec((d, tf), lambda i, j: (0, j + nf)),
            pl.BlockSpec((tf, d), lambda i, j: (j, 0)),
            pl.BlockSpec((1, d), lambda i, j: (0, 0)),
        ],
        out_specs=pl.BlockSpec((tm, d), lambda i, j: (i, 0)),
        out_shape=jax.ShapeDtypeStruct((m, d), F32),
        scratch_shapes=[pltpu.VMEM((tm, d), BF16), pltpu.VMEM((tm, d), F32)],
        compiler_params=_params(("parallel", "arbitrary")),
    )(x, g_pre, w_in, w_in, w_out, g_post)


def _norm_matmul_body(x_ref, g_ref, w_ref, o_ref, h_ref):
    @pl.when(pl.program_id(1) == 0)
    def _():
        h_ref[...] = _rms(x_ref[...], g_ref[...]).astype(BF16)

    o_ref[...] = jnp.dot(h_ref[...], w_ref[...], preferred_element_type=F32).astype(o_ref.dtype)


MAX_COL_TILE = 1536


def _col_tile(n):
    assert n % LANES == 0
    blocks = n // LANES
    best = max(k for k in range(1, MAX_COL_TILE // LANES + 1) if blocks % k == 0)
    return best * LANES


def _norm_matmul(x, g, w, out_dtype, *, tm, tn, n_out, name, col0=0):
    m, d = x.shape
    n = n_out
    assert m % tm == 0 and n % tn == 0 and col0 % tn == 0 and col0 + n <= w.shape[1]
    jb = col0 // tn
    return pl.pallas_call(
        _norm_matmul_body,
        name=name,
        grid=(m // tm, n // tn),
        in_specs=[
            pl.BlockSpec((tm, d), lambda i, j: (i, 0)),
            pl.BlockSpec((1, d), lambda i, j: (0, 0)),
            pl.BlockSpec((d, tn), lambda i, j: (0, j + jb)),
        ],
        out_specs=pl.BlockSpec((tm, tn), lambda i, j: (i, j)),
        out_shape=jax.ShapeDtypeStruct((m, n), out_dtype),
        scratch_shapes=[pltpu.VMEM((tm, d), BF16)],
        compiler_params=_params(("parallel", "arbitrary")),
    )(x, g, w)


EXP_UNDERFLOW = 104.0
TN_DIMS = (((0,), (0,)), ((), ()))
LOG2_E = 1.4426950408889634


def _sb_body(q_ref, k_ref, v_ref, tri_ref, o_ref, c_ref, acc_ref, *, tq, heads):
    i = pl.program_id(2)
    wide = heads * tq
    tri = tri_ref[...]
    key_i = lax.broadcasted_iota(jnp.int32, (tq, wide), 0)
    qry_i = lax.rem(lax.broadcasted_iota(jnp.int32, (tq, wide), 1), tq)
    past = key_i < qry_i
    head_lanes = [slice(h * HEAD_DIM, (h + 1) * HEAD_DIM) for h in range(heads)]

    def tile(j, diag):
        k0 = pl.multiple_of(j * tq, tq)
        z = jnp.concatenate(
            [lax.dot_general(k_ref[0, pl.ds(k0, tq), hl], q_ref[0, :, hl], NT_DIMS,
                             preferred_element_type=F32) for hl in head_lanes], axis=1)
        softplus = jnp.maximum(z, 0.0) + jnp.log(1.0 + jnp.exp2(jnp.abs(z) * (-LOG2_E)))
        log_beta = z - softplus
        if diag:
            softplus = jnp.where(past, softplus, 0.0)
        hi = softplus.astype(BF16)
        lo = (softplus - hi.astype(F32)).astype(BF16)
        later = jnp.dot(tri, jnp.concatenate([hi, lo], axis=0), preferred_element_type=F32)
        c = c_ref[...]
        w = jnp.exp(log_beta - c - later)
        if diag:
            w = jnp.where(past, w, 0.0)
        w = w.astype(BF16)
        for h, hl in enumerate(head_lanes):
            cols = slice(h * tq, (h + 1) * tq)
            acc_ref[:, cols] += lax.dot_general(v_ref[0, pl.ds(k0, tq), hl], w[:, cols], TN_DIMS,
                                                preferred_element_type=F32)
        c = c + jnp.sum(softplus, axis=0, keepdims=True)
        c_ref[...] = c
        return (jnp.min(c) <= EXP_UNDERFLOW).astype(jnp.int32)

    c_ref[...] = jnp.zeros_like(c_ref)
    acc_ref[...] = jnp.zeros_like(acc_ref)
    alive = tile(i, True)

    def keep_going(carry):
        j, alive = carry
        return jnp.logical_and(j >= 0, alive > 0)

    lax.while_loop(keep_going, lambda carry: (carry[0] - 1, tile(carry[0], False)), (i - 1, alive))
    for h, hl in enumerate(head_lanes):
        o_ref[0, :, hl] = acc_ref[:, h * tq:(h + 1) * tq].T.astype(o_ref.dtype)


def _sb_attention(proj, *, tq, heads):
    b, s, _ = proj.shape
    assert s % tq == 0 and SB_HEADS % heads == 0
    upper = np.triu(np.ones((tq, tq), np.float32), 1)
    tri = jnp.asarray(np.concatenate([upper, upper], axis=1), BF16)
    width = heads * HEAD_DIM
    return pl.pallas_call(
        functools.partial(_sb_body, tq=tq, heads=heads),
        name="sb_attn",
        grid=(b, SB_HEADS // heads, s // tq),
        in_specs=[
            pl.BlockSpec((1, tq, width), lambda bb, h, i: (bb, i, COL_Q_SB // heads + h)),
            pl.BlockSpec((1, s, width), lambda bb, h, i: (bb, 0, COL_K_SB // heads + h)),
            pl.BlockSpec((1, s, width), lambda bb, h, i: (bb, 0, COL_V_SB // heads + h)),
            pl.BlockSpec((tq, 2 * tq), lambda bb, h, i: (0, 0)),
        ],
        out_specs=pl.BlockSpec((1, tq, width), lambda bb, h, i: (bb, i, h)),
        out_shape=jax.ShapeDtypeStruct((b, s, SB_HEADS * HEAD_DIM), BF16),
        scratch_shapes=[pltpu.VMEM((1, heads * tq), F32), pltpu.VMEM((HEAD_DIM, heads * tq), F32)],
        compiler_params=_params(("parallel", "parallel", "arbitrary")),
    )(proj, proj, proj, tri)


def _compress_body(x_ref, pos_ref, w1_ref, w2_ref, o_ref):
    n_chunk = o_ref.shape[2]
    first = jnp.zeros((n_chunk, HEAD_DIM), F32)
    second = jnp.zeros((n_chunk, HEAD_DIM), F32)
    for l in range(CMP_STRIDE):
        x_l = x_ref[0, pl.ds(l, n_chunk, stride=CMP_STRIDE), :]
        xa = (x_l + pos_ref[0, 0, l:l + 1, :]).astype(BF16)
        xb = (x_l + pos_ref[0, 1, l:l + 1, :]).astype(BF16)
        first = first + jnp.dot(xa, w1_ref[0, 0, l], preferred_element_type=F32)
        second = second + jnp.dot(xb, w1_ref[0, 1, l], preferred_element_type=F32)
    pre = first + pltpu.roll(second, n_chunk - 1, 0)
    y = jax.nn.gelu(pre).astype(BF16)
    o_ref[0, 0] = jnp.dot(y, w2_ref[0], preferred_element_type=F32).astype(o_ref.dtype)


def _compress(x, pos, w1, w2):
    b, s, _ = x.shape
    n_chunk = s // CMP_STRIDE
    return pl.pallas_call(
        _compress_body,
        name="compress",
        grid=(2, b * NSA_GROUPS),
        in_specs=[
            pl.BlockSpec((1, s, HEAD_DIM), lambda a, n: (n // NSA_GROUPS, 0, a * NSA_GROUPS + n % NSA_GROUPS)),
            pl.BlockSpec((1, 2, CMP_STRIDE, HEAD_DIM), lambda a, n: (a, 0, 0, 0)),
            pl.BlockSpec((1, 2, CMP_STRIDE, HEAD_DIM, HEAD_DIM), lambda a, n: (a, 0, 0, 0, 0)),
            pl.BlockSpec((1, HEAD_DIM, HEAD_DIM), lambda a, n: (a, 0, 0)),
        ],
        out_specs=pl.BlockSpec((1, 1, n_chunk, HEAD_DIM), lambda a, n: (a, n, 0, 0)),
        out_shape=jax.ShapeDtypeStruct((2, b * NSA_GROUPS, n_chunk, HEAD_DIM), BF16),
        compiler_params=_params(("parallel", "parallel")),
    )(x, pos, w1, w2)


POS_HI_COL = 0
POS_LO_COL = 1
BLOCK_COL0 = HEAD_DIM // 2
SLAB_PARTS_PER_HEAD = 2


def _key_extra_columns(s, tk):
    pos = np.arange(s)
    extra = np.zeros((s, HEAD_DIM), np.float32)
    extra[:, POS_HI_COL] = (pos % tk) // SLC_BLOCK
    extra[:, POS_LO_COL] = pos % SLC_BLOCK
    extra[pos, BLOCK_COL0 + pos // SLC_BLOCK] = 1.0
    return jnp.asarray(extra, BF16)


def _nsa_body(q_ref, kc_ref, vc_ref, ks_ref, vs_ref, kw_ref, vw_ref, gate_ref,
              pool_ref, kextra_ref, o_ref, ksa_ref, kwa_ref, m_ref, l_ref, acc_ref, *, tq, n_slc):
    g = pl.program_id(1)
    i = pl.program_id(2)
    q0 = i * tq
    n_cmp_pad = kc_ref.shape[2]

    @pl.when(i == 0)
    def _():
        ksa_ref[:, :HEAD_DIM] = ks_ref[0]
        ksa_ref[:, HEAD_DIM:] = kextra_ref[...]
        kwa_ref[:, :HEAD_DIM] = kw_ref[0]
        kwa_ref[:, HEAD_DIM:] = kextra_ref[...]

    slopes = [jnp.where(g == 0, 2.0 ** -(r + 1), 2.0 ** -(r + 1 + NSA_HPG)) for r in range(NSA_HPG)]
    q_heads = [q_ref[0, :, r * HEAD_DIM:(r + 1) * HEAD_DIM] for r in range(NSA_HPG)]
    t_row = q0 + lax.broadcasted_iota(jnp.int32, (1, tq), 1)

    wide = NSA_HPG * tq
    head_of_lane = lax.broadcasted_iota(jnp.int32, (1, wide), 1) // tq
    slope_row = jnp.zeros((1, wide), F32)
    for r in range(NSA_HPG):
        slope_row = jnp.where(head_of_lane == r, slopes[r], slope_row)

    def split_heads(x):
        return [x[:, r * tq:(r + 1) * tq] for r in range(NSA_HPG)]

    kc = kc_ref[0, 0]
    vc = vc_ref[0, 0]
    c_idx = lax.broadcasted_iota(jnp.int32, (n_cmp_pad, wide), 0)
    t_wide = q0 + lax.rem(lax.broadcasted_iota(jnp.int32, (1, wide), 1), tq)
    cmp_end = c_idx * CMP_STRIDE + (CMP_BLOCK - 1)
    valid_c = jnp.logical_and(t_wide >= cmp_end, c_idx < n_cmp_pad - 1)
    s = lax.dot_general(kc, jnp.concatenate(q_heads, axis=0), NT_DIMS, preferred_element_type=F32)
    s = jnp.where(valid_c, s + slope_row * (cmp_end - q0).astype(F32), NEG_INF)
    m = jnp.max(s, axis=0, keepdims=True)
    p = jnp.where(valid_c, jnp.exp(s - m), 0.0)
    l = jnp.sum(p, axis=0, keepdims=True)
    p = p * (1.0 / jnp.where(l > 0.0, l, 1.0))
    o_cmp = split_heads(lax.dot_general(vc, p.astype(BF16), TN_DIMS,
                                        preferred_element_type=F32))
    p_grp = functools.reduce(lambda a, b: a + b, split_heads(p))

    pool = pool_ref[...]
    p_grp = p_grp * SCORE_SCALE
    p1 = p_grp.astype(BF16)
    r1 = p_grp - p1.astype(F32)
    p2 = r1.astype(BF16)
    p3 = (r1 - p2.astype(F32)).astype(BF16)
    score = (jnp.dot(pool, p1, preferred_element_type=F32)
             + jnp.dot(pool, p2, preferred_element_type=F32)
             + jnp.dot(pool, p3, preferred_element_type=F32))
    blk = lax.broadcasted_iota(jnp.int32, (n_slc, tq), 0)
    cur = t_row // SLC_BLOCK
    forced = jnp.logical_or(blk == 0, jnp.logical_or(blk == cur, blk == cur - 1))
    score = jnp.where(forced, FORCED_SCORE, score)
    score = jnp.where(blk <= cur, score, NEG_INF)

    sub = 8
    sub_i = lax.broadcasted_iota(jnp.int32, (sub, tq), 0)
    groups = [score[a:a + sub, :] for a in range(0, n_slc, sub)]
    ranks = [jnp.zeros((sub, tq), F32) for _ in groups]
    for ii in range(n_slc):
        row = score[ii:ii + 1, :]
        for gi, grp in enumerate(groups):
            ge = jnp.where(row >= grp, 1.0, 0.0)
            gt = jnp.where(row > grp, 1.0, 0.0)
            if ii < gi * sub:
                beats = ge
            elif ii >= (gi + 1) * sub:
                beats = gt
            else:
                beats = jnp.where(sub_i > ii - gi * sub, ge, gt)
            ranks[gi] = ranks[gi] + beats
    rank = jnp.concatenate(ranks, axis=0)
    sel_bias_t = jnp.where(rank < float(N_SELECT), 0.0, NEG_INF)

    pieces = [jnp.zeros((BLOCK_COL0, tq), F32), sel_bias_t]
    if BLOCK_COL0 + n_slc < HEAD_DIM:
        pieces.append(jnp.zeros((HEAD_DIM - BLOCK_COL0 - n_slc, tq), F32))
    sel_extra = jnp.concatenate(pieces, axis=0).T
    lane = lax.broadcasted_iota(jnp.int32, (tq, HEAD_DIM), 1)

    def query_slab(r, extra):
        pos_cols = jnp.where(lane == POS_HI_COL, slopes[r] * SLC_BLOCK,
                             jnp.where(lane == POS_LO_COL, slopes[r], extra))
        return jnp.concatenate([q_heads[r], pos_cols.astype(BF16)], axis=1)

    q_slc = jnp.concatenate([query_slab(r, sel_extra) for r in range(NSA_HPG)], axis=0)
    q_win = jnp.concatenate([query_slab(r, 0.0) for r in range(NSA_HPG)], axis=0)

    part_w = tq // SLAB_PARTS_PER_HEAD
    parts = [(r * tq + a * part_w, a) for r in range(NSA_HPG) for a in range(SLAB_PARTS_PER_HEAD)]

    def key_rows(d, a, limit):
        lo = max(0, d * tq + a * part_w - limit)
        hi = min(tq - 1, d * tq + (a + 1) * part_w - 1)
        return (lo // part_w) * part_w, (hi // part_w + 1) * part_w

    def distance_mask(d, a, limit, rows):
        lo, hi = rows
        dist_min = d * tq + a * part_w - (hi - 1)
        dist_max = d * tq + (a + 1) * part_w - 1 - lo
        if dist_min >= 0 and dist_max <= limit:
            return None
        key = lo + lax.broadcasted_iota(jnp.int32, (hi - lo, part_w), 0)
        qry = a * part_w + lax.broadcasted_iota(jnp.int32, (hi - lo, part_w), 1)
        dist = d * tq + qry - key
        return jnp.logical_and(dist >= 0, dist <= limit)

    def scores(ka_ref, q_all, j, rows_of):
        k0 = pl.multiple_of(j * tq, tq)
        out = []
        for c0, a in parts:
            lo, hi = rows_of(a)
            ka = ka_ref[pl.ds(k0 + lo, hi - lo), :]
            out.append(lax.dot_general(ka, q_all[c0:c0 + part_w, :], NT_DIMS, preferred_element_type=F32))
        return out

    def reset():
        m_ref[...] = jnp.full((1, wide), NEG_INF, F32)
        l_ref[...] = jnp.zeros((1, wide), F32)
        acc_ref[...] = jnp.zeros((HEAD_DIM, wide), F32)

    def absorb(v_ref, s_parts, j, rows_of, mask_of, live=None):
        k0 = pl.multiple_of(j * tq, tq)
        tile_off = (k0 - q0).astype(F32)
        for (c0, a), s in zip(parts, s_parts):
            cols = slice(c0, c0 + part_w)
            lo, hi = rows_of(a)
            mask = mask_of(a)
            if mask is not None:
                s = jnp.where(mask, s, NEG_INF)
            if live is not None:
                s = jnp.where(live, s, NEG_INF)
            off = slope_row[:, cols] * tile_off
            m = m_ref[:, cols]
            m_new = jnp.maximum(m, jnp.max(s, axis=0, keepdims=True) + off)
            alpha = jnp.exp(m - m_new)
            p = jnp.exp(s - (m_new - off))
            m_ref[:, cols] = m_new
            l_ref[:, cols] = alpha * l_ref[:, cols] + jnp.sum(p, axis=0, keepdims=True)
            acc_ref[:, cols] = alpha * acc_ref[:, cols] + lax.dot_general(
                v_ref[0, pl.ds(k0 + lo, hi - lo), :], p.astype(BF16), TN_DIMS, preferred_element_type=F32)

    def finish():
        return split_heads(acc_ref[...] * (1.0 / l_ref[...]))

    def tile_plan(d, limit):
        rows_of = lambda a: key_rows(d, a, limit)
        return rows_of, (lambda a: distance_mask(d, a, limit, rows_of(a)))

    all_rows = lambda a: (0, tq)
    no_mask = lambda a: None

    reset()
    rows_of, mask_of = tile_plan(0, tq)
    absorb(vs_ref, scores(ksa_ref, q_slc, i, rows_of), i, rows_of, mask_of)

    @pl.loop(0, i)
    def _(j):
        absorb(vs_ref, scores(ksa_ref, q_slc, j, all_rows), j, all_rows, no_mask)

    o_slc = finish()

    reset()
    for d in range((WINDOW - 1) // tq + 2):
        rows_of, mask_of = tile_plan(d, WINDOW - 1)
        j = jnp.maximum(i - d, 0)
        absorb(vw_ref, scores(kwa_ref, q_win, j, rows_of), j, rows_of, mask_of,
               live=None if d == 0 else i - d >= 0)
    o_win = finish()

    sg_t = jax.nn.sigmoid(gate_ref[0]).T

    def gate(branch, r):
        c0 = branch * NSA_HEADS + r
        c1 = c0 + NSA_HPG
        return jnp.where(g == 0, sg_t[c0:c0 + 1, :], sg_t[c1:c1 + 1, :])

    for r in range(NSA_HPG):
        out_t = gate(0, r) * o_cmp[r] + gate(1, r) * o_slc[r] + gate(2, r) * o_win[r]
        o_ref[0, :, r * HEAD_DIM:(r + 1) * HEAD_DIM] = out_t.T.astype(o_ref.dtype)


def _nsa_attention(proj, kvc, gm, *, tq):
    b, s, _ = proj.shape
    n_cmp_pad = s // CMP_STRIDE
    n_slc = s // SLC_BLOCK
    assert s % tq == 0 and tq % SLC_BLOCK == 0 and n_cmp_pad % LANES == 0
    assert BLOCK_COL0 + n_slc <= HEAD_DIM and tq // SLC_BLOCK <= 256 and n_slc % 8 == 0
    ratio = SLC_BLOCK // CMP_STRIDE
    span = CMP_BLOCK // CMP_STRIDE
    pool = np.zeros((n_slc, n_cmp_pad), np.float32)
    for jj in range(n_slc):
        for mm in range(ratio):
            for nn in range(span):
                c = ratio * jj + mm + nn
                if c < n_cmp_pad:
                    pool[jj, c] += 1.0
    grp_w = NSA_HPG * HEAD_DIM
    kv_spec = lambda col: pl.BlockSpec((1, s, HEAD_DIM), lambda bb, g, i: (bb, 0, col + g))
    cmp_spec = lambda a: pl.BlockSpec((1, 1, n_cmp_pad, HEAD_DIM),
                                      lambda bb, g, i: (a, bb * NSA_GROUPS + g, 0, 0))
    return pl.pallas_call(
        functools.partial(_nsa_body, tq=tq, n_slc=n_slc),
        name="nsa_attn",
        grid=(b, NSA_GROUPS, s // tq),
        in_specs=[
            pl.BlockSpec((1, tq, grp_w), lambda bb, g, i: (bb, i, COL_Q_NSA // NSA_HPG + g)),
            cmp_spec(0), cmp_spec(1),
            kv_spec(COL_K_SLC), kv_spec(COL_V_SLC), kv_spec(COL_K_WIN), kv_spec(COL_V_WIN),
            pl.BlockSpec((1, tq, GATE_PAD), lambda bb, g, i: (bb, i, gm.shape[2] // GATE_PAD - 1)),
            pl.BlockSpec((n_slc, n_cmp_pad), lambda bb, g, i: (0, 0)),
            pl.BlockSpec((s, HEAD_DIM), lambda bb, g, i: (0, 0)),
        ],
        out_specs=pl.BlockSpec((1, tq, grp_w), lambda bb, g, i: (bb, i, g)),
        out_shape=jax.ShapeDtypeStruct((b, s, NSA_HEADS * HEAD_DIM), BF16),
        scratch_shapes=[pltpu.VMEM((s, 2 * HEAD_DIM), BF16), pltpu.VMEM((s, 2 * HEAD_DIM), BF16),
                        pltpu.VMEM((1, NSA_HPG * tq), F32), pltpu.VMEM((1, NSA_HPG * tq), F32),
                        pltpu.VMEM((HEAD_DIM, NSA_HPG * tq), F32)],
        compiler_params=_params(("parallel", "parallel", "arbitrary")),
    )(proj, kvc, kvc, proj, proj, proj, proj, gm, jnp.asarray(pool, BF16), _key_extra_columns(s, tq))


def _merge_body(sb_ref, nsa_ref, m0_ref, m1_ref, x_ref, wsb_ref, wnsa_ref, wout_ref, g_ref, o_ref):
    y_sb = jnp.dot(sb_ref[...], wsb_ref[...], preferred_element_type=F32)
    y_nsa = jnp.dot(nsa_ref[...], wnsa_ref[...], preferred_element_type=F32)
    merged = jax.nn.sigmoid(m0_ref[...]) * y_sb + jax.nn.sigmoid(m1_ref[...]) * y_nsa
    y = jnp.dot(merged.astype(BF16), wout_ref[...], preferred_element_type=F32)
    o_ref[...] = x_ref[...] + _rms(y, g_ref[...])


def _merge_out(sb, nsa, gm, x, w_sb, w_nsa, w_out, g_post, *, tm):
    m, d = x.shape
    assert m % tm == 0 and gm.shape[0] == m and gm.shape[1] >= 2 * d

    def resident(shape):
        return pl.BlockSpec(shape, lambda i: (0, 0), pipeline_mode=pl.Buffered(1))

    return pl.pallas_call(
        _merge_body,
        name="merge_out",
        grid=(m // tm,),
        in_specs=[
            pl.BlockSpec((tm, sb.shape[1]), lambda i: (i, 0)),
            pl.BlockSpec((tm, nsa.shape[1]), lambda i: (i, 0)),
            pl.BlockSpec((tm, d), lambda i: (i, 0)),
            pl.BlockSpec((tm, d), lambda i: (i, 1)),
            pl.BlockSpec((tm, d), lambda i: (i, 0)),
            resident(w_sb.shape), resident(w_nsa.shape), resident(w_out.shape),
            pl.BlockSpec((1, d), lambda i: (0, 0)),
        ],
        out_specs=pl.BlockSpec((tm, d), lambda i: (i, 0)),
        out_shape=jax.ShapeDtypeStruct((m, d), F32),
        compiler_params=_params(("parallel",)),
    )(sb, nsa, gm, gm, x, w_sb, w_nsa, w_out, g_post)


def kernel(x, ffn1_pre_g, ffn1_w_in, ffn1_w_out, ffn1_post_g, mix_pre_g, w_in, cmp_pos_k, cmp_k_w1, cmp_k_w2, cmp_pos_v, cmp_v_w1, cmp_v_w2, w_branch_sb, w_branch_nsa, w_out, mix_post_g, ffn2_pre_g, ffn2_w_in, ffn2_w_out, ffn2_post_g):
    b, s, d = x.shape
    m = b * s
    depth = ffn1_pre_g.shape[0]
    h = x.reshape(m, d)
    for l in range(depth):
        h = _ffn(h, ffn1_pre_g[l][None], ffn1_w_in[l].astype(BF16), ffn1_w_out[l].astype(BF16),
                 ffn1_post_g[l][None], tm=512, tf=512)

        g_mix = mix_pre_g[l][None]
        col_scale = np.ones((w_in.shape[2],), np.float32)
        col_scale[COL_Q_SB * HEAD_DIM:(COL_Q_SB + SB_HEADS) * HEAD_DIM] = HEAD_DIM ** -0.5
        col_scale[COL_Q_NSA * HEAD_DIM:(COL_Q_NSA + NSA_HEADS) * HEAD_DIM] = HEAD_DIM ** -0.5
        w_bf = (w_in[l] * jnp.asarray(col_scale)[None, :]).astype(BF16)
        gate_end = QKV_COLS + N_GATE_LOGITS
        w_logits = jnp.concatenate([w_bf[:, gate_end:], w_bf[:, QKV_COLS:gate_end],
                                    jnp.zeros((d, GATE_PAD - N_GATE_LOGITS), BF16)], axis=1)
        proj = _norm_matmul(h, g_mix, w_bf, BF16, tm=1024, tn=_col_tile(QKV_COLS), n_out=QKV_COLS,
                            name="in_proj_qkv")
        logits = _norm_matmul(h, g_mix, w_logits, F32, tm=1024, tn=_col_tile(2 * d + GATE_PAD),
                              n_out=2 * d + GATE_PAD, name="in_proj_logits")
        proj = proj.reshape(b, s, QKV_COLS)

        sb = _sb_attention(proj, tq=256, heads=8)

        cmp_cols = (COL_K_SLC - COL_K_CMP) * HEAD_DIM
        kv_cmp = _norm_matmul(h, g_mix, w_bf, F32, tm=1024, tn=cmp_cols, n_out=cmp_cols,
                              col0=COL_K_CMP * HEAD_DIM, name="in_proj_cmp").reshape(b, s, cmp_cols)
        pos = jnp.stack([cmp_pos_k[l], cmp_pos_v[l]]).reshape(2, 2, CMP_STRIDE, HEAD_DIM)
        w1 = jnp.stack([cmp_k_w1[l], cmp_v_w1[l]]).astype(BF16).reshape(2, 2, CMP_STRIDE, HEAD_DIM, HEAD_DIM)
        w2 = jnp.stack([cmp_k_w2[l], cmp_v_w2[l]]).astype(BF16)
        kvc = _compress(kv_cmp, pos, w1, w2)

        nsa = _nsa_attention(proj, kvc,
                             logits.reshape(b, s, 2 * d + GATE_PAD), tq=512)

        h = _merge_out(sb.reshape(m, -1), nsa.reshape(m, -1), logits, h,
                       w_branch_sb[l].astype(BF16), w_branch_nsa[l].astype(BF16), w_out[l].astype(BF16),
                       mix_post_g[l][None], tm=256)

        h = _ffn(h, ffn2_pre_g[l][None], ffn2_w_in[l].astype(BF16), ffn2_w_out[l].astype(BF16),
                 ffn2_post_g[l][None], tm=512, tf=512)
    return h.reshape(b, s, d)
```

```python
import functools

import numpy as np
import jax
import jax.numpy as jnp
from jax import lax
from jax.experimental import pallas as pl
from jax.experimental.pallas import tpu as pltpu

HEAD_DIM = 128
SB_HEADS = 8
NSA_HEADS = 8
NSA_GROUPS = 2
NSA_HPG = NSA_HEADS // NSA_GROUPS
CMP_BLOCK = 32
CMP_STRIDE = 16
SLC_BLOCK = 64
N_SELECT = 16
WINDOW = 512
NORM_EPS = 1e-6
NEG_INF = -1e30
SCORE_SCALE = 2.0 ** 64
FORCED_SCORE = 1e30

QKV_COLS = (3 * SB_HEADS + NSA_HEADS + 6 * NSA_GROUPS) * HEAD_DIM
COL_Q_SB = 0
COL_K_SB = SB_HEADS
COL_V_SB = 2 * SB_HEADS
COL_Q_NSA = 3 * SB_HEADS
COL_K_CMP = COL_Q_NSA + NSA_HEADS
COL_K_SLC = COL_K_CMP + 2 * NSA_GROUPS
COL_V_SLC = COL_K_SLC + NSA_GROUPS
COL_K_WIN = COL_V_SLC + NSA_GROUPS
COL_V_WIN = COL_K_WIN + NSA_GROUPS
N_GATE_LOGITS = 3 * NSA_HEADS
GATE_PAD = 128

LANES = 128
V7X_VMEM_LIMIT = 56 * 1024 * 1024

F32 = jnp.float32
BF16 = jnp.bfloat16
NT_DIMS = (((1,), (1,)), ((), ()))


def _params(semantics):
    return pltpu.CompilerParams(dimension_semantics=semantics,
                                vmem_limit_bytes=V7X_VMEM_LIMIT)


def _rms(x, g):
    ms = jnp.mean(x * x, axis=-1, keepdims=True)
    return x * lax.rsqrt(ms + NORM_EPS) * g


def _ffn_body(x_ref, gpre_ref, wg_ref, wu_ref, wo_ref, gpost_ref, o_ref, h_ref, acc_ref):
    j = pl.program_id(1)

    @pl.when(j == 0)
    def _():
        h_ref[...] = _rms(x_ref[...], gpre_ref[...]).astype(BF16)
        acc_ref[...] = jnp.zeros_like(acc_ref)

    h = h_ref[...]
    gate = jnp.dot(h, wg_ref[...], preferred_element_type=F32)
    up = jnp.dot(h, wu_ref[...], preferred_element_type=F32)
    act = (gate * jax.nn.sigmoid(gate)) * up
    acc_ref[...] += jnp.dot(act.astype(BF16), wo_ref[...], preferred_element_type=F32)

    @pl.when(j == pl.num_programs(1) - 1)
    def _():
        o_ref[...] = x_ref[...] + 0.5 * _rms(acc_ref[...], gpost_ref[...])


def _ffn(x, g_pre, w_in, w_out, g_post, *, tm, tf):
    m, d = x.shape
    f = w_out.shape[0]
    nf = f // tf
    assert m % tm == 0 and f % tf == 0 and w_in.shape == (d, 2 * f)
    return pl.pallas_call(
        _ffn_body,
        name="ffn",
        grid=(m // tm, nf),
        in_specs=[
            pl.BlockSpec((tm, d), lambda i, j: (i, 0)),
            pl.BlockSpec((1, d), lambda i, j: (0, 0)),
            pl.BlockSpec((d, tf), lambda i, j: (0, j)),
            pl.BlockSpec((d, tf), lambda i, j: (0, j + nf)),
            pl.BlockSpec((tf, d), lambda i, j: (j, 0)),
            pl.BlockSpec((1, d), lambda i, j: (0, 0)),
        ],
        out_specs=pl.BlockSpec((tm, d), lambda i, j: (i, 0)),
        out_shape=jax.ShapeDtypeStruct((m, d), F32),
        scratch_shapes=[pltpu.VMEM((tm, d), BF16), pltpu.VMEM((tm, d), F32)],
        compiler_params=_params(("parallel", "arbitrary")),
    )(x, g_pre, w_in, w_in, w_out, g_post)


def _norm_matmul_body(x_ref, g_ref, wt_ref, o_ref, h_ref):
    @pl.when(pl.program_id(1) == 0)
    def _():
        h_ref[...] = _rms(x_ref[...], g_ref[...]).astype(BF16)

    o_ref[...] = lax.dot_general(h_ref[...], wt_ref[...], NT_DIMS,
                                 preferred_element_type=F32).astype(o_ref.dtype)


MAX_COL_TILE = 1536


def _col_tile(n):
    assert n % LANES == 0
    blocks = n // LANES
    best = max(k for k in range(1, MAX_COL_TILE // LANES + 1) if blocks % k == 0)
    return best * LANES


def _norm_matmul(x, g, wt, out_dtype, *, tm, tn, n_out, name, col0=0):
    m, d = x.shape
    n = n_out
    assert m % tm == 0 and n % tn == 0 and col0 % tn == 0 and col0 + n <= wt.shape[0]
    jb = col0 // tn
    return pl.pallas_call(
        _norm_matmul_body,
        name=name,
        grid=(m // tm, n // tn),
        in_specs=[
            pl.BlockSpec((tm, d), lambda i, j: (i, 0)),
            pl.BlockSpec((1, d), lambda i, j: (0, 0)),
            pl.BlockSpec((tn, d), lambda i, j: (j + jb, 0)),
        ],
        out_specs=pl.BlockSpec((tm, tn), lambda i, j: (i, j)),
        out_shape=jax.ShapeDtypeStruct((m, n), out_dtype),
        scratch_shapes=[pltpu.VMEM((tm, d), BF16)],
        compiler_params=_params(("parallel", "arbitrary")),
    )(x, g, wt)


EXP_UNDERFLOW = 104.0
TN_DIMS = (((0,), (0,)), ((), ()))
LOG2_E = 1.4426950408889634


def _sb_body(q_ref, k_ref, v_ref, tri_ref, o_ref, c_ref, acc_ref, *, tq, heads):
    i = pl.program_id(2)
    wide = heads * tq
    tri = tri_ref[...]
    key_i = lax.broadcasted_iota(jnp.int32, (tq, wide), 0)
    qry_i = lax.rem(lax.broadcasted_iota(jnp.int32, (tq, wide), 1), tq)
    past = key_i < qry_i
    head_lanes = [slice(h * HEAD_DIM, (h + 1) * HEAD_DIM) for h in range(heads)]

    def tile(j, diag):
        k0 = pl.multiple_of(j * tq, tq)
        z = jnp.concatenate(
            [lax.dot_general(k_ref[0, pl.ds(k0, tq), hl], q_ref[0, :, hl], NT_DIMS,
                             preferred_element_type=F32) for hl in head_lanes], axis=1)
        softplus = jnp.maximum(z, 0.0) + jnp.log(1.0 + jnp.exp2(jnp.abs(z) * (-LOG2_E)))
        log_beta = z - softplus
        if diag:
            softplus = jnp.where(past, softplus, 0.0)
        hi = softplus.astype(BF16)
        lo = (softplus - hi.astype(F32)).astype(BF16)
        later = jnp.dot(tri, jnp.concatenate([hi, lo], axis=0), preferred_element_type=F32)
        c = c_ref[...]
        w = jnp.exp(log_beta - c - later)
        if diag:
            w = jnp.where(past, w, 0.0)
        w = w.astype(BF16)
        for h, hl in enumerate(head_lanes):
            cols = slice(h * tq, (h + 1) * tq)
            acc_ref[:, cols] += lax.dot_general(v_ref[0, pl.ds(k0, tq), hl], w[:, cols], TN_DIMS,
                                                preferred_element_type=F32)
        c = c + jnp.sum(softplus, axis=0, keepdims=True)
        c_ref[...] = c
        return (jnp.min(c) <= EXP_UNDERFLOW).astype(jnp.int32)

    c_ref[...] = jnp.zeros_like(c_ref)
    acc_ref[...] = jnp.zeros_like(acc_ref)
    alive = tile(i, True)

    def keep_going(carry):
        j, alive = carry
        return jnp.logical_and(j >= 0, alive > 0)

    lax.while_loop(keep_going, lambda carry: (carry[0] - 1, tile(carry[0], False)), (i - 1, alive))
    for h, hl in enumerate(head_lanes):
        o_ref[0, :, hl] = acc_ref[:, h * tq:(h + 1) * tq].T.astype(o_ref.dtype)


def _sb_attention(proj, *, tq, heads):
    b, s, _ = proj.shape
    assert s % tq == 0 and SB_HEADS % heads == 0
    upper = np.triu(np.ones((tq, tq), np.float32), 1)
    tri = jnp.asarray(np.concatenate([upper, upper], axis=1), BF16)
    width = heads * HEAD_DIM
    return pl.pallas_call(
        functools.partial(_sb_body, tq=tq, heads=heads),
        name="sb_attn",
        grid=(b, SB_HEADS // heads, s // tq),
        in_specs=[
            pl.BlockSpec((1, tq, width), lambda bb, h, i: (bb, i, COL_Q_SB // heads + h)),
            pl.BlockSpec((1, s, width), lambda bb, h, i: (bb, 0, COL_K_SB // heads + h)),
            pl.BlockSpec((1, s, width), lambda bb, h, i: (bb, 0, COL_V_SB // heads + h)),
            pl.BlockSpec((tq, 2 * tq), lambda bb, h, i: (0, 0)),
        ],
        out_specs=pl.BlockSpec((1, tq, width), lambda bb, h, i: (bb, i, h)),
        out_shape=jax.ShapeDtypeStruct((b, s, SB_HEADS * HEAD_DIM), BF16),
        scratch_shapes=[pltpu.VMEM((1, heads * tq), F32), pltpu.VMEM((HEAD_DIM, heads * tq), F32)],
        compiler_params=_params(("parallel", "parallel", "arbitrary")),
    )(proj, proj, proj, tri)


def _compress_body(x_ref, pos_ref, w1_ref, w2_ref, o_ref):
    n_chunk = o_ref.shape[2]
    first = jnp.zeros((n_chunk, HEAD_DIM), F32)
    second = jnp.zeros((n_chunk, HEAD_DIM), F32)
    for l in range(CMP_STRIDE):
        x_l = x_ref[0, pl.ds(l, n_chunk, stride=CMP_STRIDE), :]
        xa = (x_l + pos_ref[0, 0, l:l + 1, :]).astype(BF16)
        xb = (x_l + pos_ref[0, 1, l:l + 1, :]).astype(BF16)
        first = first + jnp.dot(xa, w1_ref[0, 0, l], preferred_element_type=F32)
        second = second + jnp.dot(xb, w1_ref[0, 1, l], preferred_element_type=F32)
    pre = first + pltpu.roll(second, n_chunk - 1, 0)
    y = jax.nn.gelu(pre).astype(BF16)
    o_ref[0, 0] = jnp.dot(y, w2_ref[0], preferred_element_type=F32).astype(o_ref.dtype)


def _compress(x, pos, w1, w2):
    b, s, _ = x.shape
    n_chunk = s // CMP_STRIDE
    return pl.pallas_call(
        _compress_body,
        name="compress",
        grid=(2, b * NSA_GROUPS),
        in_specs=[
            pl.BlockSpec((1, s, HEAD_DIM), lambda a, n: (n // NSA_GROUPS, 0, a * NSA_GROUPS + n % NSA_GROUPS)),
            pl.BlockSpec((1, 2, CMP_STRIDE, HEAD_DIM), lambda a, n: (a, 0, 0, 0)),
            pl.BlockSpec((1, 2, CMP_STRIDE, HEAD_DIM, HEAD_DIM), lambda a, n: (a, 0, 0, 0, 0)),
            pl.BlockSpec((1, HEAD_DIM, HEAD_DIM), lambda a, n: (a, 0, 0)),
        ],
        out_specs=pl.BlockSpec((1, 1, n_chunk, HEAD_DIM), lambda a, n: (a, n, 0, 0)),
        out_shape=jax.ShapeDtypeStruct((2, b * NSA_GROUPS, n_chunk, HEAD_DIM), BF16),
        compiler_params=_params(("parallel", "parallel")),
    )(x, pos, w1, w2)


POS_HI_COL = 0
POS_LO_COL = 1
BLOCK_COL0 = HEAD_DIM // 2
SLAB_PARTS_PER_HEAD = 2


def _key_extra_columns(s, tk):
    pos = np.arange(s)
    extra = np.zeros((s, HEAD_DIM), np.float32)
    extra[:, POS_HI_COL] = (pos % tk) // SLC_BLOCK
    extra[:, POS_LO_COL] = pos % SLC_BLOCK
    extra[pos, BLOCK_COL0 + pos // SLC_BLOCK] = 1.0
    return jnp.asarray(extra, BF16)


def _nsa_body(q_ref, kc_ref, vc_ref, ks_ref, vs_ref, kw_ref, vw_ref, gate_ref,
              pool_ref, kextra_ref, o_ref, ksa_ref, kwa_ref, m_ref, l_ref, acc_ref, *, tq, n_slc):
    g = pl.program_id(1)
    i = pl.program_id(2)
    q0 = i * tq
    n_cmp_pad = kc_ref.shape[2]

    @pl.when(i == 0)
    def _():
        ksa_ref[:, :HEAD_DIM] = ks_ref[0]
        ksa_ref[:, HEAD_DIM:] = kextra_ref[...]
        kwa_ref[:, :HEAD_DIM] = kw_ref[0]
        kwa_ref[:, HEAD_DIM:] = kextra_ref[...]

    slopes = [jnp.where(g == 0, 2.0 ** -(r + 1), 2.0 ** -(r + 1 + NSA_HPG)) for r in range(NSA_HPG)]
    q_heads = [q_ref[0, :, r * HEAD_DIM:(r + 1) * HEAD_DIM] for r in range(NSA_HPG)]
    t_row = q0 + lax.broadcasted_iota(jnp.int32, (1, tq), 1)

    wide = NSA_HPG * tq
    head_of_lane = lax.broadcasted_iota(jnp.int32, (1, wide), 1) // tq
    slope_row = jnp.zeros((1, wide), F32)
    for r in range(NSA_HPG):
        slope_row = jnp.where(head_of_lane == r, slopes[r], slope_row)

    def split_heads(x):
        return [x[:, r * tq:(r + 1) * tq] for r in range(NSA_HPG)]

    kc = kc_ref[0, 0]
    vc = vc_ref[0, 0]
    c_idx = lax.broadcasted_iota(jnp.int32, (n_cmp_pad, wide), 0)
    t_wide = q0 + lax.rem(lax.broadcasted_iota(jnp.int32, (1, wide), 1), tq)
    cmp_end = c_idx * CMP_STRIDE + (CMP_BLOCK - 1)
    valid_c = jnp.logical_and(t_wide >= cmp_end, c_idx < n_cmp_pad - 1)
    s = lax.dot_general(kc, jnp.concatenate(q_heads, axis=0), NT_DIMS, preferred_element_type=F32)
    s = jnp.where(valid_c, s + slope_row * (cmp_end - q0).astype(F32), NEG_INF)
    m = jnp.max(s, axis=0, keepdims=True)
    p = jnp.where(valid_c, jnp.exp(s - m), 0.0)
    l = jnp.sum(p, axis=0, keepdims=True)
    p = p * (1.0 / jnp.where(l > 0.0, l, 1.0))
    o_cmp = split_heads(lax.dot_general(vc, p.astype(BF16), TN_DIMS,
                                        preferred_element_type=F32))
    p_grp = functools.reduce(lambda a, b: a + b, split_heads(p))

    pool = pool_ref[...]
    p_grp = p_grp * SCORE_SCALE
    p1 = p_grp.astype(BF16)
    r1 = p_grp - p1.astype(F32)
    p2 = r1.astype(BF16)
    p3 = (r1 - p2.astype(F32)).astype(BF16)
    score = (jnp.dot(pool, p1, preferred_element_type=F32)
             + jnp.dot(pool, p2, preferred_element_type=F32)
             + jnp.dot(pool, p3, preferred_element_type=F32))
    blk = lax.broadcasted_iota(jnp.int32, (n_slc, tq), 0)
    cur = t_row // SLC_BLOCK
    forced = jnp.logical_or(blk == 0, jnp.logical_or(blk == cur, blk == cur - 1))
    score = jnp.where(forced, FORCED_SCORE, score)
    score = jnp.where(blk <= cur, score, NEG_INF)

    sub = 8
    sub_i = lax.broadcasted_iota(jnp.int32, (sub, tq), 0)
    groups = [score[a:a + sub, :] for a in range(0, n_slc, sub)]
    ranks = [jnp.zeros((sub, tq), F32) for _ in groups]
    for ii in range(n_slc):
        row = score[ii:ii + 1, :]
        for gi, grp in enumerate(groups):
            ge = jnp.where(row >= grp, 1.0, 0.0)
            gt = jnp.where(row > grp, 1.0, 0.0)
            if ii < gi * sub:
                beats = ge
            elif ii >= (gi + 1) * sub:
                beats = gt
            else:
                beats = jnp.where(sub_i > ii - gi * sub, ge, gt)
            ranks[gi] = ranks[gi] + beats
    rank = jnp.concatenate(ranks, axis=0)
    sel_bias_t = jnp.where(rank < float(N_SELECT), 0.0, NEG_INF)

    pieces = [jnp.zeros((BLOCK_COL0, tq), F32), sel_bias_t]
    if BLOCK_COL0 + n_slc < HEAD_DIM:
        pieces.append(jnp.zeros((HEAD_DIM - BLOCK_COL0 - n_slc, tq), F32))
    sel_extra = jnp.concatenate(pieces, axis=0).T
    lane = lax.broadcasted_iota(jnp.int32, (tq, HEAD_DIM), 1)

    def query_slab(r, extra):
        pos_cols = jnp.where(lane == POS_HI_COL, slopes[r] * SLC_BLOCK,
                             jnp.where(lane == POS_LO_COL, slopes[r], extra))
        return jnp.concatenate([q_heads[r], pos_cols.astype(BF16)], axis=1)

    q_slc = jnp.concatenate([query_slab(r, sel_extra) for r in range(NSA_HPG)], axis=0)
    q_win = jnp.concatenate([query_slab(r, 0.0) for r in range(NSA_HPG)], axis=0)

    part_w = tq // SLAB_PARTS_PER_HEAD
    parts = [(r * tq + a * part_w, a) for r in range(NSA_HPG) for a in range(SLAB_PARTS_PER_HEAD)]

    def key_rows(d, a, limit):
        lo = max(0, d * tq + a * part_w - limit)
        hi = min(tq - 1, d * tq + (a + 1) * part_w - 1)
        return (lo // part_w) * part_w, (hi // part_w + 1) * part_w

    def distance_mask(d, a, limit, rows):
        lo, hi = rows
        dist_min = d * tq + a * part_w - (hi - 1)
        dist_max = d * tq + (a + 1) * part_w - 1 - lo
        if dist_min >= 0 and dist_max <= limit:
            return None
        key = lo + lax.broadcasted_iota(jnp.int32, (hi - lo, part_w), 0)
        qry = a * part_w + lax.broadcasted_iota(jnp.int32, (hi - lo, part_w), 1)
        dist = d * tq + qry - key
        return jnp.logical_and(dist >= 0, dist <= limit)

    def scores(ka_ref, q_all, j, rows_of):
        k0 = pl.multiple_of(j * tq, tq)
        out = []
        for c0, a in parts:
            lo, hi = rows_of(a)
            ka = ka_ref[pl.ds(k0 + lo, hi - lo), :]
            out.append(lax.dot_general(ka, q_all[c0:c0 + part_w, :], NT_DIMS, preferred_element_type=F32))
        return out

    def reset():
        m_ref[...] = jnp.full((1, wide), NEG_INF, F32)
        l_ref[...] = jnp.zeros((1, wide), F32)
        acc_ref[...] = jnp.zeros((HEAD_DIM, wide), F32)

    def absorb(v_ref, s_parts, j, rows_of, mask_of, live=None):
        k0 = pl.multiple_of(j * tq, tq)
        tile_off = (k0 - q0).astype(F32)
        for (c0, a), s in zip(parts, s_parts):
            cols = slice(c0, c0 + part_w)
            lo, hi = rows_of(a)
            mask = mask_of(a)
            if mask is not None:
                s = jnp.where(mask, s, NEG_INF)
            if live is not None:
                s = jnp.where(live, s, NEG_INF)
            off = slope_row[:, cols] * tile_off
            m = m_ref[:, cols]
            m_new = jnp.maximum(m, jnp.max(s, axis=0, keepdims=True) + off)
            alpha = jnp.exp(m - m_new)
            p = jnp.exp(s - (m_new - off))
            m_ref[:, cols] = m_new
            l_ref[:, cols] = alpha * l_ref[:, cols] + jnp.sum(p, axis=0, keepdims=True)
            acc_ref[:, cols] = alpha * acc_ref[:, cols] + lax.dot_general(
                v_ref[0, pl.ds(k0 + lo, hi - lo), :], p.astype(BF16), TN_DIMS, preferred_element_type=F32)

    def finish():
        return split_heads(acc_ref[...] * (1.0 / l_ref[...]))

    def tile_plan(d, limit):
        rows_of = lambda a: key_rows(d, a, limit)
        return rows_of, (lambda a: distance_mask(d, a, limit, rows_of(a)))

    all_rows = lambda a: (0, tq)
    no_mask = lambda a: None

    reset()
    rows_of, mask_of = tile_plan(0, tq)
    absorb(vs_ref, scores(ksa_ref, q_slc, i, rows_of), i, rows_of, mask_of)

    @pl.loop(0, i)
    def _(j):
        absorb(vs_ref, scores(ksa_ref, q_slc, j, all_rows), j, all_rows, no_mask)

    o_slc = finish()

    reset()
    for d in range((WINDOW - 1) // tq + 2):
        rows_of, mask_of = tile_plan(d, WINDOW - 1)
        j = jnp.maximum(i - d, 0)
        absorb(vw_ref, scores(kwa_ref, q_win, j, rows_of), j, rows_of, mask_of,
               live=None if d == 0 else i - d >= 0)
    o_win = finish()

    sg_t = jax.nn.sigmoid(gate_ref[0]).T

    def gate(branch, r):
        c0 = branch * NSA_HEADS + r
        c1 = c0 + NSA_HPG
        return jnp.where(g == 0, sg_t[c0:c0 + 1, :], sg_t[c1:c1 + 1, :])

    for r in range(NSA_HPG):
        out_t = gate(0, r) * o_cmp[r] + gate(1, r) * o_slc[r] + gate(2, r) * o_win[r]
        o_ref[0, :, r * HEAD_DIM:(r + 1) * HEAD_DIM] = out_t.T.astype(o_ref.dtype)


def _nsa_attention(proj, kvc, gm, *, tq):
    b, s, _ = proj.shape
    n_cmp_pad = s // CMP_STRIDE
    n_slc = s // SLC_BLOCK
    assert s % tq == 0 and tq % SLC_BLOCK == 0 and n_cmp_pad % LANES == 0
    assert BLOCK_COL0 + n_slc <= HEAD_DIM and tq // SLC_BLOCK <= 256 and n_slc % 8 == 0
    ratio = SLC_BLOCK // CMP_STRIDE
    span = CMP_BLOCK // CMP_STRIDE
    pool = np.zeros((n_slc, n_cmp_pad), np.float32)
    for jj in range(n_slc):
        for mm in range(ratio):
            for nn in range(span):
                c = ratio * jj + mm + nn
                if c < n_cmp_pad:
                    pool[jj, c] += 1.0
    grp_w = NSA_HPG * HEAD_DIM
    kv_spec = lambda col: pl.BlockSpec((1, s, HEAD_DIM), lambda bb, g, i: (bb, 0, col + g))
    cmp_spec = lambda a: pl.BlockSpec((1, 1, n_cmp_pad, HEAD_DIM),
                                      lambda bb, g, i: (a, bb * NSA_GROUPS + g, 0, 0))
    return pl.pallas_call(
        functools.partial(_nsa_body, tq=tq, n_slc=n_slc),
        name="nsa_attn",
        grid=(b, NSA_GROUPS, s // tq),
        in_specs=[
            pl.BlockSpec((1, tq, grp_w), lambda bb, g, i: (bb, i, COL_Q_NSA // NSA_HPG + g)),
            cmp_spec(0), cmp_spec(1),
            kv_spec(COL_K_SLC), kv_spec(COL_V_SLC), kv_spec(COL_K_WIN), kv_spec(COL_V_WIN),
            pl.BlockSpec((1, tq, GATE_PAD), lambda bb, g, i: (bb, i, gm.shape[2] // GATE_PAD - 1)),
            pl.BlockSpec((n_slc, n_cmp_pad), lambda bb, g, i: (0, 0)),
            pl.BlockSpec((s, HEAD_DIM), lambda bb, g, i: (0, 0)),
        ],
        out_specs=pl.BlockSpec((1, tq, grp_w), lambda bb, g, i: (bb, i, g)),
        out_shape=jax.ShapeDtypeStruct((b, s, NSA_HEADS * HEAD_DIM), BF16),
        scratch_shapes=[pltpu.VMEM((s, 2 * HEAD_DIM), BF16), pltpu.VMEM((s, 2 * HEAD_DIM), BF16),
                        pltpu.VMEM((1, NSA_HPG * tq), F32), pltpu.VMEM((1, NSA_HPG * tq), F32),
                        pltpu.VMEM((HEAD_DIM, NSA_HPG * tq), F32)],
        compiler_params=_params(("parallel", "parallel", "arbitrary")),
    )(proj, kvc, kvc, proj, proj, proj, proj, gm, jnp.asarray(pool, BF16), _key_extra_columns(s, tq))


def _merge_body(sb_ref, nsa_ref, m0_ref, m1_ref, x_ref, wsb_ref, wnsa_ref, wout_ref, g_ref, o_ref):
    y_sb = jnp.dot(sb_ref[...], wsb_ref[...], preferred_element_type=F32)
    y_nsa = jnp.dot(nsa_ref[...], wnsa_ref[...], preferred_element_type=F32)
    merged = jax.nn.sigmoid(m0_ref[...]) * y_sb + jax.nn.sigmoid(m1_ref[...]) * y_nsa
    y = jnp.dot(merged.astype(BF16), wout_ref[...], preferred_element_type=F32)
    o_ref[...] = x_ref[...] + _rms(y, g_ref[...])


def _merge_out(sb, nsa, gm, x, w_sb, w_nsa, w_out, g_post, *, tm):
    m, d = x.shape
    assert m % tm == 0 and gm.shape[0] == m and gm.shape[1] >= 2 * d

    def resident(shape):
        return pl.BlockSpec(shape, lambda i: (0, 0), pipeline_mode=pl.Buffered(1))

    return pl.pallas_call(
        _merge_body,
        name="merge_out",
        grid=(m // tm,),
        in_specs=[
            pl.BlockSpec((tm, sb.shape[1]), lambda i: (i, 0)),
            pl.BlockSpec((tm, nsa.shape[1]), lambda i: (i, 0)),
            pl.BlockSpec((tm, d), lambda i: (i, 0)),
            pl.BlockSpec((tm, d), lambda i: (i, 1)),
            pl.BlockSpec((tm, d), lambda i: (i, 0)),
            resident(w_sb.shape), resident(w_nsa.shape), resident(w_out.shape),
            pl.BlockSpec((1, d), lambda i: (0, 0)),
        ],
        out_specs=pl.BlockSpec((tm, d), lambda i: (i, 0)),
        out_shape=jax.ShapeDtypeStruct((m, d), F32),
        compiler_params=_params(("parallel",)),
    )(sb, nsa, gm, gm, x, w_sb, w_nsa, w_out, g_post)


def kernel(x, ffn1_pre_g, ffn1_w_in, ffn1_w_out, ffn1_post_g, mix_pre_g, w_in, cmp_pos_k, cmp_k_w1, cmp_k_w2, cmp_pos_v, cmp_v_w1, cmp_v_w2, w_branch_sb, w_branch_nsa, w_out, mix_post_g, ffn2_pre_g, ffn2_w_in, ffn2_w_out, ffn2_post_g):
    b, s, d = x.shape
    m = b * s
    depth = ffn1_pre_g.shape[0]
    h = x.reshape(m, d)
    for l in range(depth):
        h = _ffn(h, ffn1_pre_g[l][None], ffn1_w_in[l].astype(BF16), ffn1_w_out[l].astype(BF16),
                 ffn1_post_g[l][None], tm=512, tf=512)

        g_mix = mix_pre_g[l][None]
        col_scale = np.ones((w_in.shape[2],), np.float32)
        col_scale[COL_Q_SB * HEAD_DIM:(COL_Q_SB + SB_HEADS) * HEAD_DIM] = HEAD_DIM ** -0.5
        col_scale[COL_Q_NSA * HEAD_DIM:(COL_Q_NSA + NSA_HEADS) * HEAD_DIM] = HEAD_DIM ** -0.5
        wt_bf = (jnp.transpose(w_in[l]) * jnp.asarray(col_scale)[:, None]).astype(BF16)
        gate_end = QKV_COLS + N_GATE_LOGITS
        wt_logits = jnp.concatenate([wt_bf[gate_end:], wt_bf[QKV_COLS:gate_end],
                                     jnp.zeros((GATE_PAD - N_GATE_LOGITS, d), BF16)], axis=0)
        proj = _norm_matmul(h, g_mix, wt_bf, BF16, tm=1024, tn=_col_tile(QKV_COLS), n_out=QKV_COLS,
                            name="in_proj_qkv")
        logits = _norm_matmul(h, g_mix, wt_logits, F32, tm=1024, tn=_col_tile(2 * d + GATE_PAD),
                              n_out=2 * d + GATE_PAD, name="in_proj_logits")
        proj = proj.reshape(b, s, QKV_COLS)

        sb = _sb_attention(proj, tq=256, heads=8)

        cmp_cols = (COL_K_SLC - COL_K_CMP) * HEAD_DIM
        kv_cmp = _norm_matmul(h, g_mix, wt_bf, F32, tm=1024, tn=cmp_cols, n_out=cmp_cols,
                              col0=COL_K_CMP * HEAD_DIM, name="in_proj_cmp").reshape(b, s, cmp_cols)
        pos = jnp.stack([cmp_pos_k[l], cmp_pos_v[l]]).reshape(2, 2, CMP_STRIDE, HEAD_DIM)
        w1 = jnp.stack([cmp_k_w1[l], cmp_v_w1[l]]).astype(BF16).reshape(2, 2, CMP_STRIDE, HEAD_DIM, HEAD_DIM)
        w2 = jnp.stack([cmp_k_w2[l], cmp_v_w2[l]]).astype(BF16)
        kvc = _compress(kv_cmp, pos, w1, w2)

        nsa = _nsa_attention(proj, kvc,
                             logits.reshape(b, s, 2 * d + GATE_PAD), tq=512)

        h = _merge_out(sb.reshape(m, -1), nsa.reshape(m, -1), logits, h,
                       w_branch_sb[l].astype(BF16), w_branch_nsa[l].astype(BF16), w_out[l].astype(BF16),
                       mix_post_g[l][None], tm=256)

        h = _ffn(h, ffn2_pre_g[l][None], ffn2_w_in[l].astype(BF16), ffn2_w_out[l].astype(BF16),
                 ffn2_post_g[l][None], tm=512, tf=512)
    return h.reshape(b, s, d)
```

```python
import functools

import numpy as np
import jax
import jax.numpy as jnp
from jax import lax
from jax.experimental import pallas as pl
from jax.experimental.pallas import tpu as pltpu

HEAD_DIM = 128
SB_HEADS = 8
NSA_HEADS = 8
NSA_GROUPS = 2
NSA_HPG = NSA_HEADS // NSA_GROUPS
CMP_BLOCK = 32
CMP_STRIDE = 16
SLC_BLOCK = 64
N_SELECT = 16
WINDOW = 512
NORM_EPS = 1e-6
NEG_INF = -1e30
SCORE_SCALE = 2.0 ** 64
FORCED_SCORE = 1e30

QKV_COLS = (3 * SB_HEADS + NSA_HEADS + 6 * NSA_GROUPS) * HEAD_DIM
COL_Q_SB = 0
COL_K_SB = SB_HEADS
COL_V_SB = 2 * SB_HEADS
COL_Q_NSA = 3 * SB_HEADS
COL_K_CMP = COL_Q_NSA + NSA_HEADS
COL_K_SLC = COL_K_CMP + 2 * NSA_GROUPS
COL_V_SLC = COL_K_SLC + NSA_GROUPS
COL_K_WIN = COL_V_SLC + NSA_GROUPS
COL_V_WIN = COL_K_WIN + NSA_GROUPS
N_GATE_LOGITS = 3 * NSA_HEADS
GATE_PAD = 128

LANES = 128
V7X_VMEM_LIMIT = 56 * 1024 * 1024
FFN_CHUNK = 1024

F32 = jnp.float32
BF16 = jnp.bfloat16
NT_DIMS = (((1,), (1,)), ((), ()))


def _params(semantics):
    return pltpu.CompilerParams(dimension_semantics=semantics,
                                vmem_limit_bytes=V7X_VMEM_LIMIT)


def _rms(x, g):
    ms = jnp.mean(x * x, axis=-1, keepdims=True)
    return x * lax.rsqrt(ms + NORM_EPS) * g


def _ffn_body(x_ref, gpre_ref, wg_ref, wu_ref, wo_ref, gpost_ref, o_ref, h_ref, acc_ref, *, tail):
    j = pl.program_id(1)
    last = pl.num_programs(1) - 1
    tf = wg_ref.shape[1]

    @pl.when(j == 0)
    def _():
        h_ref[...] = _rms(x_ref[...], gpre_ref[...]).astype(BF16)
        acc_ref[...] = jnp.zeros_like(acc_ref)

    def chunk(width):
        h = h_ref[...]
        gate = jnp.dot(h, wg_ref[:, :width], preferred_element_type=F32)
        up = jnp.dot(h, wu_ref[:, :width], preferred_element_type=F32)
        act = (gate * jax.nn.sigmoid(gate)) * up
        acc_ref[...] += jnp.dot(act.astype(BF16), wo_ref[:width, :], preferred_element_type=F32)

    if tail == tf:
        chunk(tf)
    else:
        pl.when(j < last)(lambda: chunk(tf))
        pl.when(j == last)(lambda: chunk(tail))

    @pl.when(j == last)
    def _():
        o_ref[...] = x_ref[...] + 0.5 * _rms(acc_ref[...], gpost_ref[...])


def _ffn(x, g_pre, w_gate, w_up, w_out, g_post, *, tm, tf):
    m, d = x.shape
    f = w_out.shape[0]
    nf = pl.cdiv(f, tf)
    tail = f - (nf - 1) * tf
    assert m % tm == 0 and tail % LANES == 0 and w_gate.shape == (d, f) and w_up.shape == (d, f)
    return pl.pallas_call(
        functools.partial(_ffn_body, tail=tail),
        name="ffn",
        grid=(m // tm, nf),
        in_specs=[
            pl.BlockSpec((tm, d), lambda i, j: (i, 0)),
            pl.BlockSpec((1, d), lambda i, j: (0, 0)),
            pl.BlockSpec((d, tf), lambda i, j: (0, j)),
            pl.BlockSpec((d, tf), lambda i, j: (0, j)),
            pl.BlockSpec((tf, d), lambda i, j: (j, 0)),
            pl.BlockSpec((1, d), lambda i, j: (0, 0)),
        ],
        out_specs=pl.BlockSpec((tm, d), lambda i, j: (i, 0)),
        out_shape=jax.ShapeDtypeStruct((m, d), F32),
        scratch_shapes=[pltpu.VMEM((tm, d), BF16), pltpu.VMEM((tm, d), F32)],
        compiler_params=_params(("parallel", "arbitrary")),
    )(x, g_pre, w_gate, w_up, w_out, g_post)


def _norm_matmul_body(x_ref, g_ref, wt_ref, o_ref, h_ref):
    @pl.when(pl.program_id(1) == 0)
    def _():
        h_ref[...] = _rms(x_ref[...], g_ref[...]).astype(BF16)

    o_ref[...] = lax.dot_general(h_ref[...], wt_ref[...], NT_DIMS,
                                 preferred_element_type=F32).astype(o_ref.dtype)


MAX_COL_TILE = 1536


def _col_tile(n):
    assert n % LANES == 0
    blocks = n // LANES
    best = max(k for k in range(1, MAX_COL_TILE // LANES + 1) if blocks % k == 0)
    return best * LANES


def _norm_matmul(x, g, wt, out_dtype, *, tm, tn, n_out, name, col0=0):
    m, d = x.shape
    n = n_out
    assert m % tm == 0 and n % tn == 0 and col0 % tn == 0 and col0 + n <= wt.shape[0]
    jb = col0 // tn
    return pl.pallas_call(
        _norm_matmul_body,
        name=name,
        grid=(m // tm, n // tn),
        in_specs=[
            pl.BlockSpec((tm, d), lambda i, j: (i, 0)),
            pl.BlockSpec((1, d), lambda i, j: (0, 0)),
            pl.BlockSpec((tn, d), lambda i, j: (j + jb, 0)),
        ],
        out_specs=pl.BlockSpec((tm, tn), lambda i, j: (i, j)),
        out_shape=jax.ShapeDtypeStruct((m, n), out_dtype),
        scratch_shapes=[pltpu.VMEM((tm, d), BF16)],
        compiler_params=_params(("parallel", "arbitrary")),
    )(x, g, wt)


EXP_UNDERFLOW = 104.0
TN_DIMS = (((0,), (0,)), ((), ()))
LOG2_E = 1.4426950408889634


def _sb_body(q_ref, k_ref, v_ref, tri_ref, o_ref, c_ref, acc_ref, *, tq, heads):
    i = pl.program_id(2)
    wide = heads * tq
    tri = tri_ref[...]
    key_i = lax.broadcasted_iota(jnp.int32, (tq, wide), 0)
    qry_i = lax.rem(lax.broadcasted_iota(jnp.int32, (tq, wide), 1), tq)
    past = key_i < qry_i
    head_lanes = [slice(h * HEAD_DIM, (h + 1) * HEAD_DIM) for h in range(heads)]

    def tile(j, diag):
        k0 = pl.multiple_of(j * tq, tq)
        z = jnp.concatenate(
            [lax.dot_general(k_ref[0, pl.ds(k0, tq), hl], q_ref[0, :, hl], NT_DIMS,
                             preferred_element_type=F32) for hl in head_lanes], axis=1)
        softplus = jnp.maximum(z, 0.0) + jnp.log(1.0 + jnp.exp2(jnp.abs(z) * (-LOG2_E)))
        log_beta = z - softplus
        if diag:
            softplus = jnp.where(past, softplus, 0.0)
        hi = softplus.astype(BF16)
        lo = (softplus - hi.astype(F32)).astype(BF16)
        later = jnp.dot(tri, jnp.concatenate([hi, lo], axis=0), preferred_element_type=F32)
        c = c_ref[...]
        w = jnp.exp(log_beta - c - later)
        if diag:
            w = jnp.where(past, w, 0.0)
        w = w.astype(BF16)
        for h, hl in enumerate(head_lanes):
            cols = slice(h * tq, (h + 1) * tq)
            acc_ref[:, cols] += lax.dot_general(v_ref[0, pl.ds(k0, tq), hl], w[:, cols], TN_DIMS,
                                                preferred_element_type=F32)
        c = c + jnp.sum(softplus, axis=0, keepdims=True)
        c_ref[...] = c
        return (jnp.min(c) <= EXP_UNDERFLOW).astype(jnp.int32)

    c_ref[...] = jnp.zeros_like(c_ref)
    acc_ref[...] = jnp.zeros_like(acc_ref)
    alive = tile(i, True)

    def keep_going(carry):
        j, alive = carry
        return jnp.logical_and(j >= 0, alive > 0)

    lax.while_loop(keep_going, lambda carry: (carry[0] - 1, tile(carry[0], False)), (i - 1, alive))
    for h, hl in enumerate(head_lanes):
        o_ref[0, :, hl] = acc_ref[:, h * tq:(h + 1) * tq].T.astype(o_ref.dtype)


def _sb_attention(proj, *, tq, heads):
    b, s, _ = proj.shape
    assert s % tq == 0 and SB_HEADS % heads == 0
    upper = np.triu(np.ones((tq, tq), np.float32), 1)
    tri = jnp.asarray(np.concatenate([upper, upper], axis=1), BF16)
    width = heads * HEAD_DIM
    return pl.pallas_call(
        functools.partial(_sb_body, tq=tq, heads=heads),
        name="sb_attn",
        grid=(b, SB_HEADS // heads, s // tq),
        in_specs=[
            pl.BlockSpec((1, tq, width), lambda bb, h, i: (bb, i, COL_Q_SB // heads + h)),
            pl.BlockSpec((1, s, width), lambda bb, h, i: (bb, 0, COL_K_SB // heads + h)),
            pl.BlockSpec((1, s, width), lambda bb, h, i: (bb, 0, COL_V_SB // heads + h)),
            pl.BlockSpec((tq, 2 * tq), lambda bb, h, i: (0, 0)),
        ],
        out_specs=pl.BlockSpec((1, tq, width), lambda bb, h, i: (bb, i, h)),
        out_shape=jax.ShapeDtypeStruct((b, s, SB_HEADS * HEAD_DIM), BF16),
        scratch_shapes=[pltpu.VMEM((1, heads * tq), F32), pltpu.VMEM((HEAD_DIM, heads * tq), F32)],
        compiler_params=_params(("parallel", "parallel", "arbitrary")),
    )(proj, proj, proj, tri)


def _compress_body(x_ref, pos_ref, w1_ref, w2_ref, o_ref):
    n_chunk = o_ref.shape[2]
    first = jnp.zeros((n_chunk, HEAD_DIM), F32)
    second = jnp.zeros((n_chunk, HEAD_DIM), F32)
    for l in range(CMP_STRIDE):
        x_l = x_ref[0, pl.ds(l, n_chunk, stride=CMP_STRIDE), :]
        xa = (x_l + pos_ref[0, 0, l:l + 1, :]).astype(BF16)
        xb = (x_l + pos_ref[0, 1, l:l + 1, :]).astype(BF16)
        first = first + jnp.dot(xa, w1_ref[0, 0, l], preferred_element_type=F32)
        second = second + jnp.dot(xb, w1_ref[0, 1, l], preferred_element_type=F32)
    pre = first + pltpu.roll(second, n_chunk - 1, 0)
    y = jax.nn.gelu(pre).astype(BF16)
    o_ref[0, 0] = jnp.dot(y, w2_ref[0], preferred_element_type=F32).astype(o_ref.dtype)


def _compress(x, pos, w1, w2):
    b, s, _ = x.shape
    n_chunk = s // CMP_STRIDE
    return pl.pallas_call(
        _compress_body,
        name="compress",
        grid=(2, b * NSA_GROUPS),
        in_specs=[
            pl.BlockSpec((1, s, HEAD_DIM), lambda a, n: (n // NSA_GROUPS, 0, a * NSA_GROUPS + n % NSA_GROUPS)),
            pl.BlockSpec((1, 2, CMP_STRIDE, HEAD_DIM), lambda a, n: (a, 0, 0, 0)),
            pl.BlockSpec((1, 2, CMP_STRIDE, HEAD_DIM, HEAD_DIM), lambda a, n: (a, 0, 0, 0, 0)),
            pl.BlockSpec((1, HEAD_DIM, HEAD_DIM), lambda a, n: (a, 0, 0)),
        ],
        out_specs=pl.BlockSpec((1, 1, n_chunk, HEAD_DIM), lambda a, n: (a, n, 0, 0)),
        out_shape=jax.ShapeDtypeStruct((2, b * NSA_GROUPS, n_chunk, HEAD_DIM), BF16),
        compiler_params=_params(("parallel", "parallel")),
    )(x, pos, w1, w2)


POS_HI_COL = 0
POS_LO_COL = 1
BLOCK_COL0 = HEAD_DIM // 2
SLAB_PARTS_PER_HEAD = 2


def _key_extra_columns(s, tk):
    pos = np.arange(s)
    extra = np.zeros((s, HEAD_DIM), np.float32)
    extra[:, POS_HI_COL] = (pos % tk) // SLC_BLOCK
    extra[:, POS_LO_COL] = pos % SLC_BLOCK
    extra[pos, BLOCK_COL0 + pos // SLC_BLOCK] = 1.0
    return jnp.asarray(extra, BF16)


def _nsa_body(q_ref, kc_ref, vc_ref, ks_ref, vs_ref, kw_ref, vw_ref, gate_ref,
              pool_ref, kextra_ref, o_ref, ksa_ref, kwa_ref, m_ref, l_ref, acc_ref, *, tq, n_slc):
    g = pl.program_id(1)
    i = pl.program_id(2)
    q0 = i * tq
    n_cmp_pad = kc_ref.shape[2]

    @pl.when(i == 0)
    def _():
        ksa_ref[:, :HEAD_DIM] = ks_ref[0]
        ksa_ref[:, HEAD_DIM:] = kextra_ref[...]
        kwa_ref[:, :HEAD_DIM] = kw_ref[0]
        kwa_ref[:, HEAD_DIM:] = kextra_ref[...]

    slopes = [jnp.where(g == 0, 2.0 ** -(r + 1), 2.0 ** -(r + 1 + NSA_HPG)) for r in range(NSA_HPG)]
    q_heads = [q_ref[0, :, r * HEAD_DIM:(r + 1) * HEAD_DIM] for r in range(NSA_HPG)]
    t_row = q0 + lax.broadcasted_iota(jnp.int32, (1, tq), 1)

    wide = NSA_HPG * tq
    head_of_lane = lax.broadcasted_iota(jnp.int32, (1, wide), 1) // tq
    slope_row = jnp.zeros((1, wide), F32)
    for r in range(NSA_HPG):
        slope_row = jnp.where(head_of_lane == r, slopes[r], slope_row)

    def split_heads(x):
        return [x[:, r * tq:(r + 1) * tq] for r in range(NSA_HPG)]

    kc = kc_ref[0, 0]
    vc = vc_ref[0, 0]
    c_idx = lax.broadcasted_iota(jnp.int32, (n_cmp_pad, wide), 0)
    t_wide = q0 + lax.rem(lax.broadcasted_iota(jnp.int32, (1, wide), 1), tq)
    cmp_end = c_idx * CMP_STRIDE + (CMP_BLOCK - 1)
    valid_c = jnp.logical_and(t_wide >= cmp_end, c_idx < n_cmp_pad - 1)
    s = lax.dot_general(kc, jnp.concatenate(q_heads, axis=0), NT_DIMS, preferred_element_type=F32)
    s = jnp.where(valid_c, s + slope_row * (cmp_end - q0).astype(F32), NEG_INF)
    m = jnp.max(s, axis=0, keepdims=True)
    p = jnp.where(valid_c, jnp.exp(s - m), 0.0)
    l = jnp.sum(p, axis=0, keepdims=True)
    p = p * (1.0 / jnp.where(l > 0.0, l, 1.0))
    o_cmp = split_heads(lax.dot_general(vc, p.astype(BF16), TN_DIMS,
                                        preferred_element_type=F32))
    p_grp = functools.reduce(lambda a, b: a + b, split_heads(p))

    pool = pool_ref[...]
    p_grp = p_grp * SCORE_SCALE
    p1 = p_grp.astype(BF16)
    r1 = p_grp - p1.astype(F32)
    p2 = r1.astype(BF16)
    p3 = (r1 - p2.astype(F32)).astype(BF16)
    score = (jnp.dot(pool, p1, preferred_element_type=F32)
             + jnp.dot(pool, p2, preferred_element_type=F32)
             + jnp.dot(pool, p3, preferred_element_type=F32))
    blk = lax.broadcasted_iota(jnp.int32, (n_slc, tq), 0)
    cur = t_row // SLC_BLOCK
    forced = jnp.logical_or(blk == 0, jnp.logical_or(blk == cur, blk == cur - 1))
    score = jnp.where(forced, FORCED_SCORE, score)
    score = jnp.where(blk <= cur, score, NEG_INF)

    sub = 8
    sub_i = lax.broadcasted_iota(jnp.int32, (sub, tq), 0)
    groups = [score[a:a + sub, :] for a in range(0, n_slc, sub)]
    ranks = [jnp.zeros((sub, tq), F32) for _ in groups]
    for ii in range(n_slc):
        row = score[ii:ii + 1, :]
        for gi, grp in enumerate(groups):
            ge = jnp.where(row >= grp, 1.0, 0.0)
            gt = jnp.where(row > grp, 1.0, 0.0)
            if ii < gi * sub:
                beats = ge
            elif ii >= (gi + 1) * sub:
                beats = gt
            else:
                beats = jnp.where(sub_i > ii - gi * sub, ge, gt)
            ranks[gi] = ranks[gi] + beats
    rank = jnp.concatenate(ranks, axis=0)
    sel_bias_t = jnp.where(rank < float(N_SELECT), 0.0, NEG_INF)

    pieces = [jnp.zeros((BLOCK_COL0, tq), F32), sel_bias_t]
    if BLOCK_COL0 + n_slc < HEAD_DIM:
        pieces.append(jnp.zeros((HEAD_DIM - BLOCK_COL0 - n_slc, tq), F32))
    sel_extra = jnp.concatenate(pieces, axis=0).T
    lane = lax.broadcasted_iota(jnp.int32, (tq, HEAD_DIM), 1)

    def query_slab(r, extra):
        pos_cols = jnp.where(lane == POS_HI_COL, slopes[r] * SLC_BLOCK,
                             jnp.where(lane == POS_LO_COL, slopes[r], extra))
        return jnp.concatenate([q_heads[r], pos_cols.astype(BF16)], axis=1)

    q_slc = jnp.concatenate([query_slab(r, sel_extra) for r in range(NSA_HPG)], axis=0)
    q_win = jnp.concatenate([query_slab(r, 0.0) for r in range(NSA_HPG)], axis=0)

    part_w = tq // SLAB_PARTS_PER_HEAD
    parts = [(r * tq + a * part_w, a) for r in range(NSA_HPG) for a in range(SLAB_PARTS_PER_HEAD)]

    def key_rows(d, a, limit):
        lo = max(0, d * tq + a * part_w - limit)
        hi = min(tq - 1, d * tq + (a + 1) * part_w - 1)
        return (lo // part_w) * part_w, (hi // part_w + 1) * part_w

    def distance_mask(d, a, limit, rows):
        lo, hi = rows
        dist_min = d * tq + a * part_w - (hi - 1)
        dist_max = d * tq + (a + 1) * part_w - 1 - lo
        if dist_min >= 0 and dist_max <= limit:
            return None
        key = lo + lax.broadcasted_iota(jnp.int32, (hi - lo, part_w), 0)
        qry = a * part_w + lax.broadcasted_iota(jnp.int32, (hi - lo, part_w), 1)
        dist = d * tq + qry - key
        return jnp.logical_and(dist >= 0, dist <= limit)

    def scores(ka_ref, q_all, j, rows_of):
        k0 = pl.multiple_of(j * tq, tq)
        out = []
        for c0, a in parts:
            lo, hi = rows_of(a)
            ka = ka_ref[pl.ds(k0 + lo, hi - lo), :]
            out.append(lax.dot_general(ka, q_all[c0:c0 + part_w, :], NT_DIMS, preferred_element_type=F32))
        return out

    def reset():
        m_ref[...] = jnp.full((1, wide), NEG_INF, F32)
        l_ref[...] = jnp.zeros((1, wide), F32)
        acc_ref[...] = jnp.zeros((HEAD_DIM, wide), F32)

    def absorb(v_ref, s_parts, j, rows_of, mask_of, live=None):
        k0 = pl.multiple_of(j * tq, tq)
        tile_off = (k0 - q0).astype(F32)
        v_t = v_ref[0, pl.ds(k0, tq), :].T
        for (c0, a), s in zip(parts, s_parts):
            cols = slice(c0, c0 + part_w)
            lo, hi = rows_of(a)
            mask = mask_of(a)
            if mask is not None:
                s = jnp.where(mask, s, NEG_INF)
            if live is not None:
                s = jnp.where(live, s, NEG_INF)
            off = slope_row[:, cols] * tile_off
            m = m_ref[:, cols]
            m_new = jnp.maximum(m, jnp.max(s, axis=0, keepdims=True) + off)
            alpha = jnp.exp(m - m_new)
            p = jnp.exp(s - (m_new - off))
            m_ref[:, cols] = m_new
            l_ref[:, cols] = alpha * l_ref[:, cols] + jnp.sum(p, axis=0, keepdims=True)
            acc_ref[:, cols] = alpha * acc_ref[:, cols] + jnp.dot(
                v_t[:, lo:hi], p.astype(BF16), preferred_element_type=F32)

    def finish():
        return split_heads(acc_ref[...] * (1.0 / l_ref[...]))

    def tile_plan(d, limit):
        rows_of = lambda a: key_rows(d, a, limit)
        return rows_of, (lambda a: distance_mask(d, a, limit, rows_of(a)))

    all_rows = lambda a: (0, tq)
    no_mask = lambda a: None

    reset()
    rows_of, mask_of = tile_plan(0, tq)
    absorb(vs_ref, scores(ksa_ref, q_slc, i, rows_of), i, rows_of, mask_of)

    @pl.loop(0, i)
    def _(j):
        absorb(vs_ref, scores(ksa_ref, q_slc, j, all_rows), j, all_rows, no_mask)

    o_slc = finish()

    reset()
    for d in range((WINDOW - 1) // tq + 2):
        rows_of, mask_of = tile_plan(d, WINDOW - 1)
        j = jnp.maximum(i - d, 0)
        absorb(vw_ref, scores(kwa_ref, q_win, j, rows_of), j, rows_of, mask_of,
               live=None if d == 0 else i - d >= 0)
    o_win = finish()

    sg_t = jax.nn.sigmoid(gate_ref[0]).T

    def gate(branch, r):
        c0 = branch * NSA_HEADS + r
        c1 = c0 + NSA_HPG
        return jnp.where(g == 0, sg_t[c0:c0 + 1, :], sg_t[c1:c1 + 1, :])

    for r in range(NSA_HPG):
        out_t = gate(0, r) * o_cmp[r] + gate(1, r) * o_slc[r] + gate(2, r) * o_win[r]
        o_ref[0, :, r * HEAD_DIM:(r + 1) * HEAD_DIM] = out_t.T.astype(o_ref.dtype)


def _nsa_attention(proj, kvc, gm, *, tq):
    b, s, _ = proj.shape
    n_cmp_pad = s // CMP_STRIDE
    n_slc = s // SLC_BLOCK
    assert s % tq == 0 and tq % SLC_BLOCK == 0 and n_cmp_pad % LANES == 0
    assert BLOCK_COL0 + n_slc <= HEAD_DIM and tq // SLC_BLOCK <= 256 and n_slc % 8 == 0
    ratio = SLC_BLOCK // CMP_STRIDE
    span = CMP_BLOCK // CMP_STRIDE
    pool = np.zeros((n_slc, n_cmp_pad), np.float32)
    for jj in range(n_slc):
        for mm in range(ratio):
            for nn in range(span):
                c = ratio * jj + mm + nn
                if c < n_cmp_pad:
                    pool[jj, c] += 1.0
    grp_w = NSA_HPG * HEAD_DIM
    kv_spec = lambda col: pl.BlockSpec((1, s, HEAD_DIM), lambda bb, g, i: (bb, 0, col + g))
    cmp_spec = lambda a: pl.BlockSpec((1, 1, n_cmp_pad, HEAD_DIM),
                                      lambda bb, g, i: (a, bb * NSA_GROUPS + g, 0, 0))
    return pl.pallas_call(
        functools.partial(_nsa_body, tq=tq, n_slc=n_slc),
        name="nsa_attn",
        grid=(b, NSA_GROUPS, s // tq),
        in_specs=[
            pl.BlockSpec((1, tq, grp_w), lambda bb, g, i: (bb, i, COL_Q_NSA // NSA_HPG + g)),
            cmp_spec(0), cmp_spec(1),
            kv_spec(COL_K_SLC), kv_spec(COL_V_SLC), kv_spec(COL_K_WIN), kv_spec(COL_V_WIN),
            pl.BlockSpec((1, tq, GATE_PAD), lambda bb, g, i: (bb, i, gm.shape[2] // GATE_PAD - 1)),
            pl.BlockSpec((n_slc, n_cmp_pad), lambda bb, g, i: (0, 0)),
            pl.BlockSpec((s, HEAD_DIM), lambda bb, g, i: (0, 0)),
        ],
        out_specs=pl.BlockSpec((1, tq, grp_w), lambda bb, g, i: (bb, i, g)),
        out_shape=jax.ShapeDtypeStruct((b, s, NSA_HEADS * HEAD_DIM), BF16),
        scratch_shapes=[pltpu.VMEM((s, 2 * HEAD_DIM), BF16), pltpu.VMEM((s, 2 * HEAD_DIM), BF16),
                        pltpu.VMEM((1, NSA_HPG * tq), F32), pltpu.VMEM((1, NSA_HPG * tq), F32),
                        pltpu.VMEM((HEAD_DIM, NSA_HPG * tq), F32)],
        compiler_params=_params(("parallel", "parallel", "arbitrary")),
    )(proj, kvc, kvc, proj, proj, proj, proj, gm, jnp.asarray(pool, BF16), _key_extra_columns(s, tq))


def _merge_body(sb_ref, nsa_ref, m0_ref, m1_ref, x_ref, wsb_ref, wnsa_ref, wout_ref, g_ref, o_ref):
    y_sb = jnp.dot(sb_ref[...], wsb_ref[...], preferred_element_type=F32)
    y_nsa = jnp.dot(nsa_ref[...], wnsa_ref[...], preferred_element_type=F32)
    merged = jax.nn.sigmoid(m0_ref[...]) * y_sb + jax.nn.sigmoid(m1_ref[...]) * y_nsa
    y = jnp.dot(merged.astype(BF16), wout_ref[...], preferred_element_type=F32)
    o_ref[...] = x_ref[...] + _rms(y, g_ref[...])


def _merge_out(sb, nsa, gm, x, w_sb, w_nsa, w_out, g_post, *, tm):
    m, d = x.shape
    assert m % tm == 0 and gm.shape[0] == m and gm.shape[1] >= 2 * d

    def resident(shape):
        return pl.BlockSpec(shape, lambda i: (0, 0), pipeline_mode=pl.Buffered(1))

    return pl.pallas_call(
        _merge_body,
        name="merge_out",
        grid=(m // tm,),
        in_specs=[
            pl.BlockSpec((tm, sb.shape[1]), lambda i: (i, 0)),
            pl.BlockSpec((tm, nsa.shape[1]), lambda i: (i, 0)),
            pl.BlockSpec((tm, d), lambda i: (i, 0)),
            pl.BlockSpec((tm, d), lambda i: (i, 1)),
            pl.BlockSpec((tm, d), lambda i: (i, 0)),
            resident(w_sb.shape), resident(w_nsa.shape), resident(w_out.shape),
            pl.BlockSpec((1, d), lambda i: (0, 0)),
        ],
        out_specs=pl.BlockSpec((tm, d), lambda i: (i, 0)),
        out_shape=jax.ShapeDtypeStruct((m, d), F32),
        compiler_params=_params(("parallel",)),
    )(sb, nsa, gm, gm, x, w_sb, w_nsa, w_out, g_post)


def kernel(x, ffn1_pre_g, ffn1_w_in, ffn1_w_out, ffn1_post_g, mix_pre_g, w_in, cmp_pos_k, cmp_k_w1, cmp_k_w2, cmp_pos_v, cmp_v_w1, cmp_v_w2, w_branch_sb, w_branch_nsa, w_out, mix_post_g, ffn2_pre_g, ffn2_w_in, ffn2_w_out, ffn2_post_g):
    b, s, d = x.shape
    m = b * s
    depth = ffn1_pre_g.shape[0]
    h = x.reshape(m, d)
    for l in range(depth):
        d_ff = ffn1_w_out.shape[1]
        h = _ffn(h, ffn1_pre_g[l][None], ffn1_w_in[l, :, :d_ff].astype(BF16), ffn1_w_in[l, :, d_ff:].astype(BF16),
                 ffn1_w_out[l].astype(BF16), ffn1_post_g[l][None], tm=512, tf=FFN_CHUNK)

        g_mix = mix_pre_g[l][None]
        col_scale = np.ones((w_in.shape[2],), np.float32)
        col_scale[COL_Q_SB * HEAD_DIM:(COL_Q_SB + SB_HEADS) * HEAD_DIM] = HEAD_DIM ** -0.5
        col_scale[COL_Q_NSA * HEAD_DIM:(COL_Q_NSA + NSA_HEADS) * HEAD_DIM] = HEAD_DIM ** -0.5
        wt_bf = (jnp.transpose(w_in[l]) * jnp.asarray(col_scale)[:, None]).astype(BF16)
        gate_end = QKV_COLS + N_GATE_LOGITS
        wt_logits = jnp.concatenate([wt_bf[gate_end:], wt_bf[QKV_COLS:gate_end],
                                     jnp.zeros((GATE_PAD - N_GATE_LOGITS, d), BF16)], axis=0)
        proj = _norm_matmul(h, g_mix, wt_bf, BF16, tm=1024, tn=_col_tile(QKV_COLS), n_out=QKV_COLS,
                            name="in_proj_qkv")
        logits = _norm_matmul(h, g_mix, wt_logits, F32, tm=1024, tn=_col_tile(2 * d + GATE_PAD),
                              n_out=2 * d + GATE_PAD, name="in_proj_logits")
        proj = proj.reshape(b, s, QKV_COLS)

        sb = _sb_attention(proj, tq=256, heads=8)

        cmp_cols = (COL_K_SLC - COL_K_CMP) * HEAD_DIM
        kv_cmp = _norm_matmul(h, g_mix, wt_bf, F32, tm=1024, tn=cmp_cols, n_out=cmp_cols,
                              col0=COL_K_CMP * HEAD_DIM, name="in_proj_cmp").reshape(b, s, cmp_cols)
        pos = jnp.stack([cmp_pos_k[l], cmp_pos_v[l]]).reshape(2, 2, CMP_STRIDE, HEAD_DIM)
        w1 = jnp.stack([cmp_k_w1[l], cmp_v_w1[l]]).astype(BF16).reshape(2, 2, CMP_STRIDE, HEAD_DIM, HEAD_DIM)
        w2 = jnp.stack([cmp_k_w2[l], cmp_v_w2[l]]).astype(BF16)
        kvc = _compress(kv_cmp, pos, w1, w2)

        nsa = _nsa_attention(proj, kvc,
                             logits.reshape(b, s, 2 * d + GATE_PAD), tq=512)

        h = _merge_out(sb.reshape(m, -1), nsa.reshape(m, -1), logits, h,
                       w_branch_sb[l].astype(BF16), w_branch_nsa[l].astype(BF16), w_out[l].astype(BF16),
                       mix_post_g[l][None], tm=256)

        h = _ffn(h, ffn2_pre_g[l][None], ffn2_w_in[l, :, :d_ff].astype(BF16), ffn2_w_in[l, :, d_ff:].astype(BF16),
                 ffn2_w_out[l].astype(BF16), ffn2_post_g[l][None], tm=512, tf=FFN_CHUNK)
    return h.reshape(b, s, d)
```

```python
import functools

import numpy as np
import jax
import jax.numpy as jnp
from jax import lax
from jax.experimental import pallas as pl
from jax.experimental.pallas import tpu as pltpu

HEAD_DIM = 128
SB_HEADS = 8
NSA_HEADS = 8
NSA_GROUPS = 2
NSA_HPG = NSA_HEADS // NSA_GROUPS
CMP_BLOCK = 32
CMP_STRIDE = 16
SLC_BLOCK = 64
N_SELECT = 16
WINDOW = 512
NORM_EPS = 1e-6
NEG_INF = -1e30
SCORE_SCALE = 2.0 ** 64
FORCED_SCORE = 1e30

QKV_COLS = (3 * SB_HEADS + NSA_HEADS + 6 * NSA_GROUPS) * HEAD_DIM
COL_Q_SB = 0
COL_K_SB = SB_HEADS
COL_V_SB = 2 * SB_HEADS
COL_Q_NSA = 3 * SB_HEADS
COL_K_CMP = COL_Q_NSA + NSA_HEADS
COL_K_SLC = COL_K_CMP + 2 * NSA_GROUPS
COL_V_SLC = COL_K_SLC + NSA_GROUPS
COL_K_WIN = COL_V_SLC + NSA_GROUPS
COL_V_WIN = COL_K_WIN + NSA_GROUPS
N_GATE_LOGITS = 3 * NSA_HEADS
GATE_PAD = 128

LANES = 128
V7X_VMEM_LIMIT = 56 * 1024 * 1024
V7X_VMEM_LIMIT_FFN = 60 * 1024 * 1024

F32 = jnp.float32
BF16 = jnp.bfloat16
NT_DIMS = (((1,), (1,)), ((), ()))


def _params(semantics, vmem_limit=V7X_VMEM_LIMIT):
    return pltpu.CompilerParams(dimension_semantics=semantics, vmem_limit_bytes=vmem_limit)


def _rms(x, g):
    ms = jnp.mean(x * x, axis=-1, keepdims=True)
    return x * lax.rsqrt(ms + NORM_EPS) * g


def _ffn_body(x_ref, gpre_ref, wg_ref, wu_ref, wo_ref, gpost_ref, o_ref, h_ref):
    j = pl.program_id(1)

    @pl.when(j == 0)
    def _():
        h_ref[...] = _rms(x_ref[...], gpre_ref[...]).astype(BF16)
        o_ref[...] = jnp.zeros_like(o_ref)

    h = h_ref[...]
    gate = jnp.dot(h, wg_ref[...].astype(BF16), preferred_element_type=F32)
    up = jnp.dot(h, wu_ref[...].astype(BF16), preferred_element_type=F32)
    act = (gate * jax.nn.sigmoid(gate)) * up
    o_ref[...] += jnp.dot(act.astype(BF16), wo_ref[...].astype(BF16), preferred_element_type=F32)

    @pl.when(j == pl.num_programs(1) - 1)
    def _():
        o_ref[...] = x_ref[...] + 0.5 * _rms(o_ref[...], gpost_ref[...])


def _ffn(x, g_pre, w_in, w_out, g_post, *, tm, tf):
    m, d = x.shape
    f = w_out.shape[0]
    nf = f // tf
    assert m % tm == 0 and f % tf == 0 and w_in.shape == (d, 2 * f)
    return pl.pallas_call(
        _ffn_body,
        name="ffn",
        grid=(m // tm, nf),
        in_specs=[
            pl.BlockSpec((tm, d), lambda i, j: (i, 0)),
            pl.BlockSpec((1, d), lambda i, j: (0, 0)),
            pl.BlockSpec((d, tf), lambda i, j: (0, j)),
            pl.BlockSpec((d, tf), lambda i, j: (0, j + nf)),
            pl.BlockSpec((tf, d), lambda i, j: (j, 0)),
            pl.BlockSpec((1, d), lambda i, j: (0, 0)),
        ],
        out_specs=pl.BlockSpec((tm, d), lambda i, j: (i, 0)),
        out_shape=jax.ShapeDtypeStruct((m, d), F32),
        scratch_shapes=[pltpu.VMEM((tm, d), BF16)],
        compiler_params=_params(("parallel", "arbitrary"), V7X_VMEM_LIMIT_FFN),
    )(x, g_pre, w_in, w_in, w_out, g_post)


def _norm_matmul_body(x_ref, g_ref, wt_ref, o_ref, h_ref):
    @pl.when(pl.program_id(1) == 0)
    def _():
        h_ref[...] = _rms(x_ref[...], g_ref[...]).astype(BF16)

    o_ref[...] = lax.dot_general(h_ref[...], wt_ref[...], NT_DIMS,
                                 preferred_element_type=F32).astype(o_ref.dtype)


MAX_COL_TILE = 1536


def _col_tile(n):
    assert n % LANES == 0
    blocks = n // LANES
    best = max(k for k in range(1, MAX_COL_TILE // LANES + 1) if blocks % k == 0)
    return best * LANES


def _norm_matmul(x, g, wt, out_dtype, *, tm, tn, n_out, name, col0=0):
    m, d = x.shape
    n = n_out
    assert m % tm == 0 and n % tn == 0 and col0 % tn == 0 and col0 + n <= wt.shape[0]
    jb = col0 // tn
    return pl.pallas_call(
        _norm_matmul_body,
        name=name,
        grid=(m // tm, n // tn),
        in_specs=[
            pl.BlockSpec((tm, d), lambda i, j: (i, 0)),
            pl.BlockSpec((1, d), lambda i, j: (0, 0)),
            pl.BlockSpec((tn, d), lambda i, j: (j + jb, 0)),
        ],
        out_specs=pl.BlockSpec((tm, tn), lambda i, j: (i, j)),
        out_shape=jax.ShapeDtypeStruct((m, n), out_dtype),
        scratch_shapes=[pltpu.VMEM((tm, d), BF16)],
        compiler_params=_params(("parallel", "arbitrary")),
    )(x, g, wt)


EXP_UNDERFLOW = 104.0
TN_DIMS = (((0,), (0,)), ((), ()))
LOG2_E = 1.4426950408889634


def _sb_body(q_ref, k_ref, v_ref, tri_ref, o_ref, c_ref, acc_ref, *, tq, heads):
    i = pl.program_id(2)
    wide = heads * tq
    tri = tri_ref[...]
    key_i = lax.broadcasted_iota(jnp.int32, (tq, wide), 0)
    qry_i = lax.rem(lax.broadcasted_iota(jnp.int32, (tq, wide), 1), tq)
    past = key_i < qry_i
    head_lanes = [slice(h * HEAD_DIM, (h + 1) * HEAD_DIM) for h in range(heads)]

    def tile(j, diag):
        k0 = pl.multiple_of(j * tq, tq)
        z = jnp.concatenate(
            [lax.dot_general(k_ref[0, pl.ds(k0, tq), hl], q_ref[0, :, hl], NT_DIMS,
                             preferred_element_type=F32) for hl in head_lanes], axis=1)
        softplus = jnp.maximum(z, 0.0) + jnp.log(1.0 + jnp.exp2(jnp.abs(z) * (-LOG2_E)))
        log_beta = z - softplus
        if diag:
            softplus = jnp.where(past, softplus, 0.0)
        hi = softplus.astype(BF16)
        lo = (softplus - hi.astype(F32)).astype(BF16)
        later = jnp.dot(tri, jnp.concatenate([hi, lo], axis=0), preferred_element_type=F32)
        c = c_ref[...]
        w = jnp.exp(log_beta - c - later)
        if diag:
            w = jnp.where(past, w, 0.0)
        w = w.astype(BF16)
        for h, hl in enumerate(head_lanes):
            cols = slice(h * tq, (h + 1) * tq)
            acc_ref[:, cols] += lax.dot_general(v_ref[0, pl.ds(k0, tq), hl], w[:, cols], TN_DIMS,
                                                preferred_element_type=F32)
        c = c + jnp.sum(softplus, axis=0, keepdims=True)
        c_ref[...] = c
        return (jnp.min(c) <= EXP_UNDERFLOW).astype(jnp.int32)

    c_ref[...] = jnp.zeros_like(c_ref)
    acc_ref[...] = jnp.zeros_like(acc_ref)
    alive = tile(i, True)

    def keep_going(carry):
        j, alive = carry
        return jnp.logical_and(j >= 0, alive > 0)

    lax.while_loop(keep_going, lambda carry: (carry[0] - 1, tile(carry[0], False)), (i - 1, alive))
    for h, hl in enumerate(head_lanes):
        o_ref[0, :, hl] = acc_ref[:, h * tq:(h + 1) * tq].T.astype(o_ref.dtype)


def _sb_attention(proj, *, tq, heads):
    b, s, _ = proj.shape
    assert s % tq == 0 and SB_HEADS % heads == 0
    upper = np.triu(np.ones((tq, tq), np.float32), 1)
    tri = jnp.asarray(np.concatenate([upper, upper], axis=1), BF16)
    width = heads * HEAD_DIM
    return pl.pallas_call(
        functools.partial(_sb_body, tq=tq, heads=heads),
        name="sb_attn",
        grid=(b, SB_HEADS // heads, s // tq),
        in_specs=[
            pl.BlockSpec((1, tq, width), lambda bb, h, i: (bb, i, COL_Q_SB // heads + h)),
            pl.BlockSpec((1, s, width), lambda bb, h, i: (bb, 0, COL_K_SB // heads + h)),
            pl.BlockSpec((1, s, width), lambda bb, h, i: (bb, 0, COL_V_SB // heads + h)),
            pl.BlockSpec((tq, 2 * tq), lambda bb, h, i: (0, 0)),
        ],
        out_specs=pl.BlockSpec((1, tq, width), lambda bb, h, i: (bb, i, h)),
        out_shape=jax.ShapeDtypeStruct((b, s, SB_HEADS * HEAD_DIM), BF16),
        scratch_shapes=[pltpu.VMEM((1, heads * tq), F32), pltpu.VMEM((HEAD_DIM, heads * tq), F32)],
        compiler_params=_params(("parallel", "parallel", "arbitrary")),
    )(proj, proj, proj, tri)


def _compress_body(x_ref, pos_ref, w1_ref, w2_ref, o_ref, x32_ref):
    n_chunk = o_ref.shape[2]
    x32_ref[...] = x_ref[0].astype(F32)
    first = jnp.zeros((n_chunk, HEAD_DIM), F32)
    second = jnp.zeros((n_chunk, HEAD_DIM), F32)
    for l in range(CMP_STRIDE):
        x_l = x32_ref[pl.ds(l, n_chunk, stride=CMP_STRIDE), :]
        xa = (x_l + pos_ref[0, 0, l:l + 1, :]).astype(BF16)
        xb = (x_l + pos_ref[0, 1, l:l + 1, :]).astype(BF16)
        first = first + jnp.dot(xa, w1_ref[0, 0, l], preferred_element_type=F32)
        second = second + jnp.dot(xb, w1_ref[0, 1, l], preferred_element_type=F32)
    pre = first + pltpu.roll(second, n_chunk - 1, 0)
    y = jax.nn.gelu(pre).astype(BF16)
    o_ref[0, 0] = jnp.dot(y, w2_ref[0], preferred_element_type=F32).astype(o_ref.dtype)


def _compress(proj, pos, w1, w2):
    b, s, _ = proj.shape
    n_chunk = s // CMP_STRIDE
    return pl.pallas_call(
        _compress_body,
        name="compress",
        grid=(2, b * NSA_GROUPS),
        in_specs=[
            pl.BlockSpec((1, s, HEAD_DIM),
                         lambda a, n: (n // NSA_GROUPS, 0, COL_K_CMP + a * NSA_GROUPS + n % NSA_GROUPS)),
            pl.BlockSpec((1, 2, CMP_STRIDE, HEAD_DIM), lambda a, n: (a, 0, 0, 0)),
            pl.BlockSpec((1, 2, CMP_STRIDE, HEAD_DIM, HEAD_DIM), lambda a, n: (a, 0, 0, 0, 0)),
            pl.BlockSpec((1, HEAD_DIM, HEAD_DIM), lambda a, n: (a, 0, 0)),
        ],
        out_specs=pl.BlockSpec((1, 1, n_chunk, HEAD_DIM), lambda a, n: (a, n, 0, 0)),
        out_shape=jax.ShapeDtypeStruct((2, b * NSA_GROUPS, n_chunk, HEAD_DIM), BF16),
        scratch_shapes=[pltpu.VMEM((s, HEAD_DIM), F32)],
        compiler_params=_params(("parallel", "parallel")),
    )(proj, pos, w1, w2)


POS_HI_COL = 0
POS_LO_COL = 1
BLOCK_COL0 = HEAD_DIM // 2
SLAB_PARTS_PER_HEAD = 2


def _key_extra_columns(s, tk):
    pos = np.arange(s)
    extra = np.zeros((s, HEAD_DIM), np.float32)
    extra[:, POS_HI_COL] = (pos % tk) // SLC_BLOCK
    extra[:, POS_LO_COL] = pos % SLC_BLOCK
    extra[pos, BLOCK_COL0 + pos // SLC_BLOCK] = 1.0
    return jnp.asarray(extra, BF16)


def _nsa_body(q_ref, kc_ref, vc_ref, ks_ref, vs_ref, kw_ref, vw_ref, gate_ref,
              pool_ref, kextra_ref, o_ref, ksa_ref, kwa_ref, m_ref, l_ref, acc_ref, *, tq, n_slc):
    g = pl.program_id(1)
    i = pl.program_id(2)
    q0 = i * tq
    n_cmp_pad = kc_ref.shape[2]

    @pl.when(i == 0)
    def _():
        ksa_ref[:, :HEAD_DIM] = ks_ref[0]
        ksa_ref[:, HEAD_DIM:] = kextra_ref[...]
        kwa_ref[:, :HEAD_DIM] = kw_ref[0]
        kwa_ref[:, HEAD_DIM:] = kextra_ref[...]

    slopes = [jnp.where(g == 0, 2.0 ** -(r + 1), 2.0 ** -(r + 1 + NSA_HPG)) for r in range(NSA_HPG)]
    q_heads = [q_ref[0, :, r * HEAD_DIM:(r + 1) * HEAD_DIM] for r in range(NSA_HPG)]
    t_row = q0 + lax.broadcasted_iota(jnp.int32, (1, tq), 1)

    wide = NSA_HPG * tq
    head_of_lane = lax.broadcasted_iota(jnp.int32, (1, wide), 1) // tq
    slope_row = jnp.zeros((1, wide), F32)
    for r in range(NSA_HPG):
        slope_row = jnp.where(head_of_lane == r, slopes[r], slope_row)

    def split_heads(x):
        return [x[:, r * tq:(r + 1) * tq] for r in range(NSA_HPG)]

    kc = kc_ref[0, 0]
    vc = vc_ref[0, 0]
    c_idx = lax.broadcasted_iota(jnp.int32, (n_cmp_pad, wide), 0)
    t_wide = q0 + lax.rem(lax.broadcasted_iota(jnp.int32, (1, wide), 1), tq)
    cmp_end = c_idx * CMP_STRIDE + (CMP_BLOCK - 1)
    valid_c = jnp.logical_and(t_wide >= cmp_end, c_idx < n_cmp_pad - 1)
    s = lax.dot_general(kc, jnp.concatenate(q_heads, axis=0), NT_DIMS, preferred_element_type=F32)
    s = jnp.where(valid_c, s + slope_row * (cmp_end - q0).astype(F32), NEG_INF)
    m = jnp.max(s, axis=0, keepdims=True)
    p = jnp.where(valid_c, jnp.exp(s - m), 0.0)
    l = jnp.sum(p, axis=0, keepdims=True)
    p = p * (1.0 / jnp.where(l > 0.0, l, 1.0))
    o_cmp = split_heads(lax.dot_general(vc, p.astype(BF16), TN_DIMS,
                                        preferred_element_type=F32))
    p_grp = functools.reduce(lambda a, b: a + b, split_heads(p))

    pool = pool_ref[...]
    p_grp = p_grp * SCORE_SCALE
    p1 = p_grp.astype(BF16)
    r1 = p_grp - p1.astype(F32)
    p2 = r1.astype(BF16)
    p3 = (r1 - p2.astype(F32)).astype(BF16)
    score = (jnp.dot(pool, p1, preferred_element_type=F32)
             + jnp.dot(pool, p2, preferred_element_type=F32)
             + jnp.dot(pool, p3, preferred_element_type=F32))
    blk = lax.broadcasted_iota(jnp.int32, (n_slc, tq), 0)
    cur = t_row // SLC_BLOCK
    forced = jnp.logical_or(blk == 0, jnp.logical_or(blk == cur, blk == cur - 1))
    score = jnp.where(forced, FORCED_SCORE, score)
    score = jnp.where(blk <= cur, score, NEG_INF)

    sub = 8
    sub_i = lax.broadcasted_iota(jnp.int32, (sub, tq), 0)
    groups = [score[a:a + sub, :] for a in range(0, n_slc, sub)]
    ranks = [jnp.zeros((sub, tq), F32) for _ in groups]
    for ii in range(n_slc):
        row = score[ii:ii + 1, :]
        for gi, grp in enumerate(groups):
            ge = jnp.where(row >= grp, 1.0, 0.0)
            gt = jnp.where(row > grp, 1.0, 0.0)
            if ii < gi * sub:
                beats = ge
            elif ii >= (gi + 1) * sub:
                beats = gt
            else:
                beats = jnp.where(sub_i > ii - gi * sub, ge, gt)
            ranks[gi] = ranks[gi] + beats
    rank = jnp.concatenate(ranks, axis=0)
    sel_bias_t = jnp.where(rank < float(N_SELECT), 0.0, NEG_INF)

    pieces = [jnp.zeros((BLOCK_COL0, tq), F32), sel_bias_t]
    if BLOCK_COL0 + n_slc < HEAD_DIM:
        pieces.append(jnp.zeros((HEAD_DIM - BLOCK_COL0 - n_slc, tq), F32))
    sel_extra = jnp.concatenate(pieces, axis=0).T
    lane = lax.broadcasted_iota(jnp.int32, (tq, HEAD_DIM), 1)

    def query_slab(r, extra):
        pos_cols = jnp.where(lane == POS_HI_COL, slopes[r] * SLC_BLOCK,
                             jnp.where(lane == POS_LO_COL, slopes[r], extra))
        return jnp.concatenate([q_heads[r], pos_cols.astype(BF16)], axis=1)

    q_slc = jnp.concatenate([query_slab(r, sel_extra) for r in range(NSA_HPG)], axis=0)
    q_win = jnp.concatenate([query_slab(r, 0.0) for r in range(NSA_HPG)], axis=0)

    part_w = tq // SLAB_PARTS_PER_HEAD
    parts = [(r * tq + a * part_w, a) for r in range(NSA_HPG) for a in range(SLAB_PARTS_PER_HEAD)]

    def key_rows(d, a, limit):
        lo = max(0, d * tq + a * part_w - limit)
        hi = min(tq - 1, d * tq + (a + 1) * part_w - 1)
        return (lo // part_w) * part_w, (hi // part_w + 1) * part_w

    def distance_mask(d, a, limit, rows):
        lo, hi = rows
        dist_min = d * tq + a * part_w - (hi - 1)
        dist_max = d * tq + (a + 1) * part_w - 1 - lo
        if dist_min >= 0 and dist_max <= limit:
            return None
        key = lo + lax.broadcasted_iota(jnp.int32, (hi - lo, part_w), 0)
        qry = a * part_w + lax.broadcasted_iota(jnp.int32, (hi - lo, part_w), 1)
        dist = d * tq + qry - key
        return jnp.logical_and(dist >= 0, dist <= limit)

    def scores(ka_ref, q_all, j, rows_of):
        k0 = pl.multiple_of(j * tq, tq)
        out = []
        for c0, a in parts:
            lo, hi = rows_of(a)
            ka = ka_ref[pl.ds(k0 + lo, hi - lo), :]
            out.append(lax.dot_general(ka, q_all[c0:c0 + part_w, :], NT_DIMS, preferred_element_type=F32))
        return out

    def reset():
        m_ref[...] = jnp.full((1, wide), NEG_INF, F32)
        l_ref[...] = jnp.zeros((1, wide), F32)
        acc_ref[...] = jnp.zeros((HEAD_DIM, wide), F32)

    def absorb(v_ref, s_parts, j, rows_of, mask_of, live=None):
        k0 = pl.multiple_of(j * tq, tq)
        tile_off = (k0 - q0).astype(F32)
        for (c0, a), s in zip(parts, s_parts):
            cols = slice(c0, c0 + part_w)
            lo, hi = rows_of(a)
            mask = mask_of(a)
            if mask is not None:
                s = jnp.where(mask, s, NEG_INF)
            if live is not None:
                s = jnp.where(live, s, NEG_INF)
            off = slope_row[:, cols] * tile_off
            m = m_ref[:, cols]
            m_new = jnp.maximum(m, jnp.max(s, axis=0, keepdims=True) + off)
            alpha = jnp.exp(m - m_new)
            p = jnp.exp(s - (m_new - off))
            m_ref[:, cols] = m_new
            l_ref[:, cols] = alpha * l_ref[:, cols] + jnp.sum(p, axis=0, keepdims=True)
            acc_ref[:, cols] = alpha * acc_ref[:, cols] + lax.dot_general(
                v_ref[0, pl.ds(k0 + lo, hi - lo), :], p.astype(BF16), TN_DIMS, preferred_element_type=F32)

    def finish():
        return split_heads(acc_ref[...] * (1.0 / l_ref[...]))

    def tile_plan(d, limit):
        rows_of = lambda a: key_rows(d, a, limit)
        return rows_of, (lambda a: distance_mask(d, a, limit, rows_of(a)))

    all_rows = lambda a: (0, tq)
    no_mask = lambda a: None

    reset()
    rows_of, mask_of = tile_plan(0, tq)
    absorb(vs_ref, scores(ksa_ref, q_slc, i, rows_of), i, rows_of, mask_of)

    @pl.loop(0, i)
    def _(j):
        absorb(vs_ref, scores(ksa_ref, q_slc, j, all_rows), j, all_rows, no_mask)

    o_slc = finish()

    reset()
    for d in range((WINDOW - 1) // tq + 2):
        rows_of, mask_of = tile_plan(d, WINDOW - 1)
        j = jnp.maximum(i - d, 0)
        absorb(vw_ref, scores(kwa_ref, q_win, j, rows_of), j, rows_of, mask_of,
               live=None if d == 0 else i - d >= 0)
    o_win = finish()

    sg_t = jax.nn.sigmoid(gate_ref[0]).T

    def gate(branch, r):
        c0 = branch * NSA_HEADS + r
        c1 = c0 + NSA_HPG
        return jnp.where(g == 0, sg_t[c0:c0 + 1, :], sg_t[c1:c1 + 1, :])

    for r in range(NSA_HPG):
        out_t = gate(0, r) * o_cmp[r] + gate(1, r) * o_slc[r] + gate(2, r) * o_win[r]
        o_ref[0, :, r * HEAD_DIM:(r + 1) * HEAD_DIM] = out_t.T.astype(o_ref.dtype)


def _nsa_attention(proj, kvc, gm, *, tq):
    b, s, _ = proj.shape
    n_cmp_pad = s // CMP_STRIDE
    n_slc = s // SLC_BLOCK
    assert s % tq == 0 and tq % SLC_BLOCK == 0 and n_cmp_pad % LANES == 0
    assert BLOCK_COL0 + n_slc <= HEAD_DIM and tq // SLC_BLOCK <= 256 and n_slc % 8 == 0
    ratio = SLC_BLOCK // CMP_STRIDE
    span = CMP_BLOCK // CMP_STRIDE
    pool = np.zeros((n_slc, n_cmp_pad), np.float32)
    for jj in range(n_slc):
        for mm in range(ratio):
            for nn in range(span):
                c = ratio * jj + mm + nn
                if c < n_cmp_pad:
                    pool[jj, c] += 1.0
    grp_w = NSA_HPG * HEAD_DIM
    kv_spec = lambda col: pl.BlockSpec((1, s, HEAD_DIM), lambda bb, g, i: (bb, 0, col + g))
    cmp_spec = lambda a: pl.BlockSpec((1, 1, n_cmp_pad, HEAD_DIM),
                                      lambda bb, g, i: (a, bb * NSA_GROUPS + g, 0, 0))
    return pl.pallas_call(
        functools.partial(_nsa_body, tq=tq, n_slc=n_slc),
        name="nsa_attn",
        grid=(b, NSA_GROUPS, s // tq),
        in_specs=[
            pl.BlockSpec((1, tq, grp_w), lambda bb, g, i: (bb, i, COL_Q_NSA // NSA_HPG + g)),
            cmp_spec(0), cmp_spec(1),
            kv_spec(COL_K_SLC), kv_spec(COL_V_SLC), kv_spec(COL_K_WIN), kv_spec(COL_V_WIN),
            pl.BlockSpec((1, tq, GATE_PAD), lambda bb, g, i: (bb, i, gm.shape[2] // GATE_PAD - 1)),
            pl.BlockSpec((n_slc, n_cmp_pad), lambda bb, g, i: (0, 0)),
            pl.BlockSpec((s, HEAD_DIM), lambda bb, g, i: (0, 0)),
        ],
        out_specs=pl.BlockSpec((1, tq, grp_w), lambda bb, g, i: (bb, i, g)),
        out_shape=jax.ShapeDtypeStruct((b, s, NSA_HEADS * HEAD_DIM), BF16),
        scratch_shapes=[pltpu.VMEM((s, 2 * HEAD_DIM), BF16), pltpu.VMEM((s, 2 * HEAD_DIM), BF16),
                        pltpu.VMEM((1, NSA_HPG * tq), F32), pltpu.VMEM((1, NSA_HPG * tq), F32),
                        pltpu.VMEM((HEAD_DIM, NSA_HPG * tq), F32)],
        compiler_params=_params(("parallel", "parallel", "arbitrary")),
    )(proj, kvc, kvc, proj, proj, proj, proj, gm, jnp.asarray(pool, BF16), _key_extra_columns(s, tq))


def _merge_body(sb_ref, nsa_ref, m0_ref, m1_ref, x_ref, wsb_ref, wnsa_ref, wout_ref, g_ref, o_ref):
    y_sb = jnp.dot(sb_ref[...], wsb_ref[...], preferred_element_type=F32)
    y_nsa = jnp.dot(nsa_ref[...], wnsa_ref[...], preferred_element_type=F32)
    merged = jax.nn.sigmoid(m0_ref[...]) * y_sb + jax.nn.sigmoid(m1_ref[...]) * y_nsa
    y = jnp.dot(merged.astype(BF16), wout_ref[...], preferred_element_type=F32)
    o_ref[...] = x_ref[...] + _rms(y, g_ref[...])


def _merge_out(sb, nsa, gm, x, w_sb, w_nsa, w_out, g_post, *, tm):
    m, d = x.shape
    assert m % tm == 0 and gm.shape[0] == m and gm.shape[1] >= 2 * d

    def resident(shape):
        return pl.BlockSpec(shape, lambda i: (0, 0), pipeline_mode=pl.Buffered(1))

    return pl.pallas_call(
        _merge_body,
        name="merge_out",
        grid=(m // tm,),
        in_specs=[
            pl.BlockSpec((tm, sb.shape[1]), lambda i: (i, 0)),
            pl.BlockSpec((tm, nsa.shape[1]), lambda i: (i, 0)),
            pl.BlockSpec((tm, d), lambda i: (i, 0)),
            pl.BlockSpec((tm, d), lambda i: (i, 1)),
            pl.BlockSpec((tm, d), lambda i: (i, 0)),
            resident(w_sb.shape), resident(w_nsa.shape), resident(w_out.shape),
            pl.BlockSpec((1, d), lambda i: (0, 0)),
        ],
        out_specs=pl.BlockSpec((tm, d), lambda i: (i, 0)),
        out_shape=jax.ShapeDtypeStruct((m, d), F32),
        compiler_params=_params(("parallel",)),
    )(sb, nsa, gm, gm, x, w_sb, w_nsa, w_out, g_post)


def kernel(x, ffn1_pre_g, ffn1_w_in, ffn1_w_out, ffn1_post_g, mix_pre_g, w_in, cmp_pos_k, cmp_k_w1, cmp_k_w2, cmp_pos_v, cmp_v_w1, cmp_v_w2, w_branch_sb, w_branch_nsa, w_out, mix_post_g, ffn2_pre_g, ffn2_w_in, ffn2_w_out, ffn2_post_g):
    b, s, d = x.shape
    m = b * s
    depth = ffn1_pre_g.shape[0]
    h = x.reshape(m, d)
    for l in range(depth):
        h = _ffn(h, ffn1_pre_g[l][None], ffn1_w_in[l], ffn1_w_out[l], ffn1_post_g[l][None], tm=1024, tf=256)

        g_mix = mix_pre_g[l][None]
        col_scale = np.ones((w_in.shape[2],), np.float32)
        col_scale[COL_Q_SB * HEAD_DIM:(COL_Q_SB + SB_HEADS) * HEAD_DIM] = HEAD_DIM ** -0.5
        col_scale[COL_Q_NSA * HEAD_DIM:(COL_Q_NSA + NSA_HEADS) * HEAD_DIM] = HEAD_DIM ** -0.5
        wt_bf = (jnp.transpose(w_in[l]) * jnp.asarray(col_scale)[:, None]).astype(BF16)
        gate_end = QKV_COLS + N_GATE_LOGITS
        wt_logits = jnp.concatenate([wt_bf[gate_end:], wt_bf[QKV_COLS:gate_end],
                                     jnp.zeros((GATE_PAD - N_GATE_LOGITS, d), BF16)], axis=0)
        proj = _norm_matmul(h, g_mix, wt_bf, BF16, tm=1024, tn=_col_tile(QKV_COLS), n_out=QKV_COLS,
                            name="in_proj_qkv")
        logits = _norm_matmul(h, g_mix, wt_logits, F32, tm=1024, tn=_col_tile(2 * d + GATE_PAD),
                              n_out=2 * d + GATE_PAD, name="in_proj_logits")
        proj = proj.reshape(b, s, QKV_COLS)

        sb = _sb_attention(proj, tq=256, heads=8)

        pos = jnp.stack([cmp_pos_k[l], cmp_pos_v[l]]).reshape(2, 2, CMP_STRIDE, HEAD_DIM)
        w1 = jnp.stack([cmp_k_w1[l], cmp_v_w1[l]]).astype(BF16).reshape(2, 2, CMP_STRIDE, HEAD_DIM, HEAD_DIM)
        w2 = jnp.stack([cmp_k_w2[l], cmp_v_w2[l]]).astype(BF16)
        kvc = _compress(proj, pos, w1, w2)

        nsa = _nsa_attention(proj, kvc,
                             logits.reshape(b, s, 2 * d + GATE_PAD), tq=512)

        h = _merge_out(sb.reshape(m, -1), nsa.reshape(m, -1), logits, h,
                       w_branch_sb[l].astype(BF16), w_branch_nsa[l].astype(BF16), w_out[l].astype(BF16),
                       mix_post_g[l][None], tm=256)

        h = _ffn(h, ffn2_pre_g[l][None], ffn2_w_in[l], ffn2_w_out[l], ffn2_post_g[l][None], tm=1024, tf=256)
    return h.reshape(b, s, d)
```

```python
import functools

import numpy as np
import jax
import jax.numpy as jnp
from jax import lax
from jax.experimental import pallas as pl
from jax.experimental.pallas import tpu as pltpu

HEAD_DIM = 128
SB_HEADS = 8
NSA_HEADS = 8
NSA_GROUPS = 2
NSA_HPG = NSA_HEADS // NSA_GROUPS
CMP_BLOCK = 32
CMP_STRIDE = 16
SLC_BLOCK = 64
N_SELECT = 16
WINDOW = 512
NORM_EPS = 1e-6
NEG_INF = -1e30
SCORE_SCALE = 2.0 ** 64
FORCED_SCORE = 1e30

QKV_COLS = (3 * SB_HEADS + NSA_HEADS + 6 * NSA_GROUPS) * HEAD_DIM
COL_Q_SB = 0
COL_K_SB = SB_HEADS
COL_V_SB = 2 * SB_HEADS
COL_Q_NSA = 3 * SB_HEADS
COL_K_CMP = COL_Q_NSA + NSA_HEADS
COL_K_SLC = COL_K_CMP + 2 * NSA_GROUPS
COL_V_SLC = COL_K_SLC + NSA_GROUPS
COL_K_WIN = COL_V_SLC + NSA_GROUPS
COL_V_WIN = COL_K_WIN + NSA_GROUPS
N_GATE_LOGITS = 3 * NSA_HEADS
GATE_PAD = 128

LANES = 128
V7X_VMEM_LIMIT = 56 * 1024 * 1024
V7X_VMEM_LIMIT_FFN = 60 * 1024 * 1024

F32 = jnp.float32
BF16 = jnp.bfloat16
NT_DIMS = (((1,), (1,)), ((), ()))


def _params(semantics, vmem_limit=V7X_VMEM_LIMIT):
    return pltpu.CompilerParams(dimension_semantics=semantics, vmem_limit_bytes=vmem_limit)


def _rms(x, g):
    ms = jnp.mean(x * x, axis=-1, keepdims=True)
    return x * lax.rsqrt(ms + NORM_EPS) * g


def _ffn_body(x_ref, gpre_ref, wg_ref, wu_ref, wo_ref, gpost_ref, o_ref, h_ref):
    j = pl.program_id(1)

    @pl.when(j == 0)
    def _():
        h_ref[...] = _rms(x_ref[...], gpre_ref[...]).astype(BF16)
        o_ref[...] = jnp.zeros_like(o_ref)

    h = h_ref[...]
    gate = jnp.dot(h, wg_ref[...].astype(BF16), preferred_element_type=F32)
    up = jnp.dot(h, wu_ref[...].astype(BF16), preferred_element_type=F32)
    act = (gate * jax.nn.sigmoid(gate)) * up
    o_ref[...] += jnp.dot(act.astype(BF16), wo_ref[...].astype(BF16), preferred_element_type=F32)

    @pl.when(j == pl.num_programs(1) - 1)
    def _():
        o_ref[...] = x_ref[...] + 0.5 * _rms(o_ref[...], gpost_ref[...])


def _ffn(x, g_pre, w_in, w_out, g_post, *, tm, tf):
    m, d = x.shape
    f = w_out.shape[0]
    nf = f // tf
    assert m % tm == 0 and f % tf == 0 and w_in.shape == (d, 2 * f)
    return pl.pallas_call(
        _ffn_body,
        name="ffn",
        grid=(m // tm, nf),
        in_specs=[
            pl.BlockSpec((tm, d), lambda i, j: (i, 0)),
            pl.BlockSpec((1, d), lambda i, j: (0, 0)),
            pl.BlockSpec((d, tf), lambda i, j: (0, j)),
            pl.BlockSpec((d, tf), lambda i, j: (0, j + nf)),
            pl.BlockSpec((tf, d), lambda i, j: (j, 0)),
            pl.BlockSpec((1, d), lambda i, j: (0, 0)),
        ],
        out_specs=pl.BlockSpec((tm, d), lambda i, j: (i, 0)),
        out_shape=jax.ShapeDtypeStruct((m, d), F32),
        scratch_shapes=[pltpu.VMEM((tm, d), BF16)],
        compiler_params=_params(("parallel", "arbitrary"), V7X_VMEM_LIMIT_FFN),
    )(x, g_pre, w_in, w_in, w_out, g_post)


def _norm_matmul_body(x_ref, g_ref, wt_ref, o_ref, h_ref):
    @pl.when(pl.program_id(1) == 0)
    def _():
        h_ref[...] = _rms(x_ref[...], g_ref[...]).astype(BF16)

    o_ref[...] = lax.dot_general(h_ref[...], wt_ref[...], NT_DIMS,
                                 preferred_element_type=F32).astype(o_ref.dtype)


MAX_COL_TILE = 1536


def _col_tile(n):
    assert n % LANES == 0
    blocks = n // LANES
    best = max(k for k in range(1, MAX_COL_TILE // LANES + 1) if blocks % k == 0)
    return best * LANES


def _norm_matmul(x, g, wt, out_dtype, *, tm, tn, n_out, name, col0=0):
    m, d = x.shape
    n = n_out
    assert m % tm == 0 and n % tn == 0 and col0 % tn == 0 and col0 + n <= wt.shape[0]
    jb = col0 // tn
    return pl.pallas_call(
        _norm_matmul_body,
        name=name,
        grid=(m // tm, n // tn),
        in_specs=[
            pl.BlockSpec((tm, d), lambda i, j: (i, 0)),
            pl.BlockSpec((1, d), lambda i, j: (0, 0)),
            pl.BlockSpec((tn, d), lambda i, j: (j + jb, 0)),
        ],
        out_specs=pl.BlockSpec((tm, tn), lambda i, j: (i, j)),
        out_shape=jax.ShapeDtypeStruct((m, n), out_dtype),
        scratch_shapes=[pltpu.VMEM((tm, d), BF16)],
        compiler_params=_params(("parallel", "arbitrary")),
    )(x, g, wt)


EXP_UNDERFLOW = 104.0
TN_DIMS = (((0,), (0,)), ((), ()))
LOG2_E = 1.4426950408889634


def _sb_body(q_ref, k_ref, v_ref, tri_ref, o_ref, c_ref, acc_ref, *, tq, heads):
    i = pl.program_id(2)
    wide = heads * tq
    tri = tri_ref[...]
    key_i = lax.broadcasted_iota(jnp.int32, (tq, wide), 0)
    qry_i = lax.rem(lax.broadcasted_iota(jnp.int32, (tq, wide), 1), tq)
    past = key_i < qry_i
    head_lanes = [slice(h * HEAD_DIM, (h + 1) * HEAD_DIM) for h in range(heads)]

    def tile(j, diag):
        k0 = pl.multiple_of(j * tq, tq)
        z = jnp.concatenate(
            [lax.dot_general(k_ref[0, pl.ds(k0, tq), hl], q_ref[0, :, hl], NT_DIMS,
                             preferred_element_type=F32) for hl in head_lanes], axis=1)
        softplus = jnp.maximum(z, 0.0) + jnp.log(1.0 + jnp.exp2(jnp.abs(z) * (-LOG2_E)))
        log_beta = z - softplus
        if diag:
            softplus = jnp.where(past, softplus, 0.0)
        hi = softplus.astype(BF16)
        lo = (softplus - hi.astype(F32)).astype(BF16)
        later = jnp.dot(tri, jnp.concatenate([hi, lo], axis=0), preferred_element_type=F32)
        c = c_ref[...]
        w = jnp.exp(log_beta - c - later)
        if diag:
            w = jnp.where(past, w, 0.0)
        w = w.astype(BF16)
        for h, hl in enumerate(head_lanes):
            cols = slice(h * tq, (h + 1) * tq)
            acc_ref[:, cols] += lax.dot_general(v_ref[0, pl.ds(k0, tq), hl], w[:, cols], TN_DIMS,
                                                preferred_element_type=F32)
        c = c + jnp.sum(softplus, axis=0, keepdims=True)
        c_ref[...] = c
        return (jnp.min(c) <= EXP_UNDERFLOW).astype(jnp.int32)

    c_ref[...] = jnp.zeros_like(c_ref)
    acc_ref[...] = jnp.zeros_like(acc_ref)
    alive = tile(i, True)

    def keep_going(carry):
        j, alive = carry
        return jnp.logical_and(j >= 0, alive > 0)

    lax.while_loop(keep_going, lambda carry: (carry[0] - 1, tile(carry[0], False)), (i - 1, alive))
    for h, hl in enumerate(head_lanes):
        o_ref[0, :, hl] = acc_ref[:, h * tq:(h + 1) * tq].T.astype(o_ref.dtype)


def _sb_attention(proj, *, tq, heads):
    b, s, _ = proj.shape
    assert s % tq == 0 and SB_HEADS % heads == 0
    upper = np.triu(np.ones((tq, tq), np.float32), 1)
    tri = jnp.asarray(np.concatenate([upper, upper], axis=1), BF16)
    width = heads * HEAD_DIM
    return pl.pallas_call(
        functools.partial(_sb_body, tq=tq, heads=heads),
        name="sb_attn",
        grid=(b, SB_HEADS // heads, s // tq),
        in_specs=[
            pl.BlockSpec((1, tq, width), lambda bb, h, i: (bb, i, COL_Q_SB // heads + h)),
            pl.BlockSpec((1, s, width), lambda bb, h, i: (bb, 0, COL_K_SB // heads + h)),
            pl.BlockSpec((1, s, width), lambda bb, h, i: (bb, 0, COL_V_SB // heads + h)),
            pl.BlockSpec((tq, 2 * tq), lambda bb, h, i: (0, 0)),
        ],
        out_specs=pl.BlockSpec((1, tq, width), lambda bb, h, i: (bb, i, h)),
        out_shape=jax.ShapeDtypeStruct((b, s, SB_HEADS * HEAD_DIM), BF16),
        scratch_shapes=[pltpu.VMEM((1, heads * tq), F32), pltpu.VMEM((HEAD_DIM, heads * tq), F32)],
        compiler_params=_params(("parallel", "parallel", "arbitrary")),
    )(proj, proj, proj, tri)


def _compress_body(x_ref, pos_ref, w1_ref, w2_ref, o_ref, x32_ref):
    n_chunk = o_ref.shape[2]
    x32_ref[...] = x_ref[0].astype(F32)
    first = jnp.zeros((n_chunk, HEAD_DIM), F32)
    second = jnp.zeros((n_chunk, HEAD_DIM), F32)
    for l in range(CMP_STRIDE):
        x_l = x32_ref[pl.ds(l, n_chunk, stride=CMP_STRIDE), :]
        xa = (x_l + pos_ref[0, 0, l:l + 1, :]).astype(BF16)
        xb = (x_l + pos_ref[0, 1, l:l + 1, :]).astype(BF16)
        first = first + jnp.dot(xa, w1_ref[0, 0, l], preferred_element_type=F32)
        second = second + jnp.dot(xb, w1_ref[0, 1, l], preferred_element_type=F32)
    pre = first + pltpu.roll(second, n_chunk - 1, 0)
    y = jax.nn.gelu(pre).astype(BF16)
    o_ref[0, 0] = jnp.dot(y, w2_ref[0], preferred_element_type=F32).astype(o_ref.dtype)


def _compress(proj, pos, w1, w2):
    b, s, _ = proj.shape
    n_chunk = s // CMP_STRIDE
    return pl.pallas_call(
        _compress_body,
        name="compress",
        grid=(2, b * NSA_GROUPS),
        in_specs=[
            pl.BlockSpec((1, s, HEAD_DIM),
                         lambda a, n: (n // NSA_GROUPS, 0, COL_K_CMP + a * NSA_GROUPS + n % NSA_GROUPS)),
            pl.BlockSpec((1, 2, CMP_STRIDE, HEAD_DIM), lambda a, n: (a, 0, 0, 0)),
            pl.BlockSpec((1, 2, CMP_STRIDE, HEAD_DIM, HEAD_DIM), lambda a, n: (a, 0, 0, 0, 0)),
            pl.BlockSpec((1, HEAD_DIM, HEAD_DIM), lambda a, n: (a, 0, 0)),
        ],
        out_specs=pl.BlockSpec((1, 1, n_chunk, HEAD_DIM), lambda a, n: (a, n, 0, 0)),
        out_shape=jax.ShapeDtypeStruct((2, b * NSA_GROUPS, n_chunk, HEAD_DIM), BF16),
        scratch_shapes=[pltpu.VMEM((s, HEAD_DIM), F32)],
        compiler_params=_params(("parallel", "parallel")),
    )(proj, pos, w1, w2)


POS_HI_COL = 0
POS_LO_COL = 1
BLOCK_COL0 = HEAD_DIM // 2
SLAB_PARTS_PER_HEAD = 2


def _key_extra_columns(s, tk):
    pos = np.arange(s)
    extra = np.zeros((s, HEAD_DIM), np.float32)
    extra[:, POS_HI_COL] = (pos % tk) // SLC_BLOCK
    extra[:, POS_LO_COL] = pos % SLC_BLOCK
    extra[pos, BLOCK_COL0 + pos // SLC_BLOCK] = 1.0
    return jnp.asarray(extra, BF16)


def _nsa_body(q_ref, kc_ref, vc_ref, ks_ref, vs_ref, kw_ref, vw_ref, gate_ref,
              pool_ref, kextra_ref, o_ref, ksa_ref, kwa_ref, m_ref, l_ref, acc_ref, tile_live_ref,
              *, tq, n_slc):
    g = pl.program_id(1)
    i = pl.program_id(2)
    q0 = i * tq
    n_cmp_pad = kc_ref.shape[2]

    @pl.when(i == 0)
    def _():
        ksa_ref[:, :HEAD_DIM] = ks_ref[0]
        ksa_ref[:, HEAD_DIM:] = kextra_ref[...]
        kwa_ref[:, :HEAD_DIM] = kw_ref[0]
        kwa_ref[:, HEAD_DIM:] = kextra_ref[...]

    slopes = [jnp.where(g == 0, 2.0 ** -(r + 1), 2.0 ** -(r + 1 + NSA_HPG)) for r in range(NSA_HPG)]
    q_heads = [q_ref[0, :, r * HEAD_DIM:(r + 1) * HEAD_DIM] for r in range(NSA_HPG)]
    t_row = q0 + lax.broadcasted_iota(jnp.int32, (1, tq), 1)

    wide = NSA_HPG * tq
    head_of_lane = lax.broadcasted_iota(jnp.int32, (1, wide), 1) // tq
    slope_row = jnp.zeros((1, wide), F32)
    for r in range(NSA_HPG):
        slope_row = jnp.where(head_of_lane == r, slopes[r], slope_row)

    def split_heads(x):
        return [x[:, r * tq:(r + 1) * tq] for r in range(NSA_HPG)]

    kc = kc_ref[0, 0]
    vc = vc_ref[0, 0]
    c_idx = lax.broadcasted_iota(jnp.int32, (n_cmp_pad, wide), 0)
    t_wide = q0 + lax.rem(lax.broadcasted_iota(jnp.int32, (1, wide), 1), tq)
    cmp_end = c_idx * CMP_STRIDE + (CMP_BLOCK - 1)
    valid_c = jnp.logical_and(t_wide >= cmp_end, c_idx < n_cmp_pad - 1)
    s = lax.dot_general(kc, jnp.concatenate(q_heads, axis=0), NT_DIMS, preferred_element_type=F32)
    s = jnp.where(valid_c, s + slope_row * (cmp_end - q0).astype(F32), NEG_INF)
    m = jnp.max(s, axis=0, keepdims=True)
    p = jnp.where(valid_c, jnp.exp(s - m), 0.0)
    l = jnp.sum(p, axis=0, keepdims=True)
    p = p * (1.0 / jnp.where(l > 0.0, l, 1.0))
    o_cmp = split_heads(lax.dot_general(vc, p.astype(BF16), TN_DIMS,
                                        preferred_element_type=F32))
    p_grp = functools.reduce(lambda a, b: a + b, split_heads(p))

    pool = pool_ref[...]
    p_grp = p_grp * SCORE_SCALE
    p1 = p_grp.astype(BF16)
    r1 = p_grp - p1.astype(F32)
    p2 = r1.astype(BF16)
    p3 = (r1 - p2.astype(F32)).astype(BF16)
    score = (jnp.dot(pool, p1, preferred_element_type=F32)
             + jnp.dot(pool, p2, preferred_element_type=F32)
             + jnp.dot(pool, p3, preferred_element_type=F32))
    blk = lax.broadcasted_iota(jnp.int32, (n_slc, tq), 0)
    cur = t_row // SLC_BLOCK
    forced = jnp.logical_or(blk == 0, jnp.logical_or(blk == cur, blk == cur - 1))
    score = jnp.where(forced, FORCED_SCORE, score)
    score = jnp.where(blk <= cur, score, NEG_INF)

    sub = 8
    sub_i = lax.broadcasted_iota(jnp.int32, (sub, tq), 0)
    groups = [score[a:a + sub, :] for a in range(0, n_slc, sub)]
    ranks = [jnp.zeros((sub, tq), F32) for _ in groups]
    for ii in range(n_slc):
        row = score[ii:ii + 1, :]
        for gi, grp in enumerate(groups):
            ge = jnp.where(row >= grp, 1.0, 0.0)
            gt = jnp.where(row > grp, 1.0, 0.0)
            if ii < gi * sub:
                beats = ge
            elif ii >= (gi + 1) * sub:
                beats = gt
            else:
                beats = jnp.where(sub_i > ii - gi * sub, ge, gt)
            ranks[gi] = ranks[gi] + beats
    rank = jnp.concatenate(ranks, axis=0)
    sel_bias_t = jnp.where(rank < float(N_SELECT), 0.0, NEG_INF)

    blocks_per_tile = tq // SLC_BLOCK
    for kt in range(n_slc // blocks_per_tile):
        tile_bias = sel_bias_t[kt * blocks_per_tile:(kt + 1) * blocks_per_tile, :]
        tile_live_ref[kt] = (jnp.max(tile_bias) == 0.0).astype(jnp.int32)

    pieces = [jnp.zeros((BLOCK_COL0, tq), F32), sel_bias_t]
    if BLOCK_COL0 + n_slc < HEAD_DIM:
        pieces.append(jnp.zeros((HEAD_DIM - BLOCK_COL0 - n_slc, tq), F32))
    sel_extra = jnp.concatenate(pieces, axis=0).T
    lane = lax.broadcasted_iota(jnp.int32, (tq, HEAD_DIM), 1)

    def query_slab(r, extra):
        pos_cols = jnp.where(lane == POS_HI_COL, slopes[r] * SLC_BLOCK,
                             jnp.where(lane == POS_LO_COL, slopes[r], extra))
        return jnp.concatenate([q_heads[r], pos_cols.astype(BF16)], axis=1)

    q_slc = jnp.concatenate([query_slab(r, sel_extra) for r in range(NSA_HPG)], axis=0)
    q_win = jnp.concatenate([query_slab(r, 0.0) for r in range(NSA_HPG)], axis=0)

    part_w = tq // SLAB_PARTS_PER_HEAD
    parts = [(r * tq + a * part_w, a) for r in range(NSA_HPG) for a in range(SLAB_PARTS_PER_HEAD)]

    def key_rows(d, a, limit):
        lo = max(0, d * tq + a * part_w - limit)
        hi = min(tq - 1, d * tq + (a + 1) * part_w - 1)
        return (lo // part_w) * part_w, (hi // part_w + 1) * part_w

    def distance_mask(d, a, limit, rows):
        lo, hi = rows
        dist_min = d * tq + a * part_w - (hi - 1)
        dist_max = d * tq + (a + 1) * part_w - 1 - lo
        if dist_min >= 0 and dist_max <= limit:
            return None
        key = lo + lax.broadcasted_iota(jnp.int32, (hi - lo, part_w), 0)
        qry = a * part_w + lax.broadcasted_iota(jnp.int32, (hi - lo, part_w), 1)
        dist = d * tq + qry - key
        return jnp.logical_and(dist >= 0, dist <= limit)

    def scores(ka_ref, q_all, j, rows_of):
        k0 = pl.multiple_of(j * tq, tq)
        out = []
        for c0, a in parts:
            lo, hi = rows_of(a)
            ka = ka_ref[pl.ds(k0 + lo, hi - lo), :]
            out.append(lax.dot_general(ka, q_all[c0:c0 + part_w, :], NT_DIMS, preferred_element_type=F32))
        return out

    def reset():
        m_ref[...] = jnp.full((1, wide), NEG_INF, F32)
        l_ref[...] = jnp.zeros((1, wide), F32)
        acc_ref[...] = jnp.zeros((HEAD_DIM, wide), F32)

    def absorb(v_ref, s_parts, j, rows_of, mask_of, live=None):
        k0 = pl.multiple_of(j * tq, tq)
        tile_off = (k0 - q0).astype(F32)
        for (c0, a), s in zip(parts, s_parts):
            cols = slice(c0, c0 + part_w)
            lo, hi = rows_of(a)
            mask = mask_of(a)
            if mask is not None:
                s = jnp.where(mask, s, NEG_INF)
            if live is not None:
                s = jnp.where(live, s, NEG_INF)
            off = slope_row[:, cols] * tile_off
            m = m_ref[:, cols]
            m_new = jnp.maximum(m, jnp.max(s, axis=0, keepdims=True) + off)
            alpha = jnp.exp(m - m_new)
            p = jnp.exp(s - (m_new - off))
            m_ref[:, cols] = m_new
            l_ref[:, cols] = alpha * l_ref[:, cols] + jnp.sum(p, axis=0, keepdims=True)
            acc_ref[:, cols] = alpha * acc_ref[:, cols] + lax.dot_general(
                v_ref[0, pl.ds(k0 + lo, hi - lo), :], p.astype(BF16), TN_DIMS, preferred_element_type=F32)

    def finish():
        return split_heads(acc_ref[...] * (1.0 / l_ref[...]))

    def tile_plan(d, limit):
        rows_of = lambda a: key_rows(d, a, limit)
        return rows_of, (lambda a: distance_mask(d, a, limit, rows_of(a)))

    all_rows = lambda a: (0, tq)
    no_mask = lambda a: None

    reset()
    rows_of, mask_of = tile_plan(0, tq)
    absorb(vs_ref, scores(ksa_ref, q_slc, i, rows_of), i, rows_of, mask_of)

    @pl.loop(0, i)
    def _(j):
        @pl.when(tile_live_ref[j] > 0)
        def _():
            absorb(vs_ref, scores(ksa_ref, q_slc, j, all_rows), j, all_rows, no_mask)

    o_slc = finish()

    reset()
    for d in range((WINDOW - 1) // tq + 2):
        rows_of, mask_of = tile_plan(d, WINDOW - 1)
        j = jnp.maximum(i - d, 0)
        absorb(vw_ref, scores(kwa_ref, q_win, j, rows_of), j, rows_of, mask_of,
               live=None if d == 0 else i - d >= 0)
    o_win = finish()

    sg_t = jax.nn.sigmoid(gate_ref[0]).T

    def gate(branch, r):
        c0 = branch * NSA_HEADS + r
        c1 = c0 + NSA_HPG
        return jnp.where(g == 0, sg_t[c0:c0 + 1, :], sg_t[c1:c1 + 1, :])

    for r in range(NSA_HPG):
        out_t = gate(0, r) * o_cmp[r] + gate(1, r) * o_slc[r] + gate(2, r) * o_win[r]
        o_ref[0, :, r * HEAD_DIM:(r + 1) * HEAD_DIM] = out_t.T.astype(o_ref.dtype)


def _nsa_attention(proj, kvc, gm, *, tq):
    b, s, _ = proj.shape
    n_cmp_pad = s // CMP_STRIDE
    n_slc = s // SLC_BLOCK
    assert s % tq == 0 and tq % SLC_BLOCK == 0 and n_cmp_pad % LANES == 0
    assert BLOCK_COL0 + n_slc <= HEAD_DIM and tq // SLC_BLOCK <= 256 and n_slc % 8 == 0
    ratio = SLC_BLOCK // CMP_STRIDE
    span = CMP_BLOCK // CMP_STRIDE
    pool = np.zeros((n_slc, n_cmp_pad), np.float32)
    for jj in range(n_slc):
        for mm in range(ratio):
            for nn in range(span):
                c = ratio * jj + mm + nn
                if c < n_cmp_pad:
                    pool[jj, c] += 1.0
    grp_w = NSA_HPG * HEAD_DIM
    kv_spec = lambda col: pl.BlockSpec((1, s, HEAD_DIM), lambda bb, g, i: (bb, 0, col + g))
    cmp_spec = lambda a: pl.BlockSpec((1, 1, n_cmp_pad, HEAD_DIM),
                                      lambda bb, g, i: (a, bb * NSA_GROUPS + g, 0, 0))
    return pl.pallas_call(
        functools.partial(_nsa_body, tq=tq, n_slc=n_slc),
        name="nsa_attn",
        grid=(b, NSA_GROUPS, s // tq),
        in_specs=[
            pl.BlockSpec((1, tq, grp_w), lambda bb, g, i: (bb, i, COL_Q_NSA // NSA_HPG + g)),
            cmp_spec(0), cmp_spec(1),
            kv_spec(COL_K_SLC), kv_spec(COL_V_SLC), kv_spec(COL_K_WIN), kv_spec(COL_V_WIN),
            pl.BlockSpec((1, tq, GATE_PAD), lambda bb, g, i: (bb, i, gm.shape[2] // GATE_PAD - 1)),
            pl.BlockSpec((n_slc, n_cmp_pad), lambda bb, g, i: (0, 0)),
            pl.BlockSpec((s, HEAD_DIM), lambda bb, g, i: (0, 0)),
        ],
        out_specs=pl.BlockSpec((1, tq, grp_w), lambda bb, g, i: (bb, i, g)),
        out_shape=jax.ShapeDtypeStruct((b, s, NSA_HEADS * HEAD_DIM), BF16),
        scratch_shapes=[pltpu.VMEM((s, 2 * HEAD_DIM), BF16), pltpu.VMEM((s, 2 * HEAD_DIM), BF16),
                        pltpu.VMEM((1, NSA_HPG * tq), F32), pltpu.VMEM((1, NSA_HPG * tq), F32),
                        pltpu.VMEM((HEAD_DIM, NSA_HPG * tq), F32),
                        pltpu.SMEM((s // tq,), jnp.int32)],
        compiler_params=_params(("parallel", "parallel", "arbitrary")),
    )(proj, kvc, kvc, proj, proj, proj, proj, gm, jnp.asarray(pool, BF16), _key_extra_columns(s, tq))


def _merge_body(sb_ref, nsa_ref, m0_ref, m1_ref, x_ref, wsb_ref, wnsa_ref, wout_ref, g_ref, o_ref):
    y_sb = jnp.dot(sb_ref[...], wsb_ref[...], preferred_element_type=F32)
    y_nsa = jnp.dot(nsa_ref[...], wnsa_ref[...], preferred_element_type=F32)
    merged = jax.nn.sigmoid(m0_ref[...]) * y_sb + jax.nn.sigmoid(m1_ref[...]) * y_nsa
    y = jnp.dot(merged.astype(BF16), wout_ref[...], preferred_element_type=F32)
    o_ref[...] = x_ref[...] + _rms(y, g_ref[...])


def _merge_out(sb, nsa, gm, x, w_sb, w_nsa, w_out, g_post, *, tm):
    m, d = x.shape
    assert m % tm == 0 and gm.shape[0] == m and gm.shape[1] >= 2 * d

    def resident(shape):
        return pl.BlockSpec(shape, lambda i: (0, 0), pipeline_mode=pl.Buffered(1))

    return pl.pallas_call(
        _merge_body,
        name="merge_out",
        grid=(m // tm,),
        in_specs=[
            pl.BlockSpec((tm, sb.shape[1]), lambda i: (i, 0)),
            pl.BlockSpec((tm, nsa.shape[1]), lambda i: (i, 0)),
            pl.BlockSpec((tm, d), lambda i: (i, 0)),
            pl.BlockSpec((tm, d), lambda i: (i, 1)),
            pl.BlockSpec((tm, d), lambda i: (i, 0)),
            resident(w_sb.shape), resident(w_nsa.shape), resident(w_out.shape),
            pl.BlockSpec((1, d), lambda i: (0, 0)),
        ],
        out_specs=pl.BlockSpec((tm, d), lambda i: (i, 0)),
        out_shape=jax.ShapeDtypeStruct((m, d), F32),
        compiler_params=_params(("parallel",)),
    )(sb, nsa, gm, gm, x, w_sb, w_nsa, w_out, g_post)


def kernel(x, ffn1_pre_g, ffn1_w_in, ffn1_w_out, ffn1_post_g, mix_pre_g, w_in, cmp_pos_k, cmp_k_w1, cmp_k_w2, cmp_pos_v, cmp_v_w1, cmp_v_w2, w_branch_sb, w_branch_nsa, w_out, mix_post_g, ffn2_pre_g, ffn2_w_in, ffn2_w_out, ffn2_post_g):
    b, s, d = x.shape
    m = b * s
    depth = ffn1_pre_g.shape[0]
    h = x.reshape(m, d)
    for l in range(depth):
        h = _ffn(h, ffn1_pre_g[l][None], ffn1_w_in[l], ffn1_w_out[l], ffn1_post_g[l][None], tm=1024, tf=256)

        g_mix = mix_pre_g[l][None]
        col_scale = np.ones((w_in.shape[2],), np.float32)
        col_scale[COL_Q_SB * HEAD_DIM:(COL_Q_SB + SB_HEADS) * HEAD_DIM] = HEAD_DIM ** -0.5
        col_scale[COL_Q_NSA * HEAD_DIM:(COL_Q_NSA + NSA_HEADS) * HEAD_DIM] = HEAD_DIM ** -0.5
        wt_bf = (jnp.transpose(w_in[l]) * jnp.asarray(col_scale)[:, None]).astype(BF16)
        gate_end = QKV_COLS + N_GATE_LOGITS
        wt_logits = jnp.concatenate([wt_bf[gate_end:], wt_bf[QKV_COLS:gate_end],
                                     jnp.zeros((GATE_PAD - N_GATE_LOGITS, d), BF16)], axis=0)
        proj = _norm_matmul(h, g_mix, wt_bf, BF16, tm=1024, tn=_col_tile(QKV_COLS), n_out=QKV_COLS,
                            name="in_proj_qkv")
        logits = _norm_matmul(h, g_mix, wt_logits, F32, tm=1024, tn=_col_tile(2 * d + GATE_PAD),
                              n_out=2 * d + GATE_PAD, name="in_proj_logits")
        proj = proj.reshape(b, s, QKV_COLS)

        sb = _sb_attention(proj, tq=256, heads=8)

        pos = jnp.stack([cmp_pos_k[l], cmp_pos_v[l]]).reshape(2, 2, CMP_STRIDE, HEAD_DIM)
        w1 = jnp.stack([cmp_k_w1[l], cmp_v_w1[l]]).astype(BF16).reshape(2, 2, CMP_STRIDE, HEAD_DIM, HEAD_DIM)
        w2 = jnp.stack([cmp_k_w2[l], cmp_v_w2[l]]).astype(BF16)
        kvc = _compress(proj, pos, w1, w2)

        nsa = _nsa_attention(proj, kvc,
                             logits.reshape(b, s, 2 * d + GATE_PAD), tq=512)

        h = _merge_out(sb.reshape(m, -1), nsa.reshape(m, -1), logits, h,
                       w_branch_sb[l].astype(BF16), w_branch_nsa[l].astype(BF16), w_out[l].astype(BF16),
                       mix_post_g[l][None], tm=256)

        h = _ffn(h, ffn2_pre_g[l][None], ffn2_w_in[l], ffn2_w_out[l], ffn2_post_g[l][None], tm=1024, tf=256)
    return h.reshape(b, s, d)
```

```python
import functools

import numpy as np
import jax
import jax.numpy as jnp
from jax import lax
from jax.experimental import pallas as pl
from jax.experimental.pallas import tpu as pltpu

HEAD_DIM = 128
SB_HEADS = 8
NSA_HEADS = 8
NSA_GROUPS = 2
NSA_HPG = NSA_HEADS // NSA_GROUPS
CMP_BLOCK = 32
CMP_STRIDE = 16
SLC_BLOCK = 64
N_SELECT = 16
WINDOW = 512
NORM_EPS = 1e-6
NEG_INF = -1e30
SCORE_SCALE = 2.0 ** 64
FORCED_SCORE = 1e30

QKV_COLS = (3 * SB_HEADS + NSA_HEADS + 6 * NSA_GROUPS) * HEAD_DIM
COL_Q_SB = 0
COL_K_SB = SB_HEADS
COL_V_SB = 2 * SB_HEADS
COL_Q_NSA = 3 * SB_HEADS
COL_K_CMP = COL_Q_NSA + NSA_HEADS
COL_K_SLC = COL_K_CMP + 2 * NSA_GROUPS
COL_V_SLC = COL_K_SLC + NSA_GROUPS
COL_K_WIN = COL_V_SLC + NSA_GROUPS
COL_V_WIN = COL_K_WIN + NSA_GROUPS
N_GATE_LOGITS = 3 * NSA_HEADS
GATE_PAD = 128

LANES = 128
V7X_VMEM_BYTES = 64 * 1024 * 1024
V7X_VMEM_LIMIT = V7X_VMEM_BYTES * 7 // 8
V7X_VMEM_LIMIT_FFN = V7X_VMEM_BYTES * 15 // 16

FFN_TOKEN_TILE = 1024
FFN_CHUNK = 256
PROJ_TOKEN_TILE = 1024
MERGE_TOKEN_TILE = 256
SB_QUERY_TILE = 256
SB_HEADS_PER_STEP = 8
NSA_QUERY_TILE = 512

F32 = jnp.float32
BF16 = jnp.bfloat16
NT_DIMS = (((1,), (1,)), ((), ()))


def _params(semantics, vmem_limit=V7X_VMEM_LIMIT):
    return pltpu.CompilerParams(dimension_semantics=semantics, vmem_limit_bytes=vmem_limit)


def _rms(x, g):
    ms = jnp.mean(x * x, axis=-1, keepdims=True)
    return x * lax.rsqrt(ms + NORM_EPS) * g


def _ffn_body(x_ref, gpre_ref, wg_ref, wu_ref, wo_ref, gpost_ref, o_ref, h_ref):
    j = pl.program_id(1)

    @pl.when(j == 0)
    def _():
        h_ref[...] = _rms(x_ref[...], gpre_ref[...]).astype(BF16)
        o_ref[...] = jnp.zeros_like(o_ref)

    h = h_ref[...]
    gate = jnp.dot(h, wg_ref[...].astype(BF16), preferred_element_type=F32)
    up = jnp.dot(h, wu_ref[...].astype(BF16), preferred_element_type=F32)
    act = (gate * jax.nn.sigmoid(gate)) * up
    o_ref[...] += jnp.dot(act.astype(BF16), wo_ref[...].astype(BF16), preferred_element_type=F32)

    @pl.when(j == pl.num_programs(1) - 1)
    def _():
        o_ref[...] = x_ref[...] + 0.5 * _rms(o_ref[...], gpost_ref[...])


def _ffn(x, g_pre, w_in, w_out, g_post, *, tm, tf):
    m, d = x.shape
    f = w_out.shape[0]
    nf = f // tf
    assert m % tm == 0 and f % tf == 0 and w_in.shape == (d, 2 * f)
    return pl.pallas_call(
        _ffn_body,
        name="ffn",
        grid=(m // tm, nf),
        in_specs=[
            pl.BlockSpec((tm, d), lambda i, j: (i, 0)),
            pl.BlockSpec((1, d), lambda i, j: (0, 0)),
            pl.BlockSpec((d, tf), lambda i, j: (0, j)),
            pl.BlockSpec((d, tf), lambda i, j: (0, j + nf)),
            pl.BlockSpec((tf, d), lambda i, j: (j, 0)),
            pl.BlockSpec((1, d), lambda i, j: (0, 0)),
        ],
        out_specs=pl.BlockSpec((tm, d), lambda i, j: (i, 0)),
        out_shape=jax.ShapeDtypeStruct((m, d), F32),
        scratch_shapes=[pltpu.VMEM((tm, d), BF16)],
        compiler_params=_params(("parallel", "arbitrary"), V7X_VMEM_LIMIT_FFN),
    )(x, g_pre, w_in, w_in, w_out, g_post)


def _norm_matmul_body(x_ref, g_ref, wt_ref, o_ref, h_ref):
    @pl.when(pl.program_id(1) == 0)
    def _():
        h_ref[...] = _rms(x_ref[...], g_ref[...]).astype(BF16)

    o_ref[...] = lax.dot_general(h_ref[...], wt_ref[...], NT_DIMS,
                                 preferred_element_type=F32).astype(o_ref.dtype)


MAX_COL_TILE = 1536


def _col_tile(n):
    assert n % LANES == 0
    blocks = n // LANES
    best = max(k for k in range(1, MAX_COL_TILE // LANES + 1) if blocks % k == 0)
    return best * LANES


def _norm_matmul(x, g, wt, out_dtype, *, tm, tn, n_out, name, col0=0):
    m, d = x.shape
    n = n_out
    assert m % tm == 0 and n % tn == 0 and col0 % tn == 0 and col0 + n <= wt.shape[0]
    jb = col0 // tn
    return pl.pallas_call(
        _norm_matmul_body,
        name=name,
        grid=(m // tm, n // tn),
        in_specs=[
            pl.BlockSpec((tm, d), lambda i, j: (i, 0)),
            pl.BlockSpec((1, d), lambda i, j: (0, 0)),
            pl.BlockSpec((tn, d), lambda i, j: (j + jb, 0)),
        ],
        out_specs=pl.BlockSpec((tm, tn), lambda i, j: (i, j)),
        out_shape=jax.ShapeDtypeStruct((m, n), out_dtype),
        scratch_shapes=[pltpu.VMEM((tm, d), BF16)],
        compiler_params=_params(("parallel", "arbitrary")),
    )(x, g, wt)


EXP_UNDERFLOW = 104.0
TN_DIMS = (((0,), (0,)), ((), ()))
LOG2_E = 1.4426950408889634


def _sb_body(q_ref, k_ref, v_ref, tri_ref, o_ref, c_ref, acc_ref, *, tq, heads):
    i = pl.program_id(2)
    wide = heads * tq
    tri = tri_ref[...]
    key_i = lax.broadcasted_iota(jnp.int32, (tq, wide), 0)
    qry_i = lax.rem(lax.broadcasted_iota(jnp.int32, (tq, wide), 1), tq)
    past = key_i < qry_i
    head_lanes = [slice(h * HEAD_DIM, (h + 1) * HEAD_DIM) for h in range(heads)]

    def tile(j, diag):
        k0 = pl.multiple_of(j * tq, tq)
        z = jnp.concatenate(
            [lax.dot_general(k_ref[0, pl.ds(k0, tq), hl], q_ref[0, :, hl], NT_DIMS,
                             preferred_element_type=F32) for hl in head_lanes], axis=1)
        softplus = jnp.maximum(z, 0.0) + jnp.log(1.0 + jnp.exp2(jnp.abs(z) * (-LOG2_E)))
        log_beta = z - softplus
        if diag:
            softplus = jnp.where(past, softplus, 0.0)
        hi = softplus.astype(BF16)
        lo = (softplus - hi.astype(F32)).astype(BF16)
        later = jnp.dot(tri, jnp.concatenate([hi, lo], axis=0), preferred_element_type=F32)
        c = c_ref[...]
        w = jnp.exp(log_beta - c - later)
        if diag:
            w = jnp.where(past, w, 0.0)
        w = w.astype(BF16)
        for h, hl in enumerate(head_lanes):
            cols = slice(h * tq, (h + 1) * tq)
            acc_ref[:, cols] += lax.dot_general(v_ref[0, pl.ds(k0, tq), hl], w[:, cols], TN_DIMS,
                                                preferred_element_type=F32)
        c = c + jnp.sum(softplus, axis=0, keepdims=True)
        c_ref[...] = c
        return (jnp.min(c) <= EXP_UNDERFLOW).astype(jnp.int32)

    c_ref[...] = jnp.zeros_like(c_ref)
    acc_ref[...] = jnp.zeros_like(acc_ref)
    alive = tile(i, True)

    def keep_going(carry):
        j, alive = carry
        return jnp.logical_and(j >= 0, alive > 0)

    lax.while_loop(keep_going, lambda carry: (carry[0] - 1, tile(carry[0], False)), (i - 1, alive))
    for h, hl in enumerate(head_lanes):
        o_ref[0, :, hl] = acc_ref[:, h * tq:(h + 1) * tq].T.astype(o_ref.dtype)


def _sb_attention(proj, *, tq, heads):
    b, s, _ = proj.shape
    assert s % tq == 0 and SB_HEADS % heads == 0
    upper = np.triu(np.ones((tq, tq), np.float32), 1)
    tri = jnp.asarray(np.concatenate([upper, upper], axis=1), BF16)
    width = heads * HEAD_DIM
    return pl.pallas_call(
        functools.partial(_sb_body, tq=tq, heads=heads),
        name="sb_attn",
        grid=(b, SB_HEADS // heads, s // tq),
        in_specs=[
            pl.BlockSpec((1, tq, width), lambda bb, h, i: (bb, i, COL_Q_SB // heads + h)),
            pl.BlockSpec((1, s, width), lambda bb, h, i: (bb, 0, COL_K_SB // heads + h)),
            pl.BlockSpec((1, s, width), lambda bb, h, i: (bb, 0, COL_V_SB // heads + h)),
            pl.BlockSpec((tq, 2 * tq), lambda bb, h, i: (0, 0)),
        ],
        out_specs=pl.BlockSpec((1, tq, width), lambda bb, h, i: (bb, i, h)),
        out_shape=jax.ShapeDtypeStruct((b, s, SB_HEADS * HEAD_DIM), BF16),
        scratch_shapes=[pltpu.VMEM((1, heads * tq), F32), pltpu.VMEM((HEAD_DIM, heads * tq), F32)],
        compiler_params=_params(("parallel", "parallel", "arbitrary")),
    )(proj, proj, proj, tri)


def _compress_body(x_ref, pos_ref, w1_ref, w2_ref, o_ref, x32_ref):
    n_chunk = o_ref.shape[2]
    x32_ref[...] = x_ref[0].astype(F32)
    first = jnp.zeros((n_chunk, HEAD_DIM), F32)
    second = jnp.zeros((n_chunk, HEAD_DIM), F32)
    for l in range(CMP_STRIDE):
        x_l = x32_ref[pl.ds(l, n_chunk, stride=CMP_STRIDE), :]
        xa = (x_l + pos_ref[0, 0, l:l + 1, :]).astype(BF16)
        xb = (x_l + pos_ref[0, 1, l:l + 1, :]).astype(BF16)
        first = first + jnp.dot(xa, w1_ref[0, 0, l], preferred_element_type=F32)
        second = second + jnp.dot(xb, w1_ref[0, 1, l], preferred_element_type=F32)
    pre = first + pltpu.roll(second, n_chunk - 1, 0)
    y = jax.nn.gelu(pre).astype(BF16)
    o_ref[0, 0] = jnp.dot(y, w2_ref[0], preferred_element_type=F32).astype(o_ref.dtype)


def _compress(proj, pos, w1, w2):
    b, s, _ = proj.shape
    n_chunk = s // CMP_STRIDE
    return pl.pallas_call(
        _compress_body,
        name="compress",
        grid=(2, b * NSA_GROUPS),
        in_specs=[
            pl.BlockSpec((1, s, HEAD_DIM),
                         lambda a, n: (n // NSA_GROUPS, 0, COL_K_CMP + a * NSA_GROUPS + n % NSA_GROUPS)),
            pl.BlockSpec((1, 2, CMP_STRIDE, HEAD_DIM), lambda a, n: (a, 0, 0, 0)),
            pl.BlockSpec((1, 2, CMP_STRIDE, HEAD_DIM, HEAD_DIM), lambda a, n: (a, 0, 0, 0, 0)),
            pl.BlockSpec((1, HEAD_DIM, HEAD_DIM), lambda a, n: (a, 0, 0)),
        ],
        out_specs=pl.BlockSpec((1, 1, n_chunk, HEAD_DIM), lambda a, n: (a, n, 0, 0)),
        out_shape=jax.ShapeDtypeStruct((2, b * NSA_GROUPS, n_chunk, HEAD_DIM), BF16),
        scratch_shapes=[pltpu.VMEM((s, HEAD_DIM), F32)],
        compiler_params=_params(("parallel", "parallel")),
    )(proj, pos, w1, w2)


POS_HI_COL = 0
POS_LO_COL = 1
BLOCK_COL0 = HEAD_DIM // 2
SLAB_PARTS_PER_HEAD = 2


def _key_extra_columns(s, tk):
    pos = np.arange(s)
    extra = np.zeros((s, HEAD_DIM), np.float32)
    extra[:, POS_HI_COL] = (pos % tk) // SLC_BLOCK
    extra[:, POS_LO_COL] = pos % SLC_BLOCK
    extra[pos, BLOCK_COL0 + pos // SLC_BLOCK] = 1.0
    return jnp.asarray(extra, BF16)


def _nsa_body(q_ref, kc_ref, vc_ref, ks_ref, vs_ref, kw_ref, vw_ref, gate_ref,
              pool_ref, kextra_ref, o_ref, ksa_ref, kwa_ref, m_ref, l_ref, acc_ref, half_live_ref,
              *, tq, n_slc):
    g = pl.program_id(1)
    i = pl.program_id(2)
    q0 = i * tq
    n_cmp_pad = kc_ref.shape[2]

    @pl.when(i == 0)
    def _():
        ksa_ref[:, :HEAD_DIM] = ks_ref[0]
        ksa_ref[:, HEAD_DIM:] = kextra_ref[...]
        kwa_ref[:, :HEAD_DIM] = kw_ref[0]
        kwa_ref[:, HEAD_DIM:] = kextra_ref[...]

    slopes = [jnp.where(g == 0, 2.0 ** -(r + 1), 2.0 ** -(r + 1 + NSA_HPG)) for r in range(NSA_HPG)]
    q_heads = [q_ref[0, :, r * HEAD_DIM:(r + 1) * HEAD_DIM] for r in range(NSA_HPG)]
    t_row = q0 + lax.broadcasted_iota(jnp.int32, (1, tq), 1)

    wide = NSA_HPG * tq
    head_of_lane = lax.broadcasted_iota(jnp.int32, (1, wide), 1) // tq
    slope_row = jnp.zeros((1, wide), F32)
    for r in range(NSA_HPG):
        slope_row = jnp.where(head_of_lane == r, slopes[r], slope_row)

    def split_heads(x):
        return [x[:, r * tq:(r + 1) * tq] for r in range(NSA_HPG)]

    kc = kc_ref[0, 0]
    vc = vc_ref[0, 0]
    c_idx = lax.broadcasted_iota(jnp.int32, (n_cmp_pad, wide), 0)
    t_wide = q0 + lax.rem(lax.broadcasted_iota(jnp.int32, (1, wide), 1), tq)
    cmp_end = c_idx * CMP_STRIDE + (CMP_BLOCK - 1)
    valid_c = jnp.logical_and(t_wide >= cmp_end, c_idx < n_cmp_pad - 1)
    s = lax.dot_general(kc, jnp.concatenate(q_heads, axis=0), NT_DIMS, preferred_element_type=F32)
    s = jnp.where(valid_c, s + slope_row * (cmp_end - q0).astype(F32), NEG_INF)
    m = jnp.max(s, axis=0, keepdims=True)
    p = jnp.where(valid_c, jnp.exp(s - m), 0.0)
    l = jnp.sum(p, axis=0, keepdims=True)
    p = p * (1.0 / jnp.where(l > 0.0, l, 1.0))
    o_cmp = split_heads(lax.dot_general(vc, p.astype(BF16), TN_DIMS,
                                        preferred_element_type=F32))
    p_grp = functools.reduce(lambda a, b: a + b, split_heads(p))

    pool = pool_ref[...]
    p_grp = p_grp * SCORE_SCALE
    p1 = p_grp.astype(BF16)
    r1 = p_grp - p1.astype(F32)
    p2 = r1.astype(BF16)
    p3 = (r1 - p2.astype(F32)).astype(BF16)
    score = (jnp.dot(pool, p1, preferred_element_type=F32)
             + jnp.dot(pool, p2, preferred_element_type=F32)
             + jnp.dot(pool, p3, preferred_element_type=F32))
    blk = lax.broadcasted_iota(jnp.int32, (n_slc, tq), 0)
    cur = t_row // SLC_BLOCK
    forced = jnp.logical_or(blk == 0, jnp.logical_or(blk == cur, blk == cur - 1))
    score = jnp.where(forced, FORCED_SCORE, score)
    score = jnp.where(blk <= cur, score, NEG_INF)

    sub = 8
    sub_i = lax.broadcasted_iota(jnp.int32, (sub, tq), 0)
    groups = [score[a:a + sub, :] for a in range(0, n_slc, sub)]
    ranks = [jnp.zeros((sub, tq), F32) for _ in groups]
    for ii in range(n_slc):
        row = score[ii:ii + 1, :]
        for gi, grp in enumerate(groups):
            ge = jnp.where(row >= grp, 1.0, 0.0)
            gt = jnp.where(row > grp, 1.0, 0.0)
            if ii < gi * sub:
                beats = ge
            elif ii >= (gi + 1) * sub:
                beats = gt
            else:
                beats = jnp.where(sub_i > ii - gi * sub, ge, gt)
            ranks[gi] = ranks[gi] + beats
    rank = jnp.concatenate(ranks, axis=0)
    sel_bias_t = jnp.where(rank < float(N_SELECT), 0.0, NEG_INF)

    blocks_per_half = tq // 2 // SLC_BLOCK
    for kh in range(n_slc // blocks_per_half):
        half_bias = sel_bias_t[kh * blocks_per_half:(kh + 1) * blocks_per_half, :]
        half_live_ref[kh] = (jnp.max(half_bias) == 0.0).astype(jnp.int32)

    pieces = [jnp.zeros((BLOCK_COL0, tq), F32), sel_bias_t]
    if BLOCK_COL0 + n_slc < HEAD_DIM:
        pieces.append(jnp.zeros((HEAD_DIM - BLOCK_COL0 - n_slc, tq), F32))
    sel_extra = jnp.concatenate(pieces, axis=0).T
    lane = lax.broadcasted_iota(jnp.int32, (tq, HEAD_DIM), 1)

    def query_slab(r, extra):
        pos_cols = jnp.where(lane == POS_HI_COL, slopes[r] * SLC_BLOCK,
                             jnp.where(lane == POS_LO_COL, slopes[r], extra))
        return jnp.concatenate([q_heads[r], pos_cols.astype(BF16)], axis=1)

    q_slc = jnp.concatenate([query_slab(r, sel_extra) for r in range(NSA_HPG)], axis=0)
    q_win = jnp.concatenate([query_slab(r, 0.0) for r in range(NSA_HPG)], axis=0)

    part_w = tq // SLAB_PARTS_PER_HEAD
    parts = [(r * tq + a * part_w, a) for r in range(NSA_HPG) for a in range(SLAB_PARTS_PER_HEAD)]

    def key_rows(d, a, limit):
        lo = max(0, d * tq + a * part_w - limit)
        hi = min(tq - 1, d * tq + (a + 1) * part_w - 1)
        return (lo // part_w) * part_w, (hi // part_w + 1) * part_w

    def distance_mask(d, a, limit, rows):
        lo, hi = rows
        dist_min = d * tq + a * part_w - (hi - 1)
        dist_max = d * tq + (a + 1) * part_w - 1 - lo
        if dist_min >= 0 and dist_max <= limit:
            return None
        key = lo + lax.broadcasted_iota(jnp.int32, (hi - lo, part_w), 0)
        qry = a * part_w + lax.broadcasted_iota(jnp.int32, (hi - lo, part_w), 1)
        dist = d * tq + qry - key
        return jnp.logical_and(dist >= 0, dist <= limit)

    def scores(ka_ref, q_all, j, rows_of):
        k0 = pl.multiple_of(j * tq, tq)
        out = []
        for c0, a in parts:
            lo, hi = rows_of(a)
            ka = ka_ref[pl.ds(k0 + lo, hi - lo), :]
            out.append(lax.dot_general(ka, q_all[c0:c0 + part_w, :], NT_DIMS, preferred_element_type=F32))
        return out

    def reset():
        m_ref[...] = jnp.full((1, wide), NEG_INF, F32)
        l_ref[...] = jnp.zeros((1, wide), F32)
        acc_ref[...] = jnp.zeros((HEAD_DIM, wide), F32)

    def absorb(v_ref, s_parts, j, rows_of, mask_of, live=None):
        k0 = pl.multiple_of(j * tq, tq)
        tile_off = (k0 - q0).astype(F32)
        for (c0, a), s in zip(parts, s_parts):
            cols = slice(c0, c0 + part_w)
            lo, hi = rows_of(a)
            mask = mask_of(a)
            if mask is not None:
                s = jnp.where(mask, s, NEG_INF)
            if live is not None:
                s = jnp.where(live, s, NEG_INF)
            off = slope_row[:, cols] * tile_off
            m = m_ref[:, cols]
            m_new = jnp.maximum(m, jnp.max(s, axis=0, keepdims=True) + off)
            alpha = jnp.exp(m - m_new)
            p = jnp.exp(s - (m_new - off))
            m_ref[:, cols] = m_new
            l_ref[:, cols] = alpha * l_ref[:, cols] + jnp.sum(p, axis=0, keepdims=True)
            acc_ref[:, cols] = alpha * acc_ref[:, cols] + lax.dot_general(
                v_ref[0, pl.ds(k0 + lo, hi - lo), :], p.astype(BF16), TN_DIMS, preferred_element_type=F32)

    def finish():
        return split_heads(acc_ref[...] * (1.0 / l_ref[...]))

    def tile_plan(d, limit):
        rows_of = lambda a: key_rows(d, a, limit)
        return rows_of, (lambda a: distance_mask(d, a, limit, rows_of(a)))

    all_rows = lambda a: (0, tq)
    no_mask = lambda a: None

    reset()
    rows_of, mask_of = tile_plan(0, tq)
    absorb(vs_ref, scores(ksa_ref, q_slc, i, rows_of), i, rows_of, mask_of)

    @pl.loop(0, i)
    def _(j):
        first_live = half_live_ref[2 * j] > 0
        second_live = half_live_ref[2 * j + 1] > 0
        first_half = lambda a: (0, tq // 2)
        second_half = lambda a: (tq // 2, tq)
        for rows_of, wanted in ((all_rows, jnp.logical_and(first_live, second_live)),
                                (first_half, jnp.logical_and(first_live, jnp.logical_not(second_live))),
                                (second_half, jnp.logical_and(jnp.logical_not(first_live), second_live))):
            pl.when(wanted)(functools.partial(
                lambda rows: absorb(vs_ref, scores(ksa_ref, q_slc, j, rows), j, rows, no_mask), rows_of))

    o_slc = finish()

    reset()
    for d in range((WINDOW - 1) // tq + 2):
        rows_of, mask_of = tile_plan(d, WINDOW - 1)
        j = jnp.maximum(i - d, 0)
        absorb(vw_ref, scores(kwa_ref, q_win, j, rows_of), j, rows_of, mask_of,
               live=None if d == 0 else i - d >= 0)
    o_win = finish()

    sg_t = jax.nn.sigmoid(gate_ref[0]).T

    def gate(branch, r):
        c0 = branch * NSA_HEADS + r
        c1 = c0 + NSA_HPG
        return jnp.where(g == 0, sg_t[c0:c0 + 1, :], sg_t[c1:c1 + 1, :])

    for r in range(NSA_HPG):
        out_t = gate(0, r) * o_cmp[r] + gate(1, r) * o_slc[r] + gate(2, r) * o_win[r]
        o_ref[0, :, r * HEAD_DIM:(r + 1) * HEAD_DIM] = out_t.T.astype(o_ref.dtype)


def _nsa_attention(proj, kvc, gm, *, tq):
    b, s, _ = proj.shape
    n_cmp_pad = s // CMP_STRIDE
    n_slc = s // SLC_BLOCK
    assert s % tq == 0 and tq % SLC_BLOCK == 0 and n_cmp_pad % LANES == 0
    assert BLOCK_COL0 + n_slc <= HEAD_DIM and tq // SLC_BLOCK <= 256 and n_slc % 8 == 0
    ratio = SLC_BLOCK // CMP_STRIDE
    span = CMP_BLOCK // CMP_STRIDE
    pool = np.zeros((n_slc, n_cmp_pad), np.float32)
    for jj in range(n_slc):
        for mm in range(ratio):
            for nn in range(span):
                c = ratio * jj + mm + nn
                if c < n_cmp_pad:
                    pool[jj, c] += 1.0
    grp_w = NSA_HPG * HEAD_DIM
    kv_spec = lambda col: pl.BlockSpec((1, s, HEAD_DIM), lambda bb, g, i: (bb, 0, col + g))
    cmp_spec = lambda a: pl.BlockSpec((1, 1, n_cmp_pad, HEAD_DIM),
                                      lambda bb, g, i: (a, bb * NSA_GROUPS + g, 0, 0))
    return pl.pallas_call(
        functools.partial(_nsa_body, tq=tq, n_slc=n_slc),
        name="nsa_attn",
        grid=(b, NSA_GROUPS, s // tq),
        in_specs=[
            pl.BlockSpec((1, tq, grp_w), lambda bb, g, i: (bb, i, COL_Q_NSA // NSA_HPG + g)),
            cmp_spec(0), cmp_spec(1),
            kv_spec(COL_K_SLC), kv_spec(COL_V_SLC), kv_spec(COL_K_WIN), kv_spec(COL_V_WIN),
            pl.BlockSpec((1, tq, GATE_PAD), lambda bb, g, i: (bb, i, gm.shape[2] // GATE_PAD - 1)),
            pl.BlockSpec((n_slc, n_cmp_pad), lambda bb, g, i: (0, 0)),
            pl.BlockSpec((s, HEAD_DIM), lambda bb, g, i: (0, 0)),
        ],
        out_specs=pl.BlockSpec((1, tq, grp_w), lambda bb, g, i: (bb, i, g)),
        out_shape=jax.ShapeDtypeStruct((b, s, NSA_HEADS * HEAD_DIM), BF16),
        scratch_shapes=[pltpu.VMEM((s, 2 * HEAD_DIM), BF16), pltpu.VMEM((s, 2 * HEAD_DIM), BF16),
                        pltpu.VMEM((1, NSA_HPG * tq), F32), pltpu.VMEM((1, NSA_HPG * tq), F32),
                        pltpu.VMEM((HEAD_DIM, NSA_HPG * tq), F32),
                        pltpu.SMEM((2 * (s // tq),), jnp.int32)],
        compiler_params=_params(("parallel", "parallel", "arbitrary")),
    )(proj, kvc, kvc, proj, proj, proj, proj, gm, jnp.asarray(pool, BF16), _key_extra_columns(s, tq))


def _merge_body(sb_ref, nsa_ref, m0_ref, m1_ref, x_ref, wsb_ref, wnsa_ref, wout_ref, g_ref, o_ref):
    y_sb = jnp.dot(sb_ref[...], wsb_ref[...], preferred_element_type=F32)
    y_nsa = jnp.dot(nsa_ref[...], wnsa_ref[...], preferred_element_type=F32)
    merged = jax.nn.sigmoid(m0_ref[...]) * y_sb + jax.nn.sigmoid(m1_ref[...]) * y_nsa
    y = jnp.dot(merged.astype(BF16), wout_ref[...], preferred_element_type=F32)
    o_ref[...] = x_ref[...] + _rms(y, g_ref[...])


def _merge_out(sb, nsa, gm, x, w_sb, w_nsa, w_out, g_post, *, tm):
    m, d = x.shape
    assert m % tm == 0 and gm.shape[0] == m and gm.shape[1] >= 2 * d

    def resident(shape):
        return pl.BlockSpec(shape, lambda i: (0, 0), pipeline_mode=pl.Buffered(1))

    return pl.pallas_call(
        _merge_body,
        name="merge_out",
        grid=(m // tm,),
        in_specs=[
            pl.BlockSpec((tm, sb.shape[1]), lambda i: (i, 0)),
            pl.BlockSpec((tm, nsa.shape[1]), lambda i: (i, 0)),
            pl.BlockSpec((tm, d), lambda i: (i, 0)),
            pl.BlockSpec((tm, d), lambda i: (i, 1)),
            pl.BlockSpec((tm, d), lambda i: (i, 0)),
            resident(w_sb.shape), resident(w_nsa.shape), resident(w_out.shape),
            pl.BlockSpec((1, d), lambda i: (0, 0)),
        ],
        out_specs=pl.BlockSpec((tm, d), lambda i: (i, 0)),
        out_shape=jax.ShapeDtypeStruct((m, d), F32),
        compiler_params=_params(("parallel",)),
    )(sb, nsa, gm, gm, x, w_sb, w_nsa, w_out, g_post)


def kernel(x, ffn1_pre_g, ffn1_w_in, ffn1_w_out, ffn1_post_g, mix_pre_g, w_in, cmp_pos_k, cmp_k_w1, cmp_k_w2, cmp_pos_v, cmp_v_w1, cmp_v_w2, w_branch_sb, w_branch_nsa, w_out, mix_post_g, ffn2_pre_g, ffn2_w_in, ffn2_w_out, ffn2_post_g):
    b, s, d = x.shape
    m = b * s
    depth = ffn1_pre_g.shape[0]
    h = x.reshape(m, d)
    for l in range(depth):
        h = _ffn(h, ffn1_pre_g[l][None], ffn1_w_in[l], ffn1_w_out[l], ffn1_post_g[l][None], tm=FFN_TOKEN_TILE, tf=FFN_CHUNK)

        g_mix = mix_pre_g[l][None]
        col_scale = np.ones((w_in.shape[2],), np.float32)
        col_scale[COL_Q_SB * HEAD_DIM:(COL_Q_SB + SB_HEADS) * HEAD_DIM] = HEAD_DIM ** -0.5
        col_scale[COL_Q_NSA * HEAD_DIM:(COL_Q_NSA + NSA_HEADS) * HEAD_DIM] = HEAD_DIM ** -0.5
        wt_bf = (jnp.transpose(w_in[l]) * jnp.asarray(col_scale)[:, None]).astype(BF16)
        gate_end = QKV_COLS + N_GATE_LOGITS
        wt_logits = jnp.concatenate([wt_bf[gate_end:], wt_bf[QKV_COLS:gate_end],
                                     jnp.zeros((GATE_PAD - N_GATE_LOGITS, d), BF16)], axis=0)
        proj = _norm_matmul(h, g_mix, wt_bf, BF16, tm=PROJ_TOKEN_TILE, tn=_col_tile(QKV_COLS), n_out=QKV_COLS,
                            name="in_proj_qkv")
        logits = _norm_matmul(h, g_mix, wt_logits, F32, tm=PROJ_TOKEN_TILE, tn=_col_tile(2 * d + GATE_PAD),
                              n_out=2 * d + GATE_PAD, name="in_proj_logits")
        proj = proj.reshape(b, s, QKV_COLS)

        sb = _sb_attention(proj, tq=SB_QUERY_TILE, heads=SB_HEADS_PER_STEP)

        pos = jnp.stack([cmp_pos_k[l], cmp_pos_v[l]]).reshape(2, 2, CMP_STRIDE, HEAD_DIM)
        w1 = jnp.stack([cmp_k_w1[l], cmp_v_w1[l]]).astype(BF16).reshape(2, 2, CMP_STRIDE, HEAD_DIM, HEAD_DIM)
        w2 = jnp.stack([cmp_k_w2[l], cmp_v_w2[l]]).astype(BF16)
        kvc = _compress(proj, pos, w1, w2)

        nsa = _nsa_attention(proj, kvc,
                             logits.reshape(b, s, 2 * d + GATE_PAD), tq=NSA_QUERY_TILE)

        h = _merge_out(sb.reshape(m, -1), nsa.reshape(m, -1), logits, h,
                       w_branch_sb[l].astype(BF16), w_branch_nsa[l].astype(BF16), w_out[l].astype(BF16),
                       mix_post_g[l][None], tm=MERGE_TOKEN_TILE)

        h = _ffn(h, ffn2_pre_g[l][None], ffn2_w_in[l], ffn2_w_out[l], ffn2_post_g[l][None], tm=FFN_TOKEN_TILE, tf=FFN_CHUNK)
    return h.reshape(b, s, d)
```

```python
import functools

import numpy as np
import jax
import jax.numpy as jnp
from jax import lax
from jax.experimental import pallas as pl
from jax.experimental.pallas import tpu as pltpu

HEAD_DIM = 128
SB_HEADS = 8
NSA_HEADS = 8
NSA_GROUPS = 2
NSA_HPG = NSA_HEADS // NSA_GROUPS
CMP_BLOCK = 32
CMP_STRIDE = 16
SLC_BLOCK = 64
N_SELECT = 16
WINDOW = 512
NORM_EPS = 1e-6
NEG_INF = -1e30
SCORE_SCALE = 2.0 ** 64
FORCED_SCORE = 1e30

QKV_COLS = (3 * SB_HEADS + NSA_HEADS + 6 * NSA_GROUPS) * HEAD_DIM
COL_Q_SB = 0
COL_K_SB = SB_HEADS
COL_V_SB = 2 * SB_HEADS
COL_Q_NSA = 3 * SB_HEADS
COL_K_CMP = COL_Q_NSA + NSA_HEADS
COL_K_SLC = COL_K_CMP + 2 * NSA_GROUPS
COL_V_SLC = COL_K_SLC + NSA_GROUPS
COL_K_WIN = COL_V_SLC + NSA_GROUPS
COL_V_WIN = COL_K_WIN + NSA_GROUPS
N_GATE_LOGITS = 3 * NSA_HEADS
GATE_PAD = 128

LANES = 128
V7X_VMEM_BYTES = 64 * 1024 * 1024
V7X_VMEM_LIMIT = V7X_VMEM_BYTES * 7 // 8
V7X_VMEM_LIMIT_FFN = V7X_VMEM_BYTES * 15 // 16

FFN_TOKEN_TILE = 1024
FFN_CHUNK = 256
PROJ_TOKEN_TILE = 1024
MERGE_TOKEN_TILE = 256
SB_QUERY_TILE = 256
SB_HEADS_PER_STEP = 8
NSA_QUERY_TILE = 512

F32 = jnp.float32
BF16 = jnp.bfloat16
NT_DIMS = (((1,), (1,)), ((), ()))


def _params(semantics, vmem_limit=V7X_VMEM_LIMIT):
    return pltpu.CompilerParams(dimension_semantics=semantics, vmem_limit_bytes=vmem_limit)


def _rms(x, g):
    ms = jnp.mean(x * x, axis=-1, keepdims=True)
    return x * lax.rsqrt(ms + NORM_EPS) * g


def _ffn_body(x_ref, gpre_ref, wg_ref, wu_ref, wo_ref, gpost_ref, o_ref, h_ref):
    j = pl.program_id(1)

    @pl.when(j == 0)
    def _():
        h_ref[...] = _rms(x_ref[...], gpre_ref[...]).astype(BF16)
        o_ref[...] = jnp.zeros_like(o_ref)

    h = h_ref[...]
    gate = jnp.dot(h, wg_ref[...].astype(BF16), preferred_element_type=F32)
    up = jnp.dot(h, wu_ref[...].astype(BF16), preferred_element_type=F32)
    act = (gate * jax.nn.sigmoid(gate)) * up
    o_ref[...] += jnp.dot(act.astype(BF16), wo_ref[...].astype(BF16), preferred_element_type=F32)

    @pl.when(j == pl.num_programs(1) - 1)
    def _():
        o_ref[...] = x_ref[...] + 0.5 * _rms(o_ref[...], gpost_ref[...])


def _ffn(x, g_pre, w_in, w_out, g_post, *, tm, tf):
    m, d = x.shape
    f = w_out.shape[0]
    nf = f // tf
    assert m % tm == 0 and f % tf == 0 and w_in.shape == (d, 2 * f)
    return pl.pallas_call(
        _ffn_body,
        name="ffn",
        grid=(m // tm, nf),
        in_specs=[
            pl.BlockSpec((tm, d), lambda i, j: (i, 0)),
            pl.BlockSpec((1, d), lambda i, j: (0, 0)),
            pl.BlockSpec((d, tf), lambda i, j: (0, j)),
            pl.BlockSpec((d, tf), lambda i, j: (0, j + nf)),
            pl.BlockSpec((tf, d), lambda i, j: (j, 0)),
            pl.BlockSpec((1, d), lambda i, j: (0, 0)),
        ],
        out_specs=pl.BlockSpec((tm, d), lambda i, j: (i, 0)),
        out_shape=jax.ShapeDtypeStruct((m, d), F32),
        scratch_shapes=[pltpu.VMEM((tm, d), BF16)],
        compiler_params=_params(("parallel", "arbitrary"), V7X_VMEM_LIMIT_FFN),
    )(x, g_pre, w_in, w_in, w_out, g_post)


def _norm_matmul_body(x_ref, g_ref, wt_ref, o_ref, h_ref):
    @pl.when(pl.program_id(1) == 0)
    def _():
        h_ref[...] = _rms(x_ref[...], g_ref[...]).astype(BF16)

    o_ref[...] = lax.dot_general(h_ref[...], wt_ref[...], NT_DIMS,
                                 preferred_element_type=F32).astype(o_ref.dtype)


MAX_COL_TILE = 1536


def _col_tile(n):
    assert n % LANES == 0
    blocks = n // LANES
    best = max(k for k in range(1, MAX_COL_TILE // LANES + 1) if blocks % k == 0)
    return best * LANES


def _norm_matmul(x, g, wt, out_dtype, *, tm, tn, n_out, name, col0=0):
    m, d = x.shape
    n = n_out
    assert m % tm == 0 and n % tn == 0 and col0 % tn == 0 and col0 + n <= wt.shape[0]
    jb = col0 // tn
    return pl.pallas_call(
        _norm_matmul_body,
        name=name,
        grid=(m // tm, n // tn),
        in_specs=[
            pl.BlockSpec((tm, d), lambda i, j: (i, 0)),
            pl.BlockSpec((1, d), lambda i, j: (0, 0)),
            pl.BlockSpec((tn, d), lambda i, j: (j + jb, 0)),
        ],
        out_specs=pl.BlockSpec((tm, tn), lambda i, j: (i, j)),
        out_shape=jax.ShapeDtypeStruct((m, n), out_dtype),
        scratch_shapes=[pltpu.VMEM((tm, d), BF16)],
        compiler_params=_params(("parallel", "arbitrary")),
    )(x, g, wt)


EXP_UNDERFLOW = 104.0
TN_DIMS = (((0,), (0,)), ((), ()))
LOG2_E = 1.4426950408889634


def _sb_body(q_ref, k_ref, v_ref, tri_ref, o_ref, c_ref, acc_ref, *, tq, heads):
    i = pl.program_id(2)
    wide = heads * tq
    tri = tri_ref[...]
    key_i = lax.broadcasted_iota(jnp.int32, (tq, wide), 0)
    qry_i = lax.rem(lax.broadcasted_iota(jnp.int32, (tq, wide), 1), tq)
    past = key_i < qry_i
    head_lanes = [slice(h * HEAD_DIM, (h + 1) * HEAD_DIM) for h in range(heads)]

    def tile(j, diag):
        k0 = pl.multiple_of(j * tq, tq)
        z = jnp.concatenate(
            [lax.dot_general(k_ref[0, pl.ds(k0, tq), hl], q_ref[0, :, hl], NT_DIMS,
                             preferred_element_type=F32) for hl in head_lanes], axis=1)
        softplus = jnp.maximum(z, 0.0) + jnp.log(1.0 + jnp.exp2(jnp.abs(z) * (-LOG2_E)))
        log_beta = z - softplus
        if diag:
            softplus = jnp.where(past, softplus, 0.0)
        hi = softplus.astype(BF16)
        lo = (softplus - hi.astype(F32)).astype(BF16)
        later = jnp.dot(tri, jnp.concatenate([hi, lo], axis=0), preferred_element_type=F32)
        c = c_ref[...]
        w = jnp.exp(log_beta - c - later)
        if diag:
            w = jnp.where(past, w, 0.0)
        w = w.astype(BF16)
        for h, hl in enumerate(head_lanes):
            cols = slice(h * tq, (h + 1) * tq)
            acc_ref[:, cols] += lax.dot_general(v_ref[0, pl.ds(k0, tq), hl], w[:, cols], TN_DIMS,
                                                preferred_element_type=F32)
        c = c + jnp.sum(softplus, axis=0, keepdims=True)
        c_ref[...] = c
        return (jnp.min(c) <= EXP_UNDERFLOW).astype(jnp.int32)

    c_ref[...] = jnp.zeros_like(c_ref)
    acc_ref[...] = jnp.zeros_like(acc_ref)
    alive = tile(i, True)

    def keep_going(carry):
        j, alive = carry
        return jnp.logical_and(j >= 0, alive > 0)

    lax.while_loop(keep_going, lambda carry: (carry[0] - 1, tile(carry[0], False)), (i - 1, alive))
    for h, hl in enumerate(head_lanes):
        o_ref[0, :, hl] = acc_ref[:, h * tq:(h + 1) * tq].T.astype(o_ref.dtype)


def _sb_attention(proj, *, tq, heads):
    b, s, _ = proj.shape
    assert s % tq == 0 and SB_HEADS % heads == 0
    upper = np.triu(np.ones((tq, tq), np.float32), 1)
    tri = jnp.asarray(np.concatenate([upper, upper], axis=1), BF16)
    width = heads * HEAD_DIM
    return pl.pallas_call(
        functools.partial(_sb_body, tq=tq, heads=heads),
        name="sb_attn",
        grid=(b, SB_HEADS // heads, s // tq),
        in_specs=[
            pl.BlockSpec((1, tq, width), lambda bb, h, i: (bb, i, COL_Q_SB // heads + h)),
            pl.BlockSpec((1, s, width), lambda bb, h, i: (bb, 0, COL_K_SB // heads + h)),
            pl.BlockSpec((1, s, width), lambda bb, h, i: (bb, 0, COL_V_SB // heads + h)),
            pl.BlockSpec((tq, 2 * tq), lambda bb, h, i: (0, 0)),
        ],
        out_specs=pl.BlockSpec((1, tq, width), lambda bb, h, i: (bb, i, h)),
        out_shape=jax.ShapeDtypeStruct((b, s, SB_HEADS * HEAD_DIM), BF16),
        scratch_shapes=[pltpu.VMEM((1, heads * tq), F32), pltpu.VMEM((HEAD_DIM, heads * tq), F32)],
        compiler_params=_params(("parallel", "parallel", "arbitrary")),
    )(proj, proj, proj, tri)


def _compress_body(x_ref, pos_ref, w1_ref, w2_ref, o_ref, x32_ref):
    n_chunk = o_ref.shape[2]
    x32_ref[...] = x_ref[0].astype(F32)
    first = jnp.zeros((n_chunk, HEAD_DIM), F32)
    second = jnp.zeros((n_chunk, HEAD_DIM), F32)
    for l in range(CMP_STRIDE):
        x_l = x32_ref[pl.ds(l, n_chunk, stride=CMP_STRIDE), :]
        xa = (x_l + pos_ref[0, 0, l:l + 1, :]).astype(BF16)
        xb = (x_l + pos_ref[0, 1, l:l + 1, :]).astype(BF16)
        first = first + jnp.dot(xa, w1_ref[0, 0, l], preferred_element_type=F32)
        second = second + jnp.dot(xb, w1_ref[0, 1, l], preferred_element_type=F32)
    pre = first + pltpu.roll(second, n_chunk - 1, 0)
    y = jax.nn.gelu(pre).astype(BF16)
    o_ref[0, 0] = jnp.dot(y, w2_ref[0], preferred_element_type=F32).astype(o_ref.dtype)


def _compress(proj, pos, w1, w2):
    b, s, _ = proj.shape
    n_chunk = s // CMP_STRIDE
    return pl.pallas_call(
        _compress_body,
        name="compress",
        grid=(2, b * NSA_GROUPS),
        in_specs=[
            pl.BlockSpec((1, s, HEAD_DIM),
                         lambda a, n: (n // NSA_GROUPS, 0, COL_K_CMP + a * NSA_GROUPS + n % NSA_GROUPS)),
            pl.BlockSpec((1, 2, CMP_STRIDE, HEAD_DIM), lambda a, n: (a, 0, 0, 0)),
            pl.BlockSpec((1, 2, CMP_STRIDE, HEAD_DIM, HEAD_DIM), lambda a, n: (a, 0, 0, 0, 0)),
            pl.BlockSpec((1, HEAD_DIM, HEAD_DIM), lambda a, n: (a, 0, 0)),
        ],
        out_specs=pl.BlockSpec((1, 1, n_chunk, HEAD_DIM), lambda a, n: (a, n, 0, 0)),
        out_shape=jax.ShapeDtypeStruct((2, b * NSA_GROUPS, n_chunk, HEAD_DIM), BF16),
        scratch_shapes=[pltpu.VMEM((s, HEAD_DIM), F32)],
        compiler_params=_params(("parallel", "parallel")),
    )(proj, pos, w1, w2)


POS_HI_COL = 0
POS_LO_COL = 1
BLOCK_COL0 = HEAD_DIM // 2
SLAB_PARTS_PER_HEAD = 2


def _key_extra_columns(s, tk):
    pos = np.arange(s)
    extra = np.zeros((s, HEAD_DIM), np.float32)
    extra[:, POS_HI_COL] = (pos % tk) // SLC_BLOCK
    extra[:, POS_LO_COL] = pos % SLC_BLOCK
    extra[pos, BLOCK_COL0 + pos // SLC_BLOCK] = 1.0
    return jnp.asarray(extra, BF16)


def _nsa_body(q_ref, kc_ref, vc_ref, ks_ref, vs_ref, kw_ref, vw_ref, gate_ref,
              pool_ref, kextra_ref, o_ref, ksa_ref, kwa_ref, m_ref, l_ref, acc_ref, half_live_ref,
              *, tq, n_slc):
    g = pl.program_id(1)
    i = pl.program_id(2)
    q0 = i * tq
    n_cmp_pad = kc_ref.shape[2]

    @pl.when(i == 0)
    def _():
        ksa_ref[:, :HEAD_DIM] = ks_ref[0]
        ksa_ref[:, HEAD_DIM:] = kextra_ref[...]
        kwa_ref[:, :HEAD_DIM] = kw_ref[0]
        kwa_ref[:, HEAD_DIM:] = kextra_ref[...]

    slopes = [jnp.where(g == 0, 2.0 ** -(r + 1), 2.0 ** -(r + 1 + NSA_HPG)) for r in range(NSA_HPG)]
    q_heads = [q_ref[0, :, r * HEAD_DIM:(r + 1) * HEAD_DIM] for r in range(NSA_HPG)]
    t_row = q0 + lax.broadcasted_iota(jnp.int32, (1, tq), 1)

    wide = NSA_HPG * tq
    head_of_lane = lax.broadcasted_iota(jnp.int32, (1, wide), 1) // tq
    slope_row = jnp.zeros((1, wide), F32)
    for r in range(NSA_HPG):
        slope_row = jnp.where(head_of_lane == r, slopes[r], slope_row)

    def split_heads(x):
        return [x[:, r * tq:(r + 1) * tq] for r in range(NSA_HPG)]

    kc = kc_ref[0, 0]
    vc = vc_ref[0, 0]
    c_idx = lax.broadcasted_iota(jnp.int32, (n_cmp_pad, wide), 0)
    t_wide = q0 + lax.rem(lax.broadcasted_iota(jnp.int32, (1, wide), 1), tq)
    cmp_end = c_idx * CMP_STRIDE + (CMP_BLOCK - 1)
    valid_c = jnp.logical_and(t_wide >= cmp_end, c_idx < n_cmp_pad - 1)
    s = lax.dot_general(kc, jnp.concatenate(q_heads, axis=0), NT_DIMS, preferred_element_type=F32)
    s = jnp.where(valid_c, s + slope_row * (cmp_end - q0).astype(F32), NEG_INF)
    m = jnp.max(s, axis=0, keepdims=True)
    p = jnp.where(valid_c, jnp.exp(s - m), 0.0)
    l = jnp.sum(p, axis=0, keepdims=True)
    p = p * (1.0 / jnp.where(l > 0.0, l, 1.0))
    o_cmp = split_heads(lax.dot_general(vc, p.astype(BF16), TN_DIMS,
                                        preferred_element_type=F32))
    p_grp = functools.reduce(lambda a, b: a + b, split_heads(p))

    pool = pool_ref[...]
    p_grp = p_grp * SCORE_SCALE
    p1 = p_grp.astype(BF16)
    r1 = p_grp - p1.astype(F32)
    p2 = r1.astype(BF16)
    p3 = (r1 - p2.astype(F32)).astype(BF16)
    score = (jnp.dot(pool, p1, preferred_element_type=F32)
             + jnp.dot(pool, p2, preferred_element_type=F32)
             + jnp.dot(pool, p3, preferred_element_type=F32))
    blk = lax.broadcasted_iota(jnp.int32, (n_slc, tq), 0)
    cur = t_row // SLC_BLOCK
    forced = jnp.logical_or(blk == 0, jnp.logical_or(blk == cur, blk == cur - 1))
    score = jnp.where(forced, FORCED_SCORE, score)
    score = jnp.where(blk <= cur, score, NEG_INF)

    sub = 8
    sub_i = lax.broadcasted_iota(jnp.int32, (sub, tq), 0)
    groups = [score[a:a + sub, :] for a in range(0, n_slc, sub)]
    ranks = [jnp.zeros((sub, tq), F32) for _ in groups]
    for ii in range(n_slc):
        row = score[ii:ii + 1, :]
        for gi, grp in enumerate(groups):
            ge = jnp.where(row >= grp, 1.0, 0.0)
            gt = jnp.where(row > grp, 1.0, 0.0)
            if ii < gi * sub:
                beats = ge
            elif ii >= (gi + 1) * sub:
                beats = gt
            else:
                beats = jnp.where(sub_i > ii - gi * sub, ge, gt)
            ranks[gi] = ranks[gi] + beats
    rank = jnp.concatenate(ranks, axis=0)
    sel_bias_t = jnp.where(rank < float(N_SELECT), 0.0, NEG_INF)

    blocks_per_half = tq // 2 // SLC_BLOCK
    for kh in range(n_slc // blocks_per_half):
        half_bias = sel_bias_t[kh * blocks_per_half:(kh + 1) * blocks_per_half, :]
        half_live_ref[kh] = (jnp.max(half_bias) == 0.0).astype(jnp.int32)

    pieces = [jnp.zeros((BLOCK_COL0, tq), F32), sel_bias_t]
    if BLOCK_COL0 + n_slc < HEAD_DIM:
        pieces.append(jnp.zeros((HEAD_DIM - BLOCK_COL0 - n_slc, tq), F32))
    sel_extra = jnp.concatenate(pieces, axis=0).T
    lane = lax.broadcasted_iota(jnp.int32, (tq, HEAD_DIM), 1)

    def query_slab(r, extra):
        pos_cols = jnp.where(lane == POS_HI_COL, slopes[r] * SLC_BLOCK,
                             jnp.where(lane == POS_LO_COL, slopes[r], extra))
        return jnp.concatenate([q_heads[r], pos_cols.astype(BF16)], axis=1)

    q_slc = jnp.concatenate([query_slab(r, sel_extra) for r in range(NSA_HPG)], axis=0)
    q_win = jnp.concatenate([query_slab(r, 0.0) for r in range(NSA_HPG)], axis=0)

    part_w = tq // SLAB_PARTS_PER_HEAD
    parts = [(r * tq + a * part_w, a) for r in range(NSA_HPG) for a in range(SLAB_PARTS_PER_HEAD)]

    def key_rows(d, a, limit):
        lo = max(0, d * tq + a * part_w - limit)
        hi = min(tq - 1, d * tq + (a + 1) * part_w - 1)
        return (lo // part_w) * part_w, (hi // part_w + 1) * part_w

    def distance_mask(d, a, limit, rows):
        lo, hi = rows
        dist_min = d * tq + a * part_w - (hi - 1)
        dist_max = d * tq + (a + 1) * part_w - 1 - lo
        if dist_min >= 0 and dist_max <= limit:
            return None
        key = lo + lax.broadcasted_iota(jnp.int32, (hi - lo, part_w), 0)
        qry = a * part_w + lax.broadcasted_iota(jnp.int32, (hi - lo, part_w), 1)
        dist = d * tq + qry - key
        return jnp.logical_and(dist >= 0, dist <= limit)

    def scores(ka_ref, q_all, j, rows_of):
        k0 = pl.multiple_of(j * tq, tq)
        out = []
        for c0, a in parts:
            lo, hi = rows_of(a)
            ka = ka_ref[pl.ds(k0 + lo, hi - lo), :]
            out.append(lax.dot_general(ka, q_all[c0:c0 + part_w, :], NT_DIMS, preferred_element_type=F32))
        return out

    def reset():
        m_ref[...] = jnp.full((1, wide), NEG_INF, F32)
        l_ref[...] = jnp.zeros((1, wide), F32)
        acc_ref[...] = jnp.zeros((HEAD_DIM, wide), F32)

    def absorb(v_ref, s_parts, j, rows_of, mask_of, live=None):
        k0 = pl.multiple_of(j * tq, tq)
        tile_off = (k0 - q0).astype(F32)
        for (c0, a), s in zip(parts, s_parts):
            cols = slice(c0, c0 + part_w)
            lo, hi = rows_of(a)
            mask = mask_of(a)
            if mask is not None:
                s = jnp.where(mask, s, NEG_INF)
            if live is not None:
                s = jnp.where(live, s, NEG_INF)
            off = slope_row[:, cols] * tile_off
            m = m_ref[:, cols]
            m_new = jnp.maximum(m, jnp.max(s, axis=0, keepdims=True) + off)
            alpha = jnp.exp(m - m_new)
            p = jnp.exp(s - (m_new - off))
            m_ref[:, cols] = m_new
            l_ref[:, cols] = alpha * l_ref[:, cols] + jnp.sum(p, axis=0, keepdims=True)
            acc_ref[:, cols] = alpha * acc_ref[:, cols] + lax.dot_general(
                v_ref[0, pl.ds(k0 + lo, hi - lo), :], p.astype(BF16), TN_DIMS, preferred_element_type=F32)

    def finish():
        return split_heads(acc_ref[...] * (1.0 / l_ref[...]))

    def tile_plan(d, limit):
        rows_of = lambda a: key_rows(d, a, limit)
        return rows_of, (lambda a: distance_mask(d, a, limit, rows_of(a)))

    all_rows = lambda a: (0, tq)
    no_mask = lambda a: None

    reset()
    rows_of, mask_of = tile_plan(0, tq)
    absorb(vs_ref, scores(ksa_ref, q_slc, i, rows_of), i, rows_of, mask_of)

    @pl.loop(0, i)
    def _(j):
        first_live = half_live_ref[2 * j] > 0
        second_live = half_live_ref[2 * j + 1] > 0
        first_half = lambda a: (0, tq // 2)
        second_half = lambda a: (tq // 2, tq)
        for rows_of, wanted in ((all_rows, jnp.logical_and(first_live, second_live)),
                                (first_half, jnp.logical_and(first_live, jnp.logical_not(second_live))),
                                (second_half, jnp.logical_and(jnp.logical_not(first_live), second_live))):
            pl.when(wanted)(functools.partial(
                lambda rows: absorb(vs_ref, scores(ksa_ref, q_slc, j, rows), j, rows, no_mask), rows_of))

    o_slc = finish()

    reset()
    for d in range((WINDOW - 1) // tq + 2):
        rows_of, mask_of = tile_plan(d, WINDOW - 1)
        j = jnp.maximum(i - d, 0)
        absorb(vw_ref, scores(kwa_ref, q_win, j, rows_of), j, rows_of, mask_of,
               live=None if d == 0 else i - d >= 0)
    o_win = finish()

    sg_t = jax.nn.sigmoid(gate_ref[0]).T

    def gate(branch, r):
        c0 = branch * NSA_HEADS + r
        c1 = c0 + NSA_HPG
        return jnp.where(g == 0, sg_t[c0:c0 + 1, :], sg_t[c1:c1 + 1, :])

    for r in range(NSA_HPG):
        out_t = gate(0, r) * o_cmp[r] + gate(1, r) * o_slc[r] + gate(2, r) * o_win[r]
        o_ref[0, :, r * HEAD_DIM:(r + 1) * HEAD_DIM] = out_t.T.astype(o_ref.dtype)


def _nsa_attention(proj, kvc, gm, *, tq):
    b, s, _ = proj.shape
    n_cmp_pad = s // CMP_STRIDE
    n_slc = s // SLC_BLOCK
    assert s % tq == 0 and tq % SLC_BLOCK == 0 and n_cmp_pad % LANES == 0
    assert BLOCK_COL0 + n_slc <= HEAD_DIM and tq // SLC_BLOCK <= 256 and n_slc % 8 == 0
    ratio = SLC_BLOCK // CMP_STRIDE
    span = CMP_BLOCK // CMP_STRIDE
    pool = np.zeros((n_slc, n_cmp_pad), np.float32)
    for jj in range(n_slc):
        for mm in range(ratio):
            for nn in range(span):
                c = ratio * jj + mm + nn
                if c < n_cmp_pad:
                    pool[jj, c] += 1.0
    grp_w = NSA_HPG * HEAD_DIM
    kv_spec = lambda col: pl.BlockSpec((1, s, HEAD_DIM), lambda bb, g, i: (bb, 0, col + g))
    cmp_spec = lambda a: pl.BlockSpec((1, 1, n_cmp_pad, HEAD_DIM),
                                      lambda bb, g, i: (a, bb * NSA_GROUPS + g, 0, 0))
    return pl.pallas_call(
        functools.partial(_nsa_body, tq=tq, n_slc=n_slc),
        name="nsa_attn",
        grid=(b, NSA_GROUPS, s // tq),
        in_specs=[
            pl.BlockSpec((1, tq, grp_w), lambda bb, g, i: (bb, i, COL_Q_NSA // NSA_HPG + g)),
            cmp_spec(0), cmp_spec(1),
            kv_spec(COL_K_SLC), kv_spec(COL_V_SLC), kv_spec(COL_K_WIN), kv_spec(COL_V_WIN),
            pl.BlockSpec((1, tq, GATE_PAD), lambda bb, g, i: (bb, i, gm.shape[2] // GATE_PAD - 1)),
            pl.BlockSpec((n_slc, n_cmp_pad), lambda bb, g, i: (0, 0)),
            pl.BlockSpec((s, HEAD_DIM), lambda bb, g, i: (0, 0)),
        ],
        out_specs=pl.BlockSpec((1, tq, grp_w), lambda bb, g, i: (bb, i, g)),
        out_shape=jax.ShapeDtypeStruct((b, s, NSA_HEADS * HEAD_DIM), BF16),
        scratch_shapes=[pltpu.VMEM((s, 2 * HEAD_DIM), BF16), pltpu.VMEM((s, 2 * HEAD_DIM), BF16),
                        pltpu.VMEM((1, NSA_HPG * tq), F32), pltpu.VMEM((1, NSA_HPG * tq), F32),
                        pltpu.VMEM((HEAD_DIM, NSA_HPG * tq), F32),
                        pltpu.SMEM((2 * (s // tq),), jnp.int32)],
        compiler_params=_params(("parallel", "parallel", "arbitrary")),
    )(proj, kvc, kvc, proj, proj, proj, proj, gm, jnp.asarray(pool, BF16), _key_extra_columns(s, tq))


def _merge_body(sb_ref, nsa_ref, m0_ref, m1_ref, x_ref, wsb_ref, wnsa_ref, wout_ref, g_ref, o_ref):
    y_sb = jnp.dot(sb_ref[...], wsb_ref[...], preferred_element_type=F32)
    y_nsa = jnp.dot(nsa_ref[...], wnsa_ref[...], preferred_element_type=F32)
    merged = jax.nn.sigmoid(m0_ref[...]) * y_sb + jax.nn.sigmoid(m1_ref[...]) * y_nsa
    y = jnp.dot(merged.astype(BF16), wout_ref[...], preferred_element_type=F32)
    o_ref[...] = x_ref[...] + _rms(y, g_ref[...])


def _merge_out(sb, nsa, gm, x, w_sb, w_nsa, w_out, g_post, *, tm):
    m, d = x.shape
    assert m % tm == 0 and gm.shape[0] == m and gm.shape[1] >= 2 * d

    def resident(shape):
        return pl.BlockSpec(shape, lambda i: (0, 0), pipeline_mode=pl.Buffered(1))

    return pl.pallas_call(
        _merge_body,
        name="merge_out",
        grid=(m // tm,),
        in_specs=[
            pl.BlockSpec((tm, sb.shape[1]), lambda i: (i, 0)),
            pl.BlockSpec((tm, nsa.shape[1]), lambda i: (i, 0)),
            pl.BlockSpec((tm, d), lambda i: (i, 0)),
            pl.BlockSpec((tm, d), lambda i: (i, 1)),
            pl.BlockSpec((tm, d), lambda i: (i, 0)),
            resident(w_sb.shape), resident(w_nsa.shape), resident(w_out.shape),
            pl.BlockSpec((1, d), lambda i: (0, 0)),
        ],
        out_specs=pl.BlockSpec((tm, d), lambda i: (i, 0)),
        out_shape=jax.ShapeDtypeStruct((m, d), F32),
        compiler_params=_params(("parallel",)),
    )(sb, nsa, gm, gm, x, w_sb, w_nsa, w_out, g_post)


def kernel(x, ffn1_pre_g, ffn1_w_in, ffn1_w_out, ffn1_post_g, mix_pre_g, w_in, cmp_pos_k, cmp_k_w1, cmp_k_w2, cmp_pos_v, cmp_v_w1, cmp_v_w2, w_branch_sb, w_branch_nsa, w_out, mix_post_g, ffn2_pre_g, ffn2_w_in, ffn2_w_out, ffn2_post_g):
    b, s, d = x.shape
    m = b * s
    depth = ffn1_pre_g.shape[0]
    h = x.reshape(m, d)
    for l in range(depth):
        h = _ffn(h, ffn1_pre_g[l][None], ffn1_w_in[l].astype(BF16), ffn1_w_out[l].astype(BF16), ffn1_post_g[l][None], tm=FFN_TOKEN_TILE, tf=FFN_CHUNK)

        g_mix = mix_pre_g[l][None]
        col_scale = np.ones((w_in.shape[2],), np.float32)
        col_scale[COL_Q_SB * HEAD_DIM:(COL_Q_SB + SB_HEADS) * HEAD_DIM] = HEAD_DIM ** -0.5
        col_scale[COL_Q_NSA * HEAD_DIM:(COL_Q_NSA + NSA_HEADS) * HEAD_DIM] = HEAD_DIM ** -0.5
        wt_bf = (jnp.transpose(w_in[l]) * jnp.asarray(col_scale)[:, None]).astype(BF16)
        gate_end = QKV_COLS + N_GATE_LOGITS
        wt_logits = jnp.concatenate([wt_bf[gate_end:], wt_bf[QKV_COLS:gate_end],
                                     jnp.zeros((GATE_PAD - N_GATE_LOGITS, d), BF16)], axis=0)
        proj = _norm_matmul(h, g_mix, wt_bf, BF16, tm=PROJ_TOKEN_TILE, tn=_col_tile(QKV_COLS), n_out=QKV_COLS,
                            name="in_proj_qkv")
        logits = _norm_matmul(h, g_mix, wt_logits, F32, tm=PROJ_TOKEN_TILE, tn=_col_tile(2 * d + GATE_PAD),
                              n_out=2 * d + GATE_PAD, name="in_proj_logits")
        proj = proj.reshape(b, s, QKV_COLS)

        sb = _sb_attention(proj, tq=SB_QUERY_TILE, heads=SB_HEADS_PER_STEP)

        pos = jnp.stack([cmp_pos_k[l], cmp_pos_v[l]]).reshape(2, 2, CMP_STRIDE, HEAD_DIM)
        w1 = jnp.stack([cmp_k_w1[l], cmp_v_w1[l]]).astype(BF16).reshape(2, 2, CMP_STRIDE, HEAD_DIM, HEAD_DIM)
        w2 = jnp.stack([cmp_k_w2[l], cmp_v_w2[l]]).astype(BF16)
        kvc = _compress(proj, pos, w1, w2)

        nsa = _nsa_attention(proj, kvc,
                             logits.reshape(b, s, 2 * d + GATE_PAD), tq=NSA_QUERY_TILE)

        h = _merge_out(sb.reshape(m, -1), nsa.reshape(m, -1), logits, h,
                       w_branch_sb[l].astype(BF16), w_branch_nsa[l].astype(BF16), w_out[l].astype(BF16),
                       mix_post_g[l][None], tm=MERGE_TOKEN_TILE)

        h = _ffn(h, ffn2_pre_g[l][None], ffn2_w_in[l].astype(BF16), ffn2_w_out[l].astype(BF16), ffn2_post_g[l][None], tm=FFN_TOKEN_TILE, tf=FFN_CHUNK)
    return h.reshape(b, s, d)
```

```python
import functools

import numpy as np
import jax
import jax.numpy as jnp
from jax import lax
from jax.experimental import pallas as pl
from jax.experimental.pallas import tpu as pltpu

HEAD_DIM = 128
SB_HEADS = 8
NSA_HEADS = 8
NSA_GROUPS = 2
NSA_HPG = NSA_HEADS // NSA_GROUPS
CMP_BLOCK = 32
CMP_STRIDE = 16
SLC_BLOCK = 64
N_SELECT = 16
WINDOW = 512
NORM_EPS = 1e-6
NEG_INF = -1e30
SCORE_SCALE = 2.0 ** 64
FORCED_SCORE = 1e30

QKV_COLS = (3 * SB_HEADS + NSA_HEADS + 6 * NSA_GROUPS) * HEAD_DIM
COL_Q_SB = 0
COL_K_SB = SB_HEADS
COL_V_SB = 2 * SB_HEADS
COL_Q_NSA = 3 * SB_HEADS
COL_K_CMP = COL_Q_NSA + NSA_HEADS
COL_K_SLC = COL_K_CMP + 2 * NSA_GROUPS
COL_V_SLC = COL_K_SLC + NSA_GROUPS
COL_K_WIN = COL_V_SLC + NSA_GROUPS
COL_V_WIN = COL_K_WIN + NSA_GROUPS
N_GATE_LOGITS = 3 * NSA_HEADS
GATE_PAD = 128

LANES = 128
V7X_VMEM_BYTES = 64 * 1024 * 1024
V7X_VMEM_LIMIT = V7X_VMEM_BYTES * 7 // 8
V7X_VMEM_LIMIT_FFN = V7X_VMEM_BYTES * 15 // 16

FFN_TOKEN_TILE = 1024
FFN_CHUNK = 256
PROJ_TOKEN_TILE = 1024
MERGE_TOKEN_TILE = 256
SB_QUERY_TILE = 256
SB_HEADS_PER_STEP = 8
NSA_QUERY_TILE = 512

F32 = jnp.float32
BF16 = jnp.bfloat16
NT_DIMS = (((1,), (1,)), ((), ()))


def _params(semantics, vmem_limit=V7X_VMEM_LIMIT):
    return pltpu.CompilerParams(dimension_semantics=semantics, vmem_limit_bytes=vmem_limit)


def _rms(x, g):
    ms = jnp.mean(x * x, axis=-1, keepdims=True)
    return x * lax.rsqrt(ms + NORM_EPS) * g


def _ffn_body(x_ref, gpre_ref, wg_ref, wu_ref, wo_ref, gpost_ref, o_ref, h_ref):
    j = pl.program_id(1)

    @pl.when(j == 0)
    def _():
        h_ref[...] = _rms(x_ref[...], gpre_ref[...]).astype(BF16)
        o_ref[...] = jnp.zeros_like(o_ref)

    h = h_ref[...]
    gate = jnp.dot(h, wg_ref[...].astype(BF16), preferred_element_type=F32)
    up = jnp.dot(h, wu_ref[...].astype(BF16), preferred_element_type=F32)
    act = (gate * jax.nn.sigmoid(gate)) * up
    o_ref[...] += jnp.dot(act.astype(BF16), wo_ref[...].astype(BF16), preferred_element_type=F32)

    @pl.when(j == pl.num_programs(1) - 1)
    def _():
        o_ref[...] = x_ref[...] + 0.5 * _rms(o_ref[...], gpost_ref[...])


def _ffn(x, g_pre, w_in, w_out, g_post, *, tm, tf):
    m, d = x.shape
    f = w_out.shape[0]
    nf = f // tf
    assert m % tm == 0 and f % tf == 0 and w_in.shape == (d, 2 * f)
    return pl.pallas_call(
        _ffn_body,
        name="ffn",
        grid=(m // tm, nf),
        in_specs=[
            pl.BlockSpec((tm, d), lambda i, j: (i, 0)),
            pl.BlockSpec((1, d), lambda i, j: (0, 0)),
            pl.BlockSpec((d, tf), lambda i, j: (0, j)),
            pl.BlockSpec((d, tf), lambda i, j: (0, j + nf)),
            pl.BlockSpec((tf, d), lambda i, j: (j, 0)),
            pl.BlockSpec((1, d), lambda i, j: (0, 0)),
        ],
        out_specs=pl.BlockSpec((tm, d), lambda i, j: (i, 0)),
        out_shape=jax.ShapeDtypeStruct((m, d), F32),
        scratch_shapes=[pltpu.VMEM((tm, d), BF16)],
        compiler_params=_params(("parallel", "arbitrary"), V7X_VMEM_LIMIT_FFN),
    )(x, g_pre, w_in, w_in, w_out, g_post)


def _norm_matmul_body(x_ref, g_ref, wt_ref, o_ref, h_ref):
    @pl.when(pl.program_id(1) == 0)
    def _():
        h_ref[...] = _rms(x_ref[...], g_ref[...]).astype(BF16)

    o_ref[...] = lax.dot_general(h_ref[...], wt_ref[...], NT_DIMS,
                                 preferred_element_type=F32).astype(o_ref.dtype)


def _matmul_nt_body(h_ref, wt_ref, o_ref):
    o_ref[...] = lax.dot_general(h_ref[...], wt_ref[...], NT_DIMS,
                                 preferred_element_type=F32).astype(o_ref.dtype)


MAX_COL_TILE = 1536


def _col_tile(n):
    assert n % LANES == 0
    blocks = n // LANES
    best = max(k for k in range(1, MAX_COL_TILE // LANES + 1) if blocks % k == 0)
    return best * LANES


def _norm_matmul(x, g, wt, out_dtype, *, tm, tn, n_out, name):
    m, d = x.shape
    assert m % tm == 0 and n_out % tn == 0 and n_out <= wt.shape[0]
    return pl.pallas_call(
        _norm_matmul_body,
        name=name,
        grid=(m // tm, n_out // tn),
        in_specs=[
            pl.BlockSpec((tm, d), lambda i, j: (i, 0)),
            pl.BlockSpec((1, d), lambda i, j: (0, 0)),
            pl.BlockSpec((tn, d), lambda i, j: (j, 0)),
        ],
        out_specs=[pl.BlockSpec((tm, tn), lambda i, j: (i, j)),
                   pl.BlockSpec((tm, d), lambda i, j: (i, 0))],
        out_shape=[jax.ShapeDtypeStruct((m, n_out), out_dtype), jax.ShapeDtypeStruct((m, d), BF16)],
        compiler_params=_params(("parallel", "arbitrary")),
    )(x, g, wt)


def _matmul_nt(h, wt, out_dtype, *, tm, tn, name):
    m, d = h.shape
    n = wt.shape[0]
    assert m % tm == 0 and n % tn == 0
    return pl.pallas_call(
        _matmul_nt_body,
        name=name,
        grid=(m // tm, n // tn),
        in_specs=[
            pl.BlockSpec((tm, d), lambda i, j: (i, 0)),
            pl.BlockSpec((tn, d), lambda i, j: (j, 0)),
        ],
        out_specs=pl.BlockSpec((tm, tn), lambda i, j: (i, j)),
        out_shape=jax.ShapeDtypeStruct((m, n), out_dtype),
        compiler_params=_params(("parallel", "arbitrary")),
    )(h, wt)


EXP_UNDERFLOW = 104.0
TN_DIMS = (((0,), (0,)), ((), ()))
LOG2_E = 1.4426950408889634


def _sb_body(q_ref, k_ref, v_ref, tri_ref, o_ref, c_ref, acc_ref, *, tq, heads):
    i = pl.program_id(2)
    wide = heads * tq
    tri = tri_ref[...]
    key_i = lax.broadcasted_iota(jnp.int32, (tq, wide), 0)
    qry_i = lax.rem(lax.broadcasted_iota(jnp.int32, (tq, wide), 1), tq)
    past = key_i < qry_i
    head_lanes = [slice(h * HEAD_DIM, (h + 1) * HEAD_DIM) for h in range(heads)]

    def tile(j, diag):
        k0 = pl.multiple_of(j * tq, tq)
        z = jnp.concatenate(
            [lax.dot_general(k_ref[0, pl.ds(k0, tq), hl], q_ref[0, :, hl], NT_DIMS,
                             preferred_element_type=F32) for hl in head_lanes], axis=1)
        softplus = jnp.maximum(z, 0.0) + jnp.log(1.0 + jnp.exp2(jnp.abs(z) * (-LOG2_E)))
        log_beta = z - softplus
        if diag:
            softplus = jnp.where(past, softplus, 0.0)
        hi = softplus.astype(BF16)
        lo = (softplus - hi.astype(F32)).astype(BF16)
        later = jnp.dot(tri, jnp.concatenate([hi, lo], axis=0), preferred_element_type=F32)
        c = c_ref[...]
        w = jnp.exp(log_beta - c - later)
        if diag:
            w = jnp.where(past, w, 0.0)
        w = w.astype(BF16)
        for h, hl in enumerate(head_lanes):
            cols = slice(h * tq, (h + 1) * tq)
            acc_ref[:, cols] += lax.dot_general(v_ref[0, pl.ds(k0, tq), hl], w[:, cols], TN_DIMS,
                                                preferred_element_type=F32)
        c = c + jnp.sum(softplus, axis=0, keepdims=True)
        c_ref[...] = c
        return (jnp.min(c) <= EXP_UNDERFLOW).astype(jnp.int32)

    c_ref[...] = jnp.zeros_like(c_ref)
    acc_ref[...] = jnp.zeros_like(acc_ref)
    alive = tile(i, True)

    def keep_going(carry):
        j, alive = carry
        return jnp.logical_and(j >= 0, alive > 0)

    lax.while_loop(keep_going, lambda carry: (carry[0] - 1, tile(carry[0], False)), (i - 1, alive))
    for h, hl in enumerate(head_lanes):
        o_ref[0, :, hl] = acc_ref[:, h * tq:(h + 1) * tq].T.astype(o_ref.dtype)


def _sb_attention(proj, *, tq, heads):
    b, s, _ = proj.shape
    assert s % tq == 0 and SB_HEADS % heads == 0
    upper = np.triu(np.ones((tq, tq), np.float32), 1)
    tri = jnp.asarray(np.concatenate([upper, upper], axis=1), BF16)
    width = heads * HEAD_DIM
    return pl.pallas_call(
        functools.partial(_sb_body, tq=tq, heads=heads),
        name="sb_attn",
        grid=(b, SB_HEADS // heads, s // tq),
        in_specs=[
            pl.BlockSpec((1, tq, width), lambda bb, h, i: (bb, i, COL_Q_SB // heads + h)),
            pl.BlockSpec((1, s, width), lambda bb, h, i: (bb, 0, COL_K_SB // heads + h)),
            pl.BlockSpec((1, s, width), lambda bb, h, i: (bb, 0, COL_V_SB // heads + h)),
            pl.BlockSpec((tq, 2 * tq), lambda bb, h, i: (0, 0)),
        ],
        out_specs=pl.BlockSpec((1, tq, width), lambda bb, h, i: (bb, i, h)),
        out_shape=jax.ShapeDtypeStruct((b, s, SB_HEADS * HEAD_DIM), BF16),
        scratch_shapes=[pltpu.VMEM((1, heads * tq), F32), pltpu.VMEM((HEAD_DIM, heads * tq), F32)],
        compiler_params=_params(("parallel", "parallel", "arbitrary")),
    )(proj, proj, proj, tri)


def _compress_body(x_ref, pos_ref, w1_ref, w2_ref, o_ref, x32_ref):
    n_chunk = o_ref.shape[2]
    x32_ref[...] = x_ref[0].astype(F32)
    first = jnp.zeros((n_chunk, HEAD_DIM), F32)
    second = jnp.zeros((n_chunk, HEAD_DIM), F32)
    for l in range(CMP_STRIDE):
        x_l = x32_ref[pl.ds(l, n_chunk, stride=CMP_STRIDE), :]
        xa = (x_l + pos_ref[0, 0, l:l + 1, :]).astype(BF16)
        xb = (x_l + pos_ref[0, 1, l:l + 1, :]).astype(BF16)
        first = first + jnp.dot(xa, w1_ref[0, 0, l], preferred_element_type=F32)
        second = second + jnp.dot(xb, w1_ref[0, 1, l], preferred_element_type=F32)
    pre = first + pltpu.roll(second, n_chunk - 1, 0)
    y = jax.nn.gelu(pre).astype(BF16)
    o_ref[0, 0] = jnp.dot(y, w2_ref[0], preferred_element_type=F32).astype(o_ref.dtype)


def _compress(proj, pos, w1, w2):
    b, s, _ = proj.shape
    n_chunk = s // CMP_STRIDE
    return pl.pallas_call(
        _compress_body,
        name="compress",
        grid=(2, b * NSA_GROUPS),
        in_specs=[
            pl.BlockSpec((1, s, HEAD_DIM),
                         lambda a, n: (n // NSA_GROUPS, 0, COL_K_CMP + a * NSA_GROUPS + n % NSA_GROUPS)),
            pl.BlockSpec((1, 2, CMP_STRIDE, HEAD_DIM), lambda a, n: (a, 0, 0, 0)),
            pl.BlockSpec((1, 2, CMP_STRIDE, HEAD_DIM, HEAD_DIM), lambda a, n: (a, 0, 0, 0, 0)),
            pl.BlockSpec((1, HEAD_DIM, HEAD_DIM), lambda a, n: (a, 0, 0)),
        ],
        out_specs=pl.BlockSpec((1, 1, n_chunk, HEAD_DIM), lambda a, n: (a, n, 0, 0)),
        out_shape=jax.ShapeDtypeStruct((2, b * NSA_GROUPS, n_chunk, HEAD_DIM), BF16),
        scratch_shapes=[pltpu.VMEM((s, HEAD_DIM), F32)],
        compiler_params=_params(("parallel", "parallel")),
    )(proj, pos, w1, w2)


POS_HI_COL = 0
POS_LO_COL = 1
BLOCK_COL0 = HEAD_DIM // 2
SLAB_PARTS_PER_HEAD = 2


def _key_extra_columns(s, tk):
    pos = np.arange(s)
    extra = np.zeros((s, HEAD_DIM), np.float32)
    extra[:, POS_HI_COL] = (pos % tk) // SLC_BLOCK
    extra[:, POS_LO_COL] = pos % SLC_BLOCK
    extra[pos, BLOCK_COL0 + pos // SLC_BLOCK] = 1.0
    return jnp.asarray(extra, BF16)


def _nsa_body(q_ref, kc_ref, vc_ref, ks_ref, vs_ref, kw_ref, vw_ref, gate_ref,
              pool_ref, kextra_ref, o_ref, ksa_ref, kwa_ref, m_ref, l_ref, acc_ref, half_live_ref,
              *, tq, n_slc):
    g = pl.program_id(1)
    i = pl.program_id(2)
    q0 = i * tq
    n_cmp_pad = kc_ref.shape[2]

    @pl.when(i == 0)
    def _():
        ksa_ref[:, :HEAD_DIM] = ks_ref[0]
        ksa_ref[:, HEAD_DIM:] = kextra_ref[...]
        kwa_ref[:, :HEAD_DIM] = kw_ref[0]
        kwa_ref[:, HEAD_DIM:] = kextra_ref[...]

    slopes = [jnp.where(g == 0, 2.0 ** -(r + 1), 2.0 ** -(r + 1 + NSA_HPG)) for r in range(NSA_HPG)]
    q_heads = [q_ref[0, :, r * HEAD_DIM:(r + 1) * HEAD_DIM] for r in range(NSA_HPG)]
    t_row = q0 + lax.broadcasted_iota(jnp.int32, (1, tq), 1)

    wide = NSA_HPG * tq
    head_of_lane = lax.broadcasted_iota(jnp.int32, (1, wide), 1) // tq
    slope_row = jnp.zeros((1, wide), F32)
    for r in range(NSA_HPG):
        slope_row = jnp.where(head_of_lane == r, slopes[r], slope_row)

    def split_heads(x):
        return [x[:, r * tq:(r + 1) * tq] for r in range(NSA_HPG)]

    kc = kc_ref[0, 0]
    vc = vc_ref[0, 0]
    c_idx = lax.broadcasted_iota(jnp.int32, (n_cmp_pad, wide), 0)
    t_wide = q0 + lax.rem(lax.broadcasted_iota(jnp.int32, (1, wide), 1), tq)
    cmp_end = c_idx * CMP_STRIDE + (CMP_BLOCK - 1)
    valid_c = jnp.logical_and(t_wide >= cmp_end, c_idx < n_cmp_pad - 1)
    s = lax.dot_general(kc, jnp.concatenate(q_heads, axis=0), NT_DIMS, preferred_element_type=F32)
    s = jnp.where(valid_c, s + slope_row * (cmp_end - q0).astype(F32), NEG_INF)
    m = jnp.max(s, axis=0, keepdims=True)
    p = jnp.where(valid_c, jnp.exp(s - m), 0.0)
    l = jnp.sum(p, axis=0, keepdims=True)
    p = p * (1.0 / jnp.where(l > 0.0, l, 1.0))
    o_cmp = split_heads(lax.dot_general(vc, p.astype(BF16), TN_DIMS,
                                        preferred_element_type=F32))
    p_grp = functools.reduce(lambda a, b: a + b, split_heads(p))

    pool = pool_ref[...]
    p_grp = p_grp * SCORE_SCALE
    p1 = p_grp.astype(BF16)
    r1 = p_grp - p1.astype(F32)
    p2 = r1.astype(BF16)
    p3 = (r1 - p2.astype(F32)).astype(BF16)
    score = (jnp.dot(pool, p1, preferred_element_type=F32)
             + jnp.dot(pool, p2, preferred_element_type=F32)
             + jnp.dot(pool, p3, preferred_element_type=F32))
    blk = lax.broadcasted_iota(jnp.int32, (n_slc, tq), 0)
    cur = t_row // SLC_BLOCK
    forced = jnp.logical_or(blk == 0, jnp.logical_or(blk == cur, blk == cur - 1))
    score = jnp.where(forced, FORCED_SCORE, score)
    score = jnp.where(blk <= cur, score, NEG_INF)

    sub = 8
    sub_i = lax.broadcasted_iota(jnp.int32, (sub, tq), 0)
    groups = [score[a:a + sub, :] for a in range(0, n_slc, sub)]
    ranks = [jnp.zeros((sub, tq), F32) for _ in groups]
    for ii in range(n_slc):
        row = score[ii:ii + 1, :]
        for gi, grp in enumerate(groups):
            ge = jnp.where(row >= grp, 1.0, 0.0)
            gt = jnp.where(row > grp, 1.0, 0.0)
            if ii < gi * sub:
                beats = ge
            elif ii >= (gi + 1) * sub:
                beats = gt
            else:
                beats = jnp.where(sub_i > ii - gi * sub, ge, gt)
            ranks[gi] = ranks[gi] + beats
    rank = jnp.concatenate(ranks, axis=0)
    sel_bias_t = jnp.where(rank < float(N_SELECT), 0.0, NEG_INF)

    blocks_per_half = tq // 2 // SLC_BLOCK
    for kh in range(n_slc // blocks_per_half):
        half_bias = sel_bias_t[kh * blocks_per_half:(kh + 1) * blocks_per_half, :]
        half_live_ref[kh] = (jnp.max(half_bias) == 0.0).astype(jnp.int32)

    pieces = [jnp.zeros((BLOCK_COL0, tq), F32), sel_bias_t]
    if BLOCK_COL0 + n_slc < HEAD_DIM:
        pieces.append(jnp.zeros((HEAD_DIM - BLOCK_COL0 - n_slc, tq), F32))
    sel_extra = jnp.concatenate(pieces, axis=0).T
    lane = lax.broadcasted_iota(jnp.int32, (tq, HEAD_DIM), 1)

    def query_slab(r, extra):
        pos_cols = jnp.where(lane == POS_HI_COL, slopes[r] * SLC_BLOCK,
                             jnp.where(lane == POS_LO_COL, slopes[r], extra))
        return jnp.concatenate([q_heads[r], pos_cols.astype(BF16)], axis=1)

    q_slc = jnp.concatenate([query_slab(r, sel_extra) for r in range(NSA_HPG)], axis=0)
    q_win = jnp.concatenate([query_slab(r, 0.0) for r in range(NSA_HPG)], axis=0)

    part_w = tq // SLAB_PARTS_PER_HEAD
    parts = [(r * tq + a * part_w, a) for r in range(NSA_HPG) for a in range(SLAB_PARTS_PER_HEAD)]

    def key_rows(d, a, limit):
        lo = max(0, d * tq + a * part_w - limit)
        hi = min(tq - 1, d * tq + (a + 1) * part_w - 1)
        return (lo // part_w) * part_w, (hi // part_w + 1) * part_w

    def distance_mask(d, a, limit, rows):
        lo, hi = rows
        dist_min = d * tq + a * part_w - (hi - 1)
        dist_max = d * tq + (a + 1) * part_w - 1 - lo
        if dist_min >= 0 and dist_max <= limit:
            return None
        key = lo + lax.broadcasted_iota(jnp.int32, (hi - lo, part_w), 0)
        qry = a * part_w + lax.broadcasted_iota(jnp.int32, (hi - lo, part_w), 1)
        dist = d * tq + qry - key
        return jnp.logical_and(dist >= 0, dist <= limit)

    def scores(ka_ref, q_all, j, rows_of):
        k0 = pl.multiple_of(j * tq, tq)
        out = []
        for c0, a in parts:
            lo, hi = rows_of(a)
            ka = ka_ref[pl.ds(k0 + lo, hi - lo), :]
            out.append(lax.dot_general(ka, q_all[c0:c0 + part_w, :], NT_DIMS, preferred_element_type=F32))
        return out

    def reset():
        m_ref[...] = jnp.full((1, wide), NEG_INF, F32)
        l_ref[...] = jnp.zeros((1, wide), F32)
        acc_ref[...] = jnp.zeros((HEAD_DIM, wide), F32)

    def absorb(v_ref, s_parts, j, rows_of, mask_of, live=None):
        k0 = pl.multiple_of(j * tq, tq)
        tile_off = (k0 - q0).astype(F32)
        for (c0, a), s in zip(parts, s_parts):
            cols = slice(c0, c0 + part_w)
            lo, hi = rows_of(a)
            mask = mask_of(a)
            if mask is not None:
                s = jnp.where(mask, s, NEG_INF)
            if live is not None:
                s = jnp.where(live, s, NEG_INF)
            off = slope_row[:, cols] * tile_off
            m = m_ref[:, cols]
            m_new = jnp.maximum(m, jnp.max(s, axis=0, keepdims=True) + off)
            alpha = jnp.exp(m - m_new)
            p = jnp.exp(s - (m_new - off))
            m_ref[:, cols] = m_new
            l_ref[:, cols] = alpha * l_ref[:, cols] + jnp.sum(p, axis=0, keepdims=True)
            acc_ref[:, cols] = alpha * acc_ref[:, cols] + lax.dot_general(
                v_ref[0, pl.ds(k0 + lo, hi - lo), :], p.astype(BF16), TN_DIMS, preferred_element_type=F32)

    def finish():
        return split_heads(acc_ref[...] * (1.0 / l_ref[...]))

    def tile_plan(d, limit):
        rows_of = lambda a: key_rows(d, a, limit)
        return rows_of, (lambda a: distance_mask(d, a, limit, rows_of(a)))

    all_rows = lambda a: (0, tq)
    no_mask = lambda a: None

    reset()
    rows_of, mask_of = tile_plan(0, tq)
    absorb(vs_ref, scores(ksa_ref, q_slc, i, rows_of), i, rows_of, mask_of)

    @pl.loop(0, i)
    def _(j):
        first_live = half_live_ref[2 * j] > 0
        second_live = half_live_ref[2 * j + 1] > 0
        first_half = lambda a: (0, tq // 2)
        second_half = lambda a: (tq // 2, tq)
        for rows_of, wanted in ((all_rows, jnp.logical_and(first_live, second_live)),
                                (first_half, jnp.logical_and(first_live, jnp.logical_not(second_live))),
                                (second_half, jnp.logical_and(jnp.logical_not(first_live), second_live))):
            pl.when(wanted)(functools.partial(
                lambda rows: absorb(vs_ref, scores(ksa_ref, q_slc, j, rows), j, rows, no_mask), rows_of))

    o_slc = finish()

    reset()
    for d in range((WINDOW - 1) // tq + 2):
        rows_of, mask_of = tile_plan(d, WINDOW - 1)
        j = jnp.maximum(i - d, 0)
        absorb(vw_ref, scores(kwa_ref, q_win, j, rows_of), j, rows_of, mask_of,
               live=None if d == 0 else i - d >= 0)
    o_win = finish()

    sg_t = jax.nn.sigmoid(gate_ref[0]).T

    def gate(branch, r):
        c0 = branch * NSA_HEADS + r
        c1 = c0 + NSA_HPG
        return jnp.where(g == 0, sg_t[c0:c0 + 1, :], sg_t[c1:c1 + 1, :])

    for r in range(NSA_HPG):
        out_t = gate(0, r) * o_cmp[r] + gate(1, r) * o_slc[r] + gate(2, r) * o_win[r]
        o_ref[0, :, r * HEAD_DIM:(r + 1) * HEAD_DIM] = out_t.T.astype(o_ref.dtype)


def _nsa_attention(proj, kvc, gm, *, tq):
    b, s, _ = proj.shape
    n_cmp_pad = s // CMP_STRIDE
    n_slc = s // SLC_BLOCK
    assert s % tq == 0 and tq % SLC_BLOCK == 0 and n_cmp_pad % LANES == 0
    assert BLOCK_COL0 + n_slc <= HEAD_DIM and tq // SLC_BLOCK <= 256 and n_slc % 8 == 0
    ratio = SLC_BLOCK // CMP_STRIDE
    span = CMP_BLOCK // CMP_STRIDE
    pool = np.zeros((n_slc, n_cmp_pad), np.float32)
    for jj in range(n_slc):
        for mm in range(ratio):
            for nn in range(span):
                c = ratio * jj + mm + nn
                if c < n_cmp_pad:
                    pool[jj, c] += 1.0
    grp_w = NSA_HPG * HEAD_DIM
    kv_spec = lambda col: pl.BlockSpec((1, s, HEAD_DIM), lambda bb, g, i: (bb, 0, col + g))
    cmp_spec = lambda a: pl.BlockSpec((1, 1, n_cmp_pad, HEAD_DIM),
                                      lambda bb, g, i: (a, bb * NSA_GROUPS + g, 0, 0))
    return pl.pallas_call(
        functools.partial(_nsa_body, tq=tq, n_slc=n_slc),
        name="nsa_attn",
        grid=(b, NSA_GROUPS, s // tq),
        in_specs=[
            pl.BlockSpec((1, tq, grp_w), lambda bb, g, i: (bb, i, COL_Q_NSA // NSA_HPG + g)),
            cmp_spec(0), cmp_spec(1),
            kv_spec(COL_K_SLC), kv_spec(COL_V_SLC), kv_spec(COL_K_WIN), kv_spec(COL_V_WIN),
            pl.BlockSpec((1, tq, GATE_PAD), lambda bb, g, i: (bb, i, gm.shape[2] // GATE_PAD - 1)),
            pl.BlockSpec((n_slc, n_cmp_pad), lambda bb, g, i: (0, 0)),
            pl.BlockSpec((s, HEAD_DIM), lambda bb, g, i: (0, 0)),
        ],
        out_specs=pl.BlockSpec((1, tq, grp_w), lambda bb, g, i: (bb, i, g)),
        out_shape=jax.ShapeDtypeStruct((b, s, NSA_HEADS * HEAD_DIM), BF16),
        scratch_shapes=[pltpu.VMEM((s, 2 * HEAD_DIM), BF16), pltpu.VMEM((s, 2 * HEAD_DIM), BF16),
                        pltpu.VMEM((1, NSA_HPG * tq), F32), pltpu.VMEM((1, NSA_HPG * tq), F32),
                        pltpu.VMEM((HEAD_DIM, NSA_HPG * tq), F32),
                        pltpu.SMEM((2 * (s // tq),), jnp.int32)],
        compiler_params=_params(("parallel", "parallel", "arbitrary")),
    )(proj, kvc, kvc, proj, proj, proj, proj, gm, jnp.asarray(pool, BF16), _key_extra_columns(s, tq))


def _merge_body(sb_ref, nsa_ref, m0_ref, m1_ref, x_ref, wsb_ref, wnsa_ref, wout_ref, g_ref, o_ref):
    y_sb = jnp.dot(sb_ref[...], wsb_ref[...], preferred_element_type=F32)
    y_nsa = jnp.dot(nsa_ref[...], wnsa_ref[...], preferred_element_type=F32)
    merged = jax.nn.sigmoid(m0_ref[...]) * y_sb + jax.nn.sigmoid(m1_ref[...]) * y_nsa
    y = jnp.dot(merged.astype(BF16), wout_ref[...], preferred_element_type=F32)
    o_ref[...] = x_ref[...] + _rms(y, g_ref[...])


def _merge_out(sb, nsa, gm, x, w_sb, w_nsa, w_out, g_post, *, tm):
    m, d = x.shape
    assert m % tm == 0 and gm.shape[0] == m and gm.shape[1] >= 2 * d

    def resident(shape):
        return pl.BlockSpec(shape, lambda i: (0, 0), pipeline_mode=pl.Buffered(1))

    return pl.pallas_call(
        _merge_body,
        name="merge_out",
        grid=(m // tm,),
        in_specs=[
            pl.BlockSpec((tm, sb.shape[1]), lambda i: (i, 0)),
            pl.BlockSpec((tm, nsa.shape[1]), lambda i: (i, 0)),
            pl.BlockSpec((tm, d), lambda i: (i, 0)),
            pl.BlockSpec((tm, d), lambda i: (i, 1)),
            pl.BlockSpec((tm, d), lambda i: (i, 0)),
            resident(w_sb.shape), resident(w_nsa.shape), resident(w_out.shape),
            pl.BlockSpec((1, d), lambda i: (0, 0)),
        ],
        out_specs=pl.BlockSpec((tm, d), lambda i: (i, 0)),
        out_shape=jax.ShapeDtypeStruct((m, d), F32),
        compiler_params=_params(("parallel",)),
    )(sb, nsa, gm, gm, x, w_sb, w_nsa, w_out, g_post)


def kernel(x, ffn1_pre_g, ffn1_w_in, ffn1_w_out, ffn1_post_g, mix_pre_g, w_in, cmp_pos_k, cmp_k_w1, cmp_k_w2, cmp_pos_v, cmp_v_w1, cmp_v_w2, w_branch_sb, w_branch_nsa, w_out, mix_post_g, ffn2_pre_g, ffn2_w_in, ffn2_w_out, ffn2_post_g):
    b, s, d = x.shape
    m = b * s
    depth = ffn1_pre_g.shape[0]
    h = x.reshape(m, d)
    for l in range(depth):
        h = _ffn(h, ffn1_pre_g[l][None], ffn1_w_in[l], ffn1_w_out[l], ffn1_post_g[l][None], tm=FFN_TOKEN_TILE, tf=FFN_CHUNK)

        g_mix = mix_pre_g[l][None]
        col_scale = np.ones((w_in.shape[2],), np.float32)
        col_scale[COL_Q_SB * HEAD_DIM:(COL_Q_SB + SB_HEADS) * HEAD_DIM] = HEAD_DIM ** -0.5
        col_scale[COL_Q_NSA * HEAD_DIM:(COL_Q_NSA + NSA_HEADS) * HEAD_DIM] = HEAD_DIM ** -0.5
        wt_bf = (jnp.transpose(w_in[l]) * jnp.asarray(col_scale)[:, None]).astype(BF16)
        gate_end = QKV_COLS + N_GATE_LOGITS
        wt_logits = jnp.concatenate([wt_bf[gate_end:], wt_bf[QKV_COLS:gate_end],
                                     jnp.zeros((GATE_PAD - N_GATE_LOGITS, d), BF16)], axis=0)
        proj, h_norm = _norm_matmul(h, g_mix, wt_bf, BF16, tm=PROJ_TOKEN_TILE, tn=_col_tile(QKV_COLS),
                                    n_out=QKV_COLS, name="in_proj_qkv")
        logits = _matmul_nt(h_norm, wt_logits, F32, tm=PROJ_TOKEN_TILE, tn=_col_tile(2 * d + GATE_PAD),
                            name="in_proj_logits")
        proj = proj.reshape(b, s, QKV_COLS)

        sb = _sb_attention(proj, tq=SB_QUERY_TILE, heads=SB_HEADS_PER_STEP)

        pos = jnp.stack([cmp_pos_k[l], cmp_pos_v[l]]).reshape(2, 2, CMP_STRIDE, HEAD_DIM)
        w1 = jnp.stack([cmp_k_w1[l], cmp_v_w1[l]]).astype(BF16).reshape(2, 2, CMP_STRIDE, HEAD_DIM, HEAD_DIM)
        w2 = jnp.stack([cmp_k_w2[l], cmp_v_w2[l]]).astype(BF16)
        kvc = _compress(proj, pos, w1, w2)

        nsa = _nsa_attention(proj, kvc,
                             logits.reshape(b, s, 2 * d + GATE_PAD), tq=NSA_QUERY_TILE)

        h = _merge_out(sb.reshape(m, -1), nsa.reshape(m, -1), logits, h,
                       w_branch_sb[l].astype(BF16), w_branch_nsa[l].astype(BF16), w_out[l].astype(BF16),
                       mix_post_g[l][None], tm=MERGE_TOKEN_TILE)

        h = _ffn(h, ffn2_pre_g[l][None], ffn2_w_in[l], ffn2_w_out[l], ffn2_post_g[l][None], tm=FFN_TOKEN_TILE, tf=FFN_CHUNK)
    return h.reshape(b, s, d)
```

```python
import functools

import numpy as np
import jax
import jax.numpy as jnp
from jax import lax
from jax.experimental import pallas as pl
from jax.experimental.pallas import tpu as pltpu

HEAD_DIM = 128
SB_HEADS = 8
NSA_HEADS = 8
NSA_GROUPS = 2
NSA_HPG = NSA_HEADS // NSA_GROUPS
CMP_BLOCK = 32
CMP_STRIDE = 16
SLC_BLOCK = 64
N_SELECT = 16
WINDOW = 512
NORM_EPS = 1e-6
NEG_INF = -1e30
SCORE_SCALE = 2.0 ** 64
FORCED_SCORE = 1e30

QKV_COLS = (3 * SB_HEADS + NSA_HEADS + 6 * NSA_GROUPS) * HEAD_DIM
COL_Q_SB = 0
COL_K_SB = SB_HEADS
COL_V_SB = 2 * SB_HEADS
COL_Q_NSA = 3 * SB_HEADS
COL_K_CMP = COL_Q_NSA + NSA_HEADS
COL_K_SLC = COL_K_CMP + 2 * NSA_GROUPS
COL_V_SLC = COL_K_SLC + NSA_GROUPS
COL_K_WIN = COL_V_SLC + NSA_GROUPS
COL_V_WIN = COL_K_WIN + NSA_GROUPS
N_GATE_LOGITS = 3 * NSA_HEADS
GATE_PAD = 128

LANES = 128
V7X_VMEM_BYTES = 64 * 1024 * 1024
V7X_VMEM_LIMIT = V7X_VMEM_BYTES * 7 // 8
V7X_VMEM_LIMIT_FFN = V7X_VMEM_BYTES * 15 // 16

FFN_TOKEN_TILE = 1024
FFN_CHUNK = 256
PROJ_TOKEN_TILE = 1024
MERGE_TOKEN_TILE = 256
SB_QUERY_TILE = 256
SB_HEADS_PER_STEP = 8
NSA_QUERY_TILE = 512

F32 = jnp.float32
BF16 = jnp.bfloat16
NT_DIMS = (((1,), (1,)), ((), ()))


def _params(semantics, vmem_limit=V7X_VMEM_LIMIT):
    return pltpu.CompilerParams(dimension_semantics=semantics, vmem_limit_bytes=vmem_limit)


def _rms(x, g):
    ms = jnp.mean(x * x, axis=-1, keepdims=True)
    return x * lax.rsqrt(ms + NORM_EPS) * g


def _ffn_body(x_ref, gpre_ref, wg_ref, wu_ref, wo_ref, gpost_ref, o_ref, h_ref):
    j = pl.program_id(1)

    @pl.when(j == 0)
    def _():
        h_ref[...] = _rms(x_ref[...], gpre_ref[...]).astype(BF16)
        o_ref[...] = jnp.zeros_like(o_ref)

    h = h_ref[...]
    gate = jnp.dot(h, wg_ref[...].astype(BF16), preferred_element_type=F32)
    up = jnp.dot(h, wu_ref[...].astype(BF16), preferred_element_type=F32)
    act = (gate * jax.nn.sigmoid(gate)) * up
    o_ref[...] += jnp.dot(act.astype(BF16), wo_ref[...].astype(BF16), preferred_element_type=F32)

    @pl.when(j == pl.num_programs(1) - 1)
    def _():
        o_ref[...] = x_ref[...] + 0.5 * _rms(o_ref[...], gpost_ref[...])


def _ffn(x, g_pre, w_in, w_out, g_post, *, tm, tf):
    m, d = x.shape
    f = w_out.shape[0]
    nf = f // tf
    assert m % tm == 0 and f % tf == 0 and w_in.shape == (d, 2 * f)
    return pl.pallas_call(
        _ffn_body,
        name="ffn",
        grid=(m // tm, nf),
        in_specs=[
            pl.BlockSpec((tm, d), lambda i, j: (i, 0)),
            pl.BlockSpec((1, d), lambda i, j: (0, 0)),
            pl.BlockSpec((d, tf), lambda i, j: (0, j)),
            pl.BlockSpec((d, tf), lambda i, j: (0, j + nf)),
            pl.BlockSpec((tf, d), lambda i, j: (j, 0)),
            pl.BlockSpec((1, d), lambda i, j: (0, 0)),
        ],
        out_specs=pl.BlockSpec((tm, d), lambda i, j: (i, 0)),
        out_shape=jax.ShapeDtypeStruct((m, d), F32),
        scratch_shapes=[pltpu.VMEM((tm, d), BF16)],
        compiler_params=_params(("parallel", "arbitrary"), V7X_VMEM_LIMIT_FFN),
    )(x, g_pre, w_in, w_in, w_out, g_post)


def _norm_matmul_body(x_ref, g_ref, wt_ref, o_ref, h_ref):
    @pl.when(pl.program_id(1) == 0)
    def _():
        h_ref[...] = _rms(x_ref[...], g_ref[...]).astype(BF16)

    o_ref[...] = lax.dot_general(h_ref[...], wt_ref[...], NT_DIMS,
                                 preferred_element_type=F32).astype(o_ref.dtype)


def _matmul_nt_body(h_ref, wt_ref, o_ref):
    o_ref[...] = lax.dot_general(h_ref[...], wt_ref[...], NT_DIMS,
                                 preferred_element_type=F32).astype(o_ref.dtype)


MAX_COL_TILE = 1536


def _col_tile(n):
    assert n % LANES == 0
    blocks = n // LANES
    best = max(k for k in range(1, MAX_COL_TILE // LANES + 1) if blocks % k == 0)
    return best * LANES


def _norm_matmul(x, g, wt, out_dtype, *, tm, tn, n_out, name):
    m, d = x.shape
    assert m % tm == 0 and n_out % tn == 0 and n_out <= wt.shape[0]
    return pl.pallas_call(
        _norm_matmul_body,
        name=name,
        grid=(m // tm, n_out // tn),
        in_specs=[
            pl.BlockSpec((tm, d), lambda i, j: (i, 0)),
            pl.BlockSpec((1, d), lambda i, j: (0, 0)),
            pl.BlockSpec((tn, d), lambda i, j: (j, 0)),
        ],
        out_specs=[pl.BlockSpec((tm, tn), lambda i, j: (i, j)),
                   pl.BlockSpec((tm, d), lambda i, j: (i, 0))],
        out_shape=[jax.ShapeDtypeStruct((m, n_out), out_dtype), jax.ShapeDtypeStruct((m, d), BF16)],
        compiler_params=_params(("parallel", "arbitrary")),
    )(x, g, wt)


def _matmul_nt(h, wt, out_dtype, *, tm, tn, name):
    m, d = h.shape
    n = wt.shape[0]
    assert m % tm == 0 and n % tn == 0
    return pl.pallas_call(
        _matmul_nt_body,
        name=name,
        grid=(m // tm, n // tn),
        in_specs=[
            pl.BlockSpec((tm, d), lambda i, j: (i, 0)),
            pl.BlockSpec((tn, d), lambda i, j: (j, 0)),
        ],
        out_specs=pl.BlockSpec((tm, tn), lambda i, j: (i, j)),
        out_shape=jax.ShapeDtypeStruct((m, n), out_dtype),
        compiler_params=_params(("parallel", "arbitrary")),
    )(h, wt)


EXP_UNDERFLOW = 104.0
TN_DIMS = (((0,), (0,)), ((), ()))
LOG2_E = 1.4426950408889634


def _sb_body(q_ref, k_ref, v_ref, tri_ref, o_ref, c_ref, acc_ref, *, tq, heads):
    i = pl.program_id(2)
    wide = heads * tq
    tri = tri_ref[...]
    key_i = lax.broadcasted_iota(jnp.int32, (tq, wide), 0)
    qry_i = lax.rem(lax.broadcasted_iota(jnp.int32, (tq, wide), 1), tq)
    past = key_i < qry_i
    head_lanes = [slice(h * HEAD_DIM, (h + 1) * HEAD_DIM) for h in range(heads)]

    def tile(j, diag):
        k0 = pl.multiple_of(j * tq, tq)
        z = jnp.concatenate(
            [lax.dot_general(k_ref[0, pl.ds(k0, tq), hl], q_ref[0, :, hl], NT_DIMS,
                             preferred_element_type=F32) for hl in head_lanes], axis=1)
        softplus = jnp.maximum(z, 0.0) + jnp.log(1.0 + jnp.exp2(jnp.abs(z) * (-LOG2_E)))
        log_beta = z - softplus
        if diag:
            softplus = jnp.where(past, softplus, 0.0)
        hi = softplus.astype(BF16)
        lo = (softplus - hi.astype(F32)).astype(BF16)
        later = jnp.dot(tri, jnp.concatenate([hi, lo], axis=0), preferred_element_type=F32)
        c = c_ref[...]
        w = jnp.exp(log_beta - c - later)
        if diag:
            w = jnp.where(past, w, 0.0)
        w = w.astype(BF16)
        for h, hl in enumerate(head_lanes):
            cols = slice(h * tq, (h + 1) * tq)
            acc_ref[:, cols] += lax.dot_general(v_ref[0, pl.ds(k0, tq), hl], w[:, cols], TN_DIMS,
                                                preferred_element_type=F32)
        c = c + jnp.sum(softplus, axis=0, keepdims=True)
        c_ref[...] = c
        return (jnp.min(c) <= EXP_UNDERFLOW).astype(jnp.int32)

    c_ref[...] = jnp.zeros_like(c_ref)
    acc_ref[...] = jnp.zeros_like(acc_ref)
    alive = tile(i, True)

    def keep_going(carry):
        j, alive = carry
        return jnp.logical_and(j >= 0, alive > 0)

    lax.while_loop(keep_going, lambda carry: (carry[0] - 1, tile(carry[0], False)), (i - 1, alive))
    for h, hl in enumerate(head_lanes):
        o_ref[0, :, hl] = acc_ref[:, h * tq:(h + 1) * tq].T.astype(o_ref.dtype)


def _sb_attention(proj, *, tq, heads):
    b, s, _ = proj.shape
    assert s % tq == 0 and SB_HEADS % heads == 0
    upper = np.triu(np.ones((tq, tq), np.float32), 1)
    tri = jnp.asarray(np.concatenate([upper, upper], axis=1), BF16)
    width = heads * HEAD_DIM
    return pl.pallas_call(
        functools.partial(_sb_body, tq=tq, heads=heads),
        name="sb_attn",
        grid=(b, SB_HEADS // heads, s // tq),
        in_specs=[
            pl.BlockSpec((1, tq, width), lambda bb, h, i: (bb, i, COL_Q_SB // heads + h)),
            pl.BlockSpec((1, s, width), lambda bb, h, i: (bb, 0, COL_K_SB // heads + h)),
            pl.BlockSpec((1, s, width), lambda bb, h, i: (bb, 0, COL_V_SB // heads + h)),
            pl.BlockSpec((tq, 2 * tq), lambda bb, h, i: (0, 0)),
        ],
        out_specs=pl.BlockSpec((1, tq, width), lambda bb, h, i: (bb, i, h)),
        out_shape=jax.ShapeDtypeStruct((b, s, SB_HEADS * HEAD_DIM), BF16),
        scratch_shapes=[pltpu.VMEM((1, heads * tq), F32), pltpu.VMEM((HEAD_DIM, heads * tq), F32)],
        compiler_params=_params(("parallel", "parallel", "arbitrary")),
    )(proj, proj, proj, tri)


def _compress_slab(x_ref, pos_ref, w1_ref, w2_ref, a, x32_ref):
    n_chunk = x32_ref.shape[0] // CMP_STRIDE
    x32_ref[...] = x_ref[0].astype(F32)
    first = jnp.zeros((n_chunk, HEAD_DIM), F32)
    second = jnp.zeros((n_chunk, HEAD_DIM), F32)
    for l in range(CMP_STRIDE):
        x_l = x32_ref[pl.ds(l, n_chunk, stride=CMP_STRIDE), :]
        xa = (x_l + pos_ref[a, 0, l:l + 1, :]).astype(BF16)
        xb = (x_l + pos_ref[a, 1, l:l + 1, :]).astype(BF16)
        first = first + jnp.dot(xa, w1_ref[a, 0, l], preferred_element_type=F32)
        second = second + jnp.dot(xb, w1_ref[a, 1, l], preferred_element_type=F32)
    pre = first + pltpu.roll(second, n_chunk - 1, 0)
    y = jax.nn.gelu(pre).astype(BF16)
    return jnp.dot(y, w2_ref[a], preferred_element_type=F32).astype(BF16)


POS_HI_COL = 0
POS_LO_COL = 1
BLOCK_COL0 = HEAD_DIM // 2
SLAB_PARTS_PER_HEAD = 2


def _key_extra_columns(s, tk):
    pos = np.arange(s)
    extra = np.zeros((s, HEAD_DIM), np.float32)
    extra[:, POS_HI_COL] = (pos % tk) // SLC_BLOCK
    extra[:, POS_LO_COL] = pos % SLC_BLOCK
    extra[pos, BLOCK_COL0 + pos // SLC_BLOCK] = 1.0
    return jnp.asarray(extra, BF16)


def _nsa_body(q_ref, kcmp_ref, vcmp_ref, pos_ref, w1_ref, w2_ref, ks_ref, vs_ref, kw_ref, vw_ref, gate_ref,
              pool_ref, kextra_ref, o_ref, ksa_ref, kwa_ref, m_ref, l_ref, acc_ref, half_live_ref,
              kc_ref, vc_ref, x32_ref, *, tq, n_slc):
    g = pl.program_id(1)
    i = pl.program_id(2)
    q0 = i * tq
    n_cmp_pad = kc_ref.shape[0]

    @pl.when(i == 0)
    def _():
        ksa_ref[:, :HEAD_DIM] = ks_ref[0]
        ksa_ref[:, HEAD_DIM:] = kextra_ref[...]
        kwa_ref[:, :HEAD_DIM] = kw_ref[0]
        kwa_ref[:, HEAD_DIM:] = kextra_ref[...]
        kc_ref[...] = _compress_slab(kcmp_ref, pos_ref, w1_ref, w2_ref, 0, x32_ref)
        vc_ref[...] = _compress_slab(vcmp_ref, pos_ref, w1_ref, w2_ref, 1, x32_ref)

    slopes = [jnp.where(g == 0, 2.0 ** -(r + 1), 2.0 ** -(r + 1 + NSA_HPG)) for r in range(NSA_HPG)]
    q_heads = [q_ref[0, :, r * HEAD_DIM:(r + 1) * HEAD_DIM] for r in range(NSA_HPG)]
    t_row = q0 + lax.broadcasted_iota(jnp.int32, (1, tq), 1)

    wide = NSA_HPG * tq
    head_of_lane = lax.broadcasted_iota(jnp.int32, (1, wide), 1) // tq
    slope_row = jnp.zeros((1, wide), F32)
    for r in range(NSA_HPG):
        slope_row = jnp.where(head_of_lane == r, slopes[r], slope_row)

    def split_heads(x):
        return [x[:, r * tq:(r + 1) * tq] for r in range(NSA_HPG)]

    kc = kc_ref[...]
    vc = vc_ref[...]
    c_idx = lax.broadcasted_iota(jnp.int32, (n_cmp_pad, wide), 0)
    t_wide = q0 + lax.rem(lax.broadcasted_iota(jnp.int32, (1, wide), 1), tq)
    cmp_end = c_idx * CMP_STRIDE + (CMP_BLOCK - 1)
    valid_c = jnp.logical_and(t_wide >= cmp_end, c_idx < n_cmp_pad - 1)
    s = lax.dot_general(kc, jnp.concatenate(q_heads, axis=0), NT_DIMS, preferred_element_type=F32)
    s = jnp.where(valid_c, s + slope_row * (cmp_end - q0).astype(F32), NEG_INF)
    m = jnp.max(s, axis=0, keepdims=True)
    p = jnp.where(valid_c, jnp.exp(s - m), 0.0)
    l = jnp.sum(p, axis=0, keepdims=True)
    p = p * (1.0 / jnp.where(l > 0.0, l, 1.0))
    o_cmp = split_heads(lax.dot_general(vc, p.astype(BF16), TN_DIMS,
                                        preferred_element_type=F32))
    p_grp = functools.reduce(lambda a, b: a + b, split_heads(p))

    pool = pool_ref[...]
    p_grp = p_grp * SCORE_SCALE
    p1 = p_grp.astype(BF16)
    r1 = p_grp - p1.astype(F32)
    p2 = r1.astype(BF16)
    p3 = (r1 - p2.astype(F32)).astype(BF16)
    score = (jnp.dot(pool, p1, preferred_element_type=F32)
             + jnp.dot(pool, p2, preferred_element_type=F32)
             + jnp.dot(pool, p3, preferred_element_type=F32))
    blk = lax.broadcasted_iota(jnp.int32, (n_slc, tq), 0)
    cur = t_row // SLC_BLOCK
    forced = jnp.logical_or(blk == 0, jnp.logical_or(blk == cur, blk == cur - 1))
    score = jnp.where(forced, FORCED_SCORE, score)
    score = jnp.where(blk <= cur, score, NEG_INF)

    sub = 8
    sub_i = lax.broadcasted_iota(jnp.int32, (sub, tq), 0)
    groups = [score[a:a + sub, :] for a in range(0, n_slc, sub)]
    ranks = [jnp.zeros((sub, tq), F32) for _ in groups]
    for ii in range(n_slc):
        row = score[ii:ii + 1, :]
        for gi, grp in enumerate(groups):
            ge = jnp.where(row >= grp, 1.0, 0.0)
            gt = jnp.where(row > grp, 1.0, 0.0)
            if ii < gi * sub:
                beats = ge
            elif ii >= (gi + 1) * sub:
                beats = gt
            else:
                beats = jnp.where(sub_i > ii - gi * sub, ge, gt)
            ranks[gi] = ranks[gi] + beats
    rank = jnp.concatenate(ranks, axis=0)
    sel_bias_t = jnp.where(rank < float(N_SELECT), 0.0, NEG_INF)

    blocks_per_half = tq // 2 // SLC_BLOCK
    for kh in range(n_slc // blocks_per_half):
        half_bias = sel_bias_t[kh * blocks_per_half:(kh + 1) * blocks_per_half, :]
        half_live_ref[kh] = (jnp.max(half_bias) == 0.0).astype(jnp.int32)

    pieces = [jnp.zeros((BLOCK_COL0, tq), F32), sel_bias_t]
    if BLOCK_COL0 + n_slc < HEAD_DIM:
        pieces.append(jnp.zeros((HEAD_DIM - BLOCK_COL0 - n_slc, tq), F32))
    sel_extra = jnp.concatenate(pieces, axis=0).T
    lane = lax.broadcasted_iota(jnp.int32, (tq, HEAD_DIM), 1)

    def query_slab(r, extra):
        pos_cols = jnp.where(lane == POS_HI_COL, slopes[r] * SLC_BLOCK,
                             jnp.where(lane == POS_LO_COL, slopes[r], extra))
        return jnp.concatenate([q_heads[r], pos_cols.astype(BF16)], axis=1)

    q_slc = jnp.concatenate([query_slab(r, sel_extra) for r in range(NSA_HPG)], axis=0)
    q_win = jnp.concatenate([query_slab(r, 0.0) for r in range(NSA_HPG)], axis=0)

    part_w = tq // SLAB_PARTS_PER_HEAD
    parts = [(r * tq + a * part_w, a) for r in range(NSA_HPG) for a in range(SLAB_PARTS_PER_HEAD)]

    def key_rows(d, a, limit):
        lo = max(0, d * tq + a * part_w - limit)
        hi = min(tq - 1, d * tq + (a + 1) * part_w - 1)
        return (lo // part_w) * part_w, (hi // part_w + 1) * part_w

    def distance_mask(d, a, limit, rows):
        lo, hi = rows
        dist_min = d * tq + a * part_w - (hi - 1)
        dist_max = d * tq + (a + 1) * part_w - 1 - lo
        if dist_min >= 0 and dist_max <= limit:
            return None
        key = lo + lax.broadcasted_iota(jnp.int32, (hi - lo, part_w), 0)
        qry = a * part_w + lax.broadcasted_iota(jnp.int32, (hi - lo, part_w), 1)
        dist = d * tq + qry - key
        return jnp.logical_and(dist >= 0, dist <= limit)

    def scores(ka_ref, q_all, j, rows_of):
        k0 = pl.multiple_of(j * tq, tq)
        out = []
        for c0, a in parts:
            lo, hi = rows_of(a)
            ka = ka_ref[pl.ds(k0 + lo, hi - lo), :]
            out.append(lax.dot_general(ka, q_all[c0:c0 + part_w, :], NT_DIMS, preferred_element_type=F32))
        return out

    def reset():
        m_ref[...] = jnp.full((1, wide), NEG_INF, F32)
        l_ref[...] = jnp.zeros((1, wide), F32)
        acc_ref[...] = jnp.zeros((HEAD_DIM, wide), F32)

    def absorb(v_ref, s_parts, j, rows_of, mask_of, live=None):
        k0 = pl.multiple_of(j * tq, tq)
        tile_off = (k0 - q0).astype(F32)
        for (c0, a), s in zip(parts, s_parts):
            cols = slice(c0, c0 + part_w)
            lo, hi = rows_of(a)
            mask = mask_of(a)
            if mask is not None:
                s = jnp.where(mask, s, NEG_INF)
            if live is not None:
                s = jnp.where(live, s, NEG_INF)
            off = slope_row[:, cols] * tile_off
            m = m_ref[:, cols]
            m_new = jnp.maximum(m, jnp.max(s, axis=0, keepdims=True) + off)
            alpha = jnp.exp(m - m_new)
            p = jnp.exp(s - (m_new - off))
            m_ref[:, cols] = m_new
            l_ref[:, cols] = alpha * l_ref[:, cols] + jnp.sum(p, axis=0, keepdims=True)
            acc_ref[:, cols] = alpha * acc_ref[:, cols] + lax.dot_general(
                v_ref[0, pl.ds(k0 + lo, hi - lo), :], p.astype(BF16), TN_DIMS, preferred_element_type=F32)

    def finish():
        return split_heads(acc_ref[...] * (1.0 / l_ref[...]))

    def tile_plan(d, limit):
        rows_of = lambda a: key_rows(d, a, limit)
        return rows_of, (lambda a: distance_mask(d, a, limit, rows_of(a)))

    all_rows = lambda a: (0, tq)
    no_mask = lambda a: None

    reset()
    rows_of, mask_of = tile_plan(0, tq)
    absorb(vs_ref, scores(ksa_ref, q_slc, i, rows_of), i, rows_of, mask_of)

    @pl.loop(0, i)
    def _(j):
        first_live = half_live_ref[2 * j] > 0
        second_live = half_live_ref[2 * j + 1] > 0
        first_half = lambda a: (0, tq // 2)
        second_half = lambda a: (tq // 2, tq)
        for rows_of, wanted in ((all_rows, jnp.logical_and(first_live, second_live)),
                                (first_half, jnp.logical_and(first_live, jnp.logical_not(second_live))),
                                (second_half, jnp.logical_and(jnp.logical_not(first_live), second_live))):
            pl.when(wanted)(functools.partial(
                lambda rows: absorb(vs_ref, scores(ksa_ref, q_slc, j, rows), j, rows, no_mask), rows_of))

    o_slc = finish()

    reset()
    for d in range((WINDOW - 1) // tq + 2):
        rows_of, mask_of = tile_plan(d, WINDOW - 1)
        j = jnp.maximum(i - d, 0)
        absorb(vw_ref, scores(kwa_ref, q_win, j, rows_of), j, rows_of, mask_of,
               live=None if d == 0 else i - d >= 0)
    o_win = finish()

    sg_t = jax.nn.sigmoid(gate_ref[0]).T

    def gate(branch, r):
        c0 = branch * NSA_HEADS + r
        c1 = c0 + NSA_HPG
        return jnp.where(g == 0, sg_t[c0:c0 + 1, :], sg_t[c1:c1 + 1, :])

    for r in range(NSA_HPG):
        out_t = gate(0, r) * o_cmp[r] + gate(1, r) * o_slc[r] + gate(2, r) * o_win[r]
        o_ref[0, :, r * HEAD_DIM:(r + 1) * HEAD_DIM] = out_t.T.astype(o_ref.dtype)


def _nsa_attention(proj, pos, w1, w2, gm, *, tq):
    b, s, _ = proj.shape
    n_cmp_pad = s // CMP_STRIDE
    n_slc = s // SLC_BLOCK
    assert s % tq == 0 and tq % SLC_BLOCK == 0 and n_cmp_pad % LANES == 0
    assert BLOCK_COL0 + n_slc <= HEAD_DIM and tq // SLC_BLOCK <= 256 and n_slc % 8 == 0
    ratio = SLC_BLOCK // CMP_STRIDE
    span = CMP_BLOCK // CMP_STRIDE
    pool = np.zeros((n_slc, n_cmp_pad), np.float32)
    for jj in range(n_slc):
        for mm in range(ratio):
            for nn in range(span):
                c = ratio * jj + mm + nn
                if c < n_cmp_pad:
                    pool[jj, c] += 1.0
    grp_w = NSA_HPG * HEAD_DIM
    kv_spec = lambda col: pl.BlockSpec((1, s, HEAD_DIM), lambda bb, g, i: (bb, 0, col + g))
    whole = lambda arr: pl.BlockSpec(arr.shape, lambda bb, g, i: (0,) * arr.ndim)
    return pl.pallas_call(
        functools.partial(_nsa_body, tq=tq, n_slc=n_slc),
        name="nsa_attn",
        grid=(b, NSA_GROUPS, s // tq),
        in_specs=[
            pl.BlockSpec((1, tq, grp_w), lambda bb, g, i: (bb, i, COL_Q_NSA // NSA_HPG + g)),
            kv_spec(COL_K_CMP), kv_spec(COL_K_CMP + NSA_GROUPS), whole(pos), whole(w1), whole(w2),
            kv_spec(COL_K_SLC), kv_spec(COL_V_SLC), kv_spec(COL_K_WIN), kv_spec(COL_V_WIN),
            pl.BlockSpec((1, tq, GATE_PAD), lambda bb, g, i: (bb, i, gm.shape[2] // GATE_PAD - 1)),
            pl.BlockSpec((n_slc, n_cmp_pad), lambda bb, g, i: (0, 0)),
            pl.BlockSpec((s, HEAD_DIM), lambda bb, g, i: (0, 0)),
        ],
        out_specs=pl.BlockSpec((1, tq, grp_w), lambda bb, g, i: (bb, i, g)),
        out_shape=jax.ShapeDtypeStruct((b, s, NSA_HEADS * HEAD_DIM), BF16),
        scratch_shapes=[pltpu.VMEM((s, 2 * HEAD_DIM), BF16), pltpu.VMEM((s, 2 * HEAD_DIM), BF16),
                        pltpu.VMEM((1, NSA_HPG * tq), F32), pltpu.VMEM((1, NSA_HPG * tq), F32),
                        pltpu.VMEM((HEAD_DIM, NSA_HPG * tq), F32),
                        pltpu.SMEM((2 * (s // tq),), jnp.int32),
                        pltpu.VMEM((n_cmp_pad, HEAD_DIM), BF16), pltpu.VMEM((n_cmp_pad, HEAD_DIM), BF16),
                        pltpu.VMEM((s, HEAD_DIM), F32)],
        compiler_params=_params(("parallel", "parallel", "arbitrary")),
    )(proj, proj, proj, pos, w1, w2, proj, proj, proj, proj, gm, jnp.asarray(pool, BF16), _key_extra_columns(s, tq))


def _merge_body(sb_ref, nsa_ref, m0_ref, m1_ref, x_ref, wsb_ref, wnsa_ref, wout_ref, g_ref, o_ref):
    y_sb = jnp.dot(sb_ref[...], wsb_ref[...], preferred_element_type=F32)
    y_nsa = jnp.dot(nsa_ref[...], wnsa_ref[...], preferred_element_type=F32)
    merged = jax.nn.sigmoid(m0_ref[...]) * y_sb + jax.nn.sigmoid(m1_ref[...]) * y_nsa
    y = jnp.dot(merged.astype(BF16), wout_ref[...], preferred_element_type=F32)
    o_ref[...] = x_ref[...] + _rms(y, g_ref[...])


def _merge_out(sb, nsa, gm, x, w_sb, w_nsa, w_out, g_post, *, tm):
    m, d = x.shape
    assert m % tm == 0 and gm.shape[0] == m and gm.shape[1] >= 2 * d

    def resident(shape):
        return pl.BlockSpec(shape, lambda i: (0, 0), pipeline_mode=pl.Buffered(1))

    return pl.pallas_call(
        _merge_body,
        name="merge_out",
        grid=(m // tm,),
        in_specs=[
            pl.BlockSpec((tm, sb.shape[1]), lambda i: (i, 0)),
            pl.BlockSpec((tm, nsa.shape[1]), lambda i: (i, 0)),
            pl.BlockSpec((tm, d), lambda i: (i, 0)),
            pl.BlockSpec((tm, d), lambda i: (i, 1)),
            pl.BlockSpec((tm, d), lambda i: (i, 0)),
            resident(w_sb.shape), resident(w_nsa.shape), resident(w_out.shape),
            pl.BlockSpec((1, d), lambda i: (0, 0)),
        ],
        out_specs=pl.BlockSpec((tm, d), lambda i: (i, 0)),
        out_shape=jax.ShapeDtypeStruct((m, d), F32),
        compiler_params=_params(("parallel",)),
    )(sb, nsa, gm, gm, x, w_sb, w_nsa, w_out, g_post)


def kernel(x, ffn1_pre_g, ffn1_w_in, ffn1_w_out, ffn1_post_g, mix_pre_g, w_in, cmp_pos_k, cmp_k_w1, cmp_k_w2, cmp_pos_v, cmp_v_w1, cmp_v_w2, w_branch_sb, w_branch_nsa, w_out, mix_post_g, ffn2_pre_g, ffn2_w_in, ffn2_w_out, ffn2_post_g):
    b, s, d = x.shape
    m = b * s
    depth = ffn1_pre_g.shape[0]
    h = x.reshape(m, d)
    for l in range(depth):
        h = _ffn(h, ffn1_pre_g[l][None], ffn1_w_in[l], ffn1_w_out[l], ffn1_post_g[l][None], tm=FFN_TOKEN_TILE, tf=FFN_CHUNK)

        g_mix = mix_pre_g[l][None]
        col_scale = np.ones((w_in.shape[2],), np.float32)
        col_scale[COL_Q_SB * HEAD_DIM:(COL_Q_SB + SB_HEADS) * HEAD_DIM] = HEAD_DIM ** -0.5
        col_scale[COL_Q_NSA * HEAD_DIM:(COL_Q_NSA + NSA_HEADS) * HEAD_DIM] = HEAD_DIM ** -0.5
        wt_bf = (jnp.transpose(w_in[l]) * jnp.asarray(col_scale)[:, None]).astype(BF16)
        gate_end = QKV_COLS + N_GATE_LOGITS
        wt_logits = jnp.concatenate([wt_bf[gate_end:], wt_bf[QKV_COLS:gate_end],
                                     jnp.zeros((GATE_PAD - N_GATE_LOGITS, d), BF16)], axis=0)
        proj, h_norm = _norm_matmul(h, g_mix, wt_bf, BF16, tm=PROJ_TOKEN_TILE, tn=_col_tile(QKV_COLS),
                                    n_out=QKV_COLS, name="in_proj_qkv")
        logits = _matmul_nt(h_norm, wt_logits, F32, tm=PROJ_TOKEN_TILE, tn=_col_tile(2 * d + GATE_PAD),
                            name="in_proj_logits")
        proj = proj.reshape(b, s, QKV_COLS)

        sb = _sb_attention(proj, tq=SB_QUERY_TILE, heads=SB_HEADS_PER_STEP)

        pos = jnp.stack([cmp_pos_k[l], cmp_pos_v[l]]).reshape(2, 2, CMP_STRIDE, HEAD_DIM)
        w1 = jnp.stack([cmp_k_w1[l], cmp_v_w1[l]]).astype(BF16).reshape(2, 2, CMP_STRIDE, HEAD_DIM, HEAD_DIM)
        w2 = jnp.stack([cmp_k_w2[l], cmp_v_w2[l]]).astype(BF16)
        nsa = _nsa_attention(proj, pos, w1, w2,
                             logits.reshape(b, s, 2 * d + GATE_PAD), tq=NSA_QUERY_TILE)

        h = _merge_out(sb.reshape(m, -1), nsa.reshape(m, -1), logits, h,
                       w_branch_sb[l].astype(BF16), w_branch_nsa[l].astype(BF16), w_out[l].astype(BF16),
                       mix_post_g[l][None], tm=MERGE_TOKEN_TILE)

        h = _ffn(h, ffn2_pre_g[l][None], ffn2_w_in[l], ffn2_w_out[l], ffn2_post_g[l][None], tm=FFN_TOKEN_TILE, tf=FFN_CHUNK)
    return h.reshape(b, s, d)
```
